```python
import jax
import jax.numpy as jnp
from jax import lax
import numpy as np

D_MODEL = 1024
BATCH = 16
SEQ = 2048
DEPTH = 1

HEAD_DIM = 64
FOX_HEADS = 8
RWKV_HEADS = 8
FOX_WIDTH = FOX_HEADS * HEAD_DIM
RWKV_WIDTH = RWKV_HEADS * HEAD_DIM
MIX_WIDTH = FOX_WIDTH + RWKV_WIDTH
DECAY_LORA = 64
A_LORA = 64
GATE_LORA = 160
D_FF = ((8 * D_MODEL + 3 * 256 - 1) // (3 * 256)) * 256
Q_BLOCK = 128
RMS_EPS = 1e-6
GN_EPS = 64e-5
L2_EPS = 1e-12

FOX_SIZES = (FOX_WIDTH, FOX_WIDTH, FOX_WIDTH, FOX_HEADS)
RWKV_SIZES = (RWKV_WIDTH, RWKV_WIDTH, RWKV_WIDTH, DECAY_LORA, A_LORA, GATE_LORA)
N_FOX_COLS = sum(FOX_SIZES)
N_RWKV_COLS = sum(RWKV_SIZES)
IN_COLS = N_FOX_COLS + N_RWKV_COLS

kernel_name = 'hymba_fox_rwkv7_adaln_layer'


def _offsets(sizes):
    return [int(o) for o in np.cumsum(sizes)[:-1]]


def rms_norm(x, gain):
    xf = x.astype(jnp.float32)
    y = xf * lax.rsqrt(jnp.mean(xf * xf, axis=-1, keepdims=True) + RMS_EPS)
    return (y * gain.astype(jnp.float32)).astype(x.dtype)


def modulate(h, shift, scale):
    return h * (1 + scale) + shift


def fox_attention(q, k, v, log_f):
    B, T, H, Dh = q.shape
    cum = jnp.transpose(jnp.cumsum(log_f, axis=1), (0, 2, 1))
    qh = jnp.transpose(q, (0, 2, 1, 3))
    kh = jnp.transpose(k, (0, 2, 1, 3))
    vh = jnp.transpose(v, (0, 2, 1, 3))
    scale = Dh ** -0.5
    outs = []
    for i in range(T // Q_BLOCK):
        lo = i * Q_BLOCK
        hi = lo + Q_BLOCK
        s = jnp.einsum('bhqd,bhkd->bhqk', qh[:, :, lo:hi], kh[:, :, :hi]).astype(jnp.float32) * scale
        s = s + (cum[:, :, lo:hi, None] - cum[:, :, None, :hi])
        causal = jnp.arange(hi)[None, :] <= (lo + jnp.arange(Q_BLOCK))[:, None]
        s = jnp.where(causal, s, -jnp.inf)
        p = jax.nn.softmax(s, axis=-1)
        outs.append(jnp.einsum('bhqk,bhkd->bhqd', p.astype(v.dtype), vh[:, :, :hi]))
    o = jnp.concatenate(outs, axis=2)
    return jnp.transpose(o, (0, 2, 1, 3))


def fox_mixer(p, f_bias, q_gain, k_gain):
    B, T, _ = p.shape
    q, k, v, f_logit = jnp.split(p, _offsets(FOX_SIZES), axis=-1)
    q = rms_norm(q.reshape(B, T, FOX_HEADS, HEAD_DIM), q_gain)
    k = rms_norm(k.reshape(B, T, FOX_HEADS, HEAD_DIM), k_gain)
    v = v.reshape(B, T, FOX_HEADS, HEAD_DIM)
    log_f = jax.nn.log_sigmoid((f_logit + f_bias).astype(jnp.float32))
    o = fox_attention(q, k, v, log_f)
    return o.reshape(B, T, FOX_WIDTH)


def rwkv7_scan(r, decay, k, v, a, b):
    B, T, H, N = r.shape

    def step(S, inp):
        r_t, d_t, k_t, v_t, a_t, b_t = inp
        sa = jnp.einsum('bhvk,bhk->bhv', S, a_t)
        S = S * d_t[:, :, None, :] + sa[..., None] * b_t[:, :, None, :] + v_t[..., None] * k_t[:, :, None, :]
        y = jnp.einsum('bhvk,bhk->bhv', S, r_t)
        return S, y

    xs = tuple(jnp.moveaxis(t, 1, 0) for t in (r, decay, k, v, a, b))
    S0 = jnp.zeros((B, H, N, N), jnp.float32)
    _, y = lax.scan(step, S0, xs)
    return jnp.moveaxis(y, 0, 1)


def rwkv_mixer(p, mu, w0, w2, a0, a2, g2, k_k, k_a, r_k, lnx_g, lnx_b):
    B, T, _ = p.shape
    H, N = RWKV_HEADS, HEAD_DIM
    f32 = jnp.float32
    p_prev = jnp.pad(p, ((0, 0), (1, 0), (0, 0)))[:, :-1]
    p = p + (p_prev - p) * mu
    r, k, v, w_dn, a_dn, g_dn = jnp.split(p, _offsets(RWKV_SIZES), axis=-1)
    w = -jax.nn.softplus(-(w0 + jnp.tanh(w_dn) @ w2)) - 0.5
    decay = jnp.exp(-jnp.exp(w.astype(f32)))
    a = jax.nn.sigmoid(a0 + a_dn @ a2)
    g = jax.nn.sigmoid(g_dn) @ g2
    kk = (k * k_k).astype(f32).reshape(B, T, H, N)
    kk = kk / jnp.maximum(jnp.sqrt(jnp.sum(kk * kk, axis=-1, keepdims=True)), L2_EPS)
    k = k * (1 + (a - 1) * k_a)
    rh = r.astype(f32).reshape(B, T, H, N)
    kh = k.astype(f32).reshape(B, T, H, N)
    vh = v.astype(f32).reshape(B, T, H, N)
    ah = a.astype(f32).reshape(B, T, H, N)
    y = rwkv7_scan(rh, decay.reshape(B, T, H, N), kh, vh, -kk, kk * ah)
    mean = jnp.mean(y, axis=-1, keepdims=True)
    var = jnp.mean(jnp.square(y - mean), axis=-1, keepdims=True)
    y = (y - mean) * lax.rsqrt(var + GN_EPS)
    y = y * lnx_g.astype(f32).reshape(H, N) + lnx_b.astype(f32).reshape(H, N)
    y = y + jnp.sum(rh * kh * r_k.astype(f32), axis=-1, keepdims=True) * vh
    return (y.reshape(B, T, RWKV_WIDTH) * g.astype(f32)).astype(p.dtype)


def swiglu(h, w_gate, w_up, w_down):
    return (jax.nn.silu(h @ w_gate) * (h @ w_up)) @ w_down


def setup_inputs(seed: int = 0) -> dict:
    key = jax.random.key(seed)
    ks = jax.random.split(key, 32)
    f32 = jnp.float32
    L, D = DEPTH, D_MODEL

    def nrm(k, shape, scale):
        return jax.random.normal(k, shape, f32) * scale

    return {
        'x': nrm(ks[0], (BATCH, SEQ, D), 1.0),
        'c': nrm(ks[1], (BATCH, D), 1.0),
        'ada_w': nrm(ks[2], (L, D, 6 * D), 0.5 * D ** -0.5),
        'ada_b': nrm(ks[3], (L, 6 * D), 0.02),
        'norm1_g': 1.0 + nrm(ks[4], (L, D), 0.02),
        'norm2_g': 1.0 + nrm(ks[5], (L, D), 0.02),
        'w_in': nrm(ks[6], (L, D, IN_COLS), D ** -0.5),
        'fox_f_bias': jax.random.uniform(ks[7], (L, FOX_HEADS), f32, 1.0, 4.0),
        'fox_q_gain': 1.0 + nrm(ks[8], (L, FOX_HEADS, HEAD_DIM), 0.02),
        'fox_k_gain': 1.0 + nrm(ks[9], (L, FOX_HEADS, HEAD_DIM), 0.02),
        'rwkv_mu': jax.random.uniform(ks[10], (L, N_RWKV_COLS), f32, 0.0, 1.0),
        'rwkv_w0': jax.random.uniform(ks[11], (L, RWKV_WIDTH), f32, -6.0, -1.0),
        'rwkv_w2': nrm(ks[12], (L, DECAY_LORA, RWKV_WIDTH), 0.5 * DECAY_LORA ** -0.5),
        'rwkv_a0': nrm(ks[13], (L, RWKV_WIDTH), 0.1),
        'rwkv_a2': nrm(ks[14], (L, A_LORA, RWKV_WIDTH), A_LORA ** -0.5),
        'rwkv_g2': nrm(ks[15], (L, GATE_LORA, RWKV_WIDTH), GATE_LORA ** -0.5),
        'rwkv_k_k': 0.85 + nrm(ks[16], (L, RWKV_WIDTH), 0.05),
        'rwkv_k_a': 1.0 + nrm(ks[17], (L, RWKV_WIDTH), 0.05),
        'rwkv_r_k': nrm(ks[18], (L, RWKV_HEADS, HEAD_DIM), 0.1),
        'rwkv_lnx_g': 1.0 + nrm(ks[19], (L, RWKV_WIDTH), 0.02),
        'rwkv_lnx_b': nrm(ks[20], (L, RWKV_WIDTH), 0.02),
        'w_out': nrm(ks[21], (L, MIX_WIDTH, D), MIX_WIDTH ** -0.5),
        'ffn_w_gate': nrm(ks[22], (L, D, D_FF), D ** -0.5),
        'ffn_w_up': nrm(ks[23], (L, D, D_FF), D ** -0.5),
        'ffn_w_down': nrm(ks[24], (L, D_FF, D), D_FF ** -0.5),
    }


def reference(x, c, ada_w, ada_b, norm1_g, norm2_g, w_in, fox_f_bias, fox_q_gain, fox_k_gain,
              rwkv_mu, rwkv_w0, rwkv_w2, rwkv_a0, rwkv_a2, rwkv_g2, rwkv_k_k, rwkv_k_a, rwkv_r_k,
              rwkv_lnx_g, rwkv_lnx_b, w_out, ffn_w_gate, ffn_w_up, ffn_w_down):
    cond = jax.nn.silu(c)
    for l in range(DEPTH):
        mod = cond @ ada_w[l] + ada_b[l]
        sh1, sc1, gt1, sh2, sc2, gt2 = jnp.split(mod[:, None, :], 6, axis=-1)
        h = modulate(rms_norm(x, norm1_g[l]), sh1, sc1)
        p = h @ w_in[l]
        p_fox = p[..., :N_FOX_COLS]
        p_rwkv = p[..., N_FOX_COLS:]
        o_fox = fox_mixer(p_fox, fox_f_bias[l], fox_q_gain[l], fox_k_gain[l])
        o_rwkv = rwkv_mixer(p_rwkv, rwkv_mu[l], rwkv_w0[l], rwkv_w2[l], rwkv_a0[l], rwkv_a2[l],
                            rwkv_g2[l], rwkv_k_k[l], rwkv_k_a[l], rwkv_r_k[l],
                            rwkv_lnx_g[l], rwkv_lnx_b[l])
        mix = jnp.concatenate([o_fox, o_rwkv], axis=-1) @ w_out[l]
        x = x + gt1 * mix
        h2 = modulate(rms_norm(x, norm2_g[l]), sh2, sc2)
        x = x + gt2 * swiglu(h2, ffn_w_gate[l], ffn_w_up[l], ffn_w_down[l])
    return x
```

```python
import functools

import jax
import jax.numpy as jnp
from jax import lax
from jax.experimental import pallas as pl
from jax.experimental.pallas import tpu as pltpu

F32 = jnp.float32
BF16 = jnp.bfloat16

HEAD_DIM = 64
FOX_HEADS = 8
RWKV_HEADS = 8
FOX_WIDTH = FOX_HEADS * HEAD_DIM
RWKV_WIDTH = RWKV_HEADS * HEAD_DIM
DECAY_LORA = 64
A_LORA = 64
GATE_LORA = 160
RMS_EPS = 1e-6
GN_EPS = 64e-5
L2_EPS = 1e-12

LANES = 128
PAIRS = FOX_HEADS // 2
RW_COLS = 2048
OFF_WDN, OFF_ADN, OFF_GDN = 1536, 1664, 1792
CHUNK = 128
NEG_BIG = -1e30
VMEM_LIMIT = 56 * 1024 * 1024


def _split_bf16(x, n):
    if x.dtype == BF16:
        return [x]
    parts = []
    r = x
    for i in range(n):
        p = r.astype(BF16)
        parts.append(p)
        if i < n - 1:
            r = r - p.astype(F32)
    return parts


_NN = (((1,), (0,)), ((), ()))
_NT = (((1,), (1,)), ((), ()))
_TN = (((0,), (0,)), ((), ()))


def _mm(a, b, pa=1, pb=1, dims=_NN):
    a_parts = _split_bf16(a, pa)
    b_parts = _split_bf16(b, pb)
    order = max(len(a_parts), len(b_parts))
    out = None
    for i, ai in enumerate(a_parts):
        for j, bj in enumerate(b_parts):
            if i + j >= order:
                continue
            t = lax.dot_general(ai, bj, dims, preferred_element_type=F32)
            out = t if out is None else out + t
    return out


def _log_sigmoid(z):
    return jnp.minimum(z, 0.0) - jnp.log(1.0 + jnp.exp(-jnp.abs(z)))


def _sigmoid(z):
    return 1.0 / (1.0 + jnp.exp(-z))


def _ada_kernel(c_ref, w_ref, b_ref, o_ref):
    c = c_ref[...]
    cond = c * _sigmoid(c)
    o_ref[...] = _mm(cond, w_ref[...], 2, 2) + b_ref[...]


def _ada(c, ada_w, ada_b):
    bsz, d = c.shape
    n = ada_w.shape[1]
    tn = 512
    return pl.pallas_call(
        _ada_kernel,
        grid=(n // tn,),
        in_specs=[
            pl.BlockSpec((bsz, d), lambda j: (0, 0)),
            pl.BlockSpec((d, tn), lambda j: (0, j)),
            pl.BlockSpec((1, tn), lambda j: (0, j)),
        ],
        out_specs=pl.BlockSpec((bsz, tn), lambda j: (0, j)),
        out_shape=jax.ShapeDtypeStruct((bsz, n), F32),
        compiler_params=pltpu.CompilerParams(dimension_semantics=("arbitrary",)),
        name="ada",
    )(c, ada_w, ada_b.reshape(1, n))


def _in_proj_kernel(x_ref, mod_ref, g1_ref, wqkv_ref, wf_ref, wft_ref, wrw_ref,
                    fbrow_ref, fbcol_ref, qg_ref, kg_ref, e_ref, trilo_ref, triup_ref,
                    q_ref, k_ref, v_ref, ccol_ref, crow_ref, rw_ref,
                    carry_row, carry_col, *, tm, tk):
    t = pl.program_id(1)

    @pl.when(t == 0)
    def _():
        carry_row[...] = jnp.zeros_like(carry_row)
        carry_col[...] = jnp.zeros_like(carry_col)

    x = x_ref[0]
    mod = mod_ref[0]
    sh1 = mod[0:1, :]
    sc1 = mod[1:2, :]
    ms = jnp.mean(x * x, axis=-1, keepdims=True)
    y = x * lax.rsqrt(ms + RMS_EPS) * g1_ref[...]
    hb = (y * (1.0 + sc1) + sh1).astype(BF16)

    qkv = _mm(hb, wqkv_ref[...])
    e = e_ref[...]
    q = qkv[:, 0:FOX_WIDTH]
    k = qkv[:, FOX_WIDTH:2 * FOX_WIDTH]
    qms = _mm(q * q, e) * (1.0 / HEAD_DIM)
    kms = _mm(k * k, e) * (1.0 / HEAD_DIM)
    q_ref[0] = (q * lax.rsqrt(qms + RMS_EPS) * qg_ref[...] * (HEAD_DIM ** -0.5)).astype(BF16)
    k_ref[0] = (k * lax.rsqrt(kms + RMS_EPS) * kg_ref[...]).astype(BF16)
    v_ref[0] = qkv[:, 2 * FOX_WIDTH:3 * FOX_WIDTH].astype(BF16)

    rw_ref[0] = _mm(hb, wrw_ref[...])

    lf_col = _log_sigmoid(_mm(hb, wf_ref[...]) + fbrow_ref[...])
    lf_row = _log_sigmoid(_mm(wft_ref[...], hb, dims=_NT) + fbcol_ref[...])
    c_col = _mm(trilo_ref[...], lf_col, 1, 3) + carry_row[...]
    c_row = _mm(lf_row, triup_ref[...], 3, 1) + carry_col[:, 0:1]
    ccol_ref[0] = c_col
    for jj in range(tm // tk):
        crow_ref[0, jj] = c_row[:, jj * tk:(jj + 1) * tk]
    carry_row[...] = c_col[tm - 1:tm, :]
    carry_col[...] = jnp.broadcast_to(c_row[:, tm - 1:tm], carry_col.shape)


def _in_proj(x, mod, g1, wqkv, wf, wft, wrw, fbrow, fbcol, qg, kg, e, tm, tk):
    bsz, seq, d = x.shape
    nt = seq // tm
    row = lax.broadcasted_iota(jnp.int32, (tm, tm), 0)
    col = lax.broadcasted_iota(jnp.int32, (tm, tm), 1)
    trilo = (col <= row).astype(BF16)
    triup = (row <= col).astype(BF16)
    const = lambda shape: pl.BlockSpec(shape, lambda b, t: (0,) * len(shape))
    kern = functools.partial(_in_proj_kernel, tm=tm, tk=tk)
    return pl.pallas_call(
        kern,
        grid=(bsz, nt),
        in_specs=[
            pl.BlockSpec((1, tm, d), lambda b, t: (b, t, 0)),
            pl.BlockSpec((1, 6, d), lambda b, t: (b, 0, 0)),
            const((1, d)),
            const(wqkv.shape), const(wf.shape), const(wft.shape), const(wrw.shape),
            const(fbrow.shape), const(fbcol.shape), const(qg.shape), const(kg.shape),
            const(e.shape), const((tm, tm)), const((tm, tm)),
        ],
        out_specs=[
            pl.BlockSpec((1, tm, FOX_WIDTH), lambda b, t: (b, t, 0)),
            pl.BlockSpec((1, tm, FOX_WIDTH), lambda b, t: (b, t, 0)),
            pl.BlockSpec((1, tm, FOX_WIDTH), lambda b, t: (b, t, 0)),
            pl.BlockSpec((1, tm, LANES), lambda b, t: (b, t, 0)),
            pl.BlockSpec((1, tm // tk, 16, tk), lambda b, t: (b, t, 0, 0)),
            pl.BlockSpec((1, tm, RW_COLS), lambda b, t: (b, t, 0)),
        ],
        out_shape=[
            jax.ShapeDtypeStruct((bsz, seq, FOX_WIDTH), BF16),
            jax.ShapeDtypeStruct((bsz, seq, FOX_WIDTH), BF16),
            jax.ShapeDtypeStruct((bsz, seq, FOX_WIDTH), BF16),
            jax.ShapeDtypeStruct((bsz, seq, LANES), F32),
            jax.ShapeDtypeStruct((bsz, seq // tk, 16, tk), F32),
            jax.ShapeDtypeStruct((bsz, seq, RW_COLS), F32),
        ],
        scratch_shapes=[pltpu.VMEM((1, LANES), F32), pltpu.VMEM((16, LANES), F32)],
        compiler_params=pltpu.CompilerParams(
            dimension_semantics=("arbitrary", "arbitrary"), vmem_limit_bytes=VMEM_LIMIT),
        name="in_proj",
    )(x, mod, g1, wqkv, wf, wft, wrw, fbrow, fbcol, qg, kg, e, trilo, triup)


def _fox_kernel(q_ref, k_ref, v_ref, ccol_ref, crow_ref, o_ref, *, tq, tk):
    i = pl.program_id(1)
    lane = lax.broadcasted_iota(jnp.int32, (tq, LANES), 1)
    low = lane < HEAD_DIM
    row = lax.broadcasted_iota(jnp.int32, (tq, tk), 0)
    col = lax.broadcasted_iota(jnp.int32, (tq, tk), 1)
    causal = col <= row
    ccol = ccol_ref[0]

    for hp in range(PAIRS):
        ls = slice(LANES * hp, LANES * (hp + 1))
        q2 = q_ref[0, :, ls]
        zero = jnp.zeros_like(q2)
        qm = (jnp.where(low, q2, zero), jnp.where(low, zero, q2))
        cc = tuple(ccol[:, 2 * hp + hh:2 * hp + hh + 1] for hh in range(2))

        def step(j, carry, masked, ls=ls, qm=qm, cc=cc, hp=hp):
            ks = pl.multiple_of(j * tk, tk)
            k2 = k_ref[0, pl.ds(ks, tk), ls]
            v2 = v_ref[0, pl.ds(ks, tk), ls]
            new = []
            for hh in range(2):
                m, l, acc = carry[hh]
                s = _mm(qm[hh], k2, dims=_NT)
                cr = crow_ref[0, j, 2 * hp + hh:2 * hp + hh + 1, :]
                s = s + (cc[hh] - cr)
                if masked:
                    s = jnp.where(causal, s, NEG_BIG)
                m_new = jnp.maximum(m, jnp.max(s, axis=1, keepdims=True))
                alpha = jnp.exp(m - m_new)
                p = jnp.exp(s - m_new)
                l_new = alpha * l + jnp.sum(p, axis=1, keepdims=True)
                acc_new = alpha * acc + _mm(p.astype(BF16), v2)
                new.append((m_new, l_new, acc_new))
            return tuple(new)

        init_one = (jnp.full((tq, 1), NEG_BIG, F32), jnp.zeros((tq, 1), F32),
                    jnp.zeros((tq, LANES), F32))
        carry = lax.fori_loop(0, i, lambda j, c: step(j, c, False), (init_one, init_one))
        carry = step(i, carry, True)
        o0 = carry[0][2] / carry[0][1]
        o1 = carry[1][2] / carry[1][1]
        o_ref[0, :, ls] = jnp.where(low, o0, o1).astype(BF16)


def _fox(q, k, v, ccol, crow, tq):
    bsz, seq, _ = q.shape
    kern = functools.partial(_fox_kernel, tq=tq, tk=tq)
    return pl.pallas_call(
        kern,
        grid=(bsz, seq // tq),
        in_specs=[
            pl.BlockSpec((1, tq, FOX_WIDTH), lambda b, i: (b, i, 0)),
            pl.BlockSpec((1, seq, FOX_WIDTH), lambda b, i: (b, 0, 0)),
            pl.BlockSpec((1, seq, FOX_WIDTH), lambda b, i: (b, 0, 0)),
            pl.BlockSpec((1, tq, LANES), lambda b, i: (b, i, 0)),
            pl.BlockSpec((1, seq // tq, 16, tq), lambda b, i: (b, 0, 0, 0)),
        ],
        out_specs=pl.BlockSpec((1, tq, FOX_WIDTH), lambda b, i: (b, i, 0)),
        out_shape=jax.ShapeDtypeStruct((bsz, seq, FOX_WIDTH), BF16),
        compiler_params=pltpu.CompilerParams(
            dimension_semantics=("arbitrary", "arbitrary"), vmem_limit_bytes=VMEM_LIMIT),
        name="fox",
    )(q, k, v, ccol, crow)


def _rwkv_kernel(rw_ref, mu_ref, w0_ref, w2_ref, a0_ref, a2_ref, g2_ref, kk_ref, ka_ref,
                 rk_ref, lg_ref, lb_ref, e_ref, tri_ref, o_ref, prev_ref, s_ref, *, prec):
    cch = CHUNK
    c = pl.program_id(1)

    @pl.when(c == 0)
    def _():
        prev_ref[...] = jnp.zeros_like(prev_ref)
        s_ref[...] = jnp.zeros_like(s_ref)

    p = rw_ref[0]
    rolled = pltpu.roll(p, 1, 0)
    first = lax.broadcasted_iota(jnp.int32, (cch, 1), 0) == 0
    p_prev = jnp.where(first, prev_ref[0:1, :], rolled)
    prev_ref[0:1, :] = p[cch - 1:cch, :]
    ps = p + (p_prev - p) * mu_ref[...]

    w = RWKV_WIDTH
    r = ps[:, 0:w]
    k = ps[:, w:2 * w]
    v = ps[:, 2 * w:3 * w]
    wdn = ps[:, OFF_WDN:OFF_WDN + LANES]
    adn = ps[:, OFF_ADN:OFF_ADN + LANES]
    gdn = ps[:, OFF_GDN:OFF_GDN + 2 * LANES]

    wlog = _log_sigmoid(w0_ref[...] + _mm(jnp.tanh(wdn), w2_ref[...])) - 0.5
    logd = -jnp.exp(wlog)
    a = _sigmoid(a0_ref[...] + _mm(adn, a2_ref[...]))
    g = _mm(_sigmoid(gdn), g2_ref[...])

    e = e_ref[...]
    kk = k * kk_ref[...]
    kk = kk / jnp.maximum(jnp.sqrt(_mm(kk * kk, e, 2, 1)), L2_EPS)
    k = k * (1.0 + (a - 1.0) * ka_ref[...])
    avec = -kk
    bvec = kk * a

    cum = _mm(tri_ref[...], logd, 1, 3)
    clast = cum[cch - 1:cch, :]
    rt = r * jnp.exp(cum)
    at = avec * jnp.exp(cum - logd)
    einv = jnp.exp(-cum)
    bt = bvec * einv
    kt = k * einv
    etail = jnp.exp(clast - cum)
    bh = bvec * etail
    kh = k * etail
    pc = jnp.exp(clast)

    lane = lax.broadcasted_iota(jnp.int32, (cch, LANES), 1)
    low = lane < HEAD_DIM
    ri = lax.broadcasted_iota(jnp.int32, (cch, cch), 0)
    ci = lax.broadcasted_iota(jnp.int32, (cch, cch), 1)
    strict = ci < ri
    incl = ci <= ri
    blockdiag = (ri < HEAD_DIM) == (ci < HEAD_DIM)

    def halves(z):
        zero = jnp.zeros_like(z)
        return jnp.concatenate([jnp.where(low, z, zero), jnp.where(low, zero, z)], axis=0)

    ys = []
    for hp in range(PAIRS):
        ls = slice(LANES * hp, LANES * (hp + 1))
        at_g, rt_g, bt_g, kt_g = at[:, ls], rt[:, ls], bt[:, ls], kt[:, ls]
        v_g, bh_g, kh_g = v[:, ls], bh[:, ls], kh[:, ls]
        rb = jnp.concatenate([bt_g, kt_g], axis=0)
        aak, arb, ark, tms = [], [], [], []
        for hh in range(2):
            sel = low if hh == 0 else jnp.logical_not(low)
            la = jnp.concatenate([jnp.where(sel, at_g, 0.0), jnp.where(sel, rt_g, 0.0)], axis=0)
            gm = _mm(la, rb, prec, prec, dims=_NT)
            aab = jnp.where(strict, gm[0:cch, 0:cch], 0.0)
            aak.append(jnp.where(strict, gm[0:cch, cch:2 * cch], 0.0))
            arb.append(jnp.where(incl, gm[cch:2 * cch, 0:cch], 0.0))
            ark.append(jnp.where(incl, gm[cch:2 * cch, cch:2 * cch], 0.0))
            tm_ = aab
            pw = aab
            for _ in range(cch.bit_length() - 2):
                pw = _mm(pw, pw, prec, prec)
                tm_ = tm_ + pw + _mm(tm_, pw, prec, prec)
            tms.append(tm_)
        sp = s_ref[hp]
        vst = halves(v_g)
        rhs = _mm(at_g, sp, prec, prec, dims=_NT) + _mm(
            jnp.concatenate(aak, axis=1), vst, prec, prec)
        u = rhs + _mm(jnp.concatenate(tms, axis=1), halves(rhs), prec, prec)
        y = _mm(rt_g, sp, prec, prec, dims=_NT) + _mm(
            jnp.concatenate(arb + ark, axis=1),
            jnp.concatenate([halves(u), vst], axis=0), prec, prec)
        uv = jnp.concatenate([u, v_g], axis=0)
        bk = jnp.concatenate([bh_g, kh_g], axis=0)
        upd = _mm(uv.T, bk, prec, prec)
        s_ref[hp] = sp * pc[:, ls] + jnp.where(blockdiag, upd, 0.0)
        ys.append(y)

    y = jnp.concatenate(ys, axis=1)
    inv_n = 1.0 / HEAD_DIM
    mean = _mm(y, e, 2, 1) * inv_n
    d = y - mean
    var = _mm(d * d, e, 2, 1) * inv_n
    yn = d * lax.rsqrt(var + GN_EPS) * lg_ref[...] + lb_ref[...]
    bonus = _mm(r * k * rk_ref[...], e, 2, 1)
    o_ref[0] = ((yn + bonus * v) * g).astype(BF16)


def _rwkv(rw, mu, w0, w2, a0, a2, g2, k_k, k_a, r_k, lnx_g, lnx_b, e, prec):
    bsz, seq, _ = rw.shape
    cch = CHUNK
    row = lax.broadcasted_iota(jnp.int32, (cch, cch), 0)
    col = lax.broadcasted_iota(jnp.int32, (cch, cch), 1)
    tri = (col <= row).astype(BF16)
    const = lambda shape: pl.BlockSpec(shape, lambda b, t: (0,) * len(shape))
    args = (mu, w0, w2, a0, a2, g2, k_k, k_a, r_k, lnx_g, lnx_b, e, tri)
    return pl.pallas_call(
        functools.partial(_rwkv_kernel, prec=prec),
        grid=(bsz, seq // cch),
        in_specs=[pl.BlockSpec((1, cch, RW_COLS), lambda b, t: (b, t, 0))]
        + [const(a.shape) for a in args],
        out_specs=pl.BlockSpec((1, cch, RWKV_WIDTH), lambda b, t: (b, t, 0)),
        out_shape=jax.ShapeDtypeStruct((bsz, seq, RWKV_WIDTH), BF16),
        scratch_shapes=[pltpu.VMEM((8, RW_COLS), F32), pltpu.VMEM((PAIRS, LANES, LANES), F32)],
        compiler_params=pltpu.CompilerParams(
            dimension_semantics=("arbitrary", "arbitrary"), vmem_limit_bytes=VMEM_LIMIT),
        name="rwkv",
    )(rw, *args)


def _out_proj_kernel(of_ref, or_ref, x_ref, mod_ref, g2_ref, wt_ref, wb_ref, x1_ref, h2_ref):
    mod = mod_ref[0]
    gt1, sh2, sc2 = mod[2:3, :], mod[3:4, :], mod[4:5, :]
    mix = _mm(of_ref[0], wt_ref[...]) + _mm(or_ref[0], wb_ref[...])
    x1 = x_ref[0] + gt1 * mix
    x1_ref[0] = x1
    ms = jnp.mean(x1 * x1, axis=-1, keepdims=True)
    y = x1 * lax.rsqrt(ms + RMS_EPS) * g2_ref[...]
    h2_ref[0] = (y * (1.0 + sc2) + sh2).astype(BF16)


def _out_proj(o_fox, o_rwkv, x, mod, g2, w_top, w_bot, tm):
    bsz, seq, d = x.shape
    const = lambda shape: pl.BlockSpec(shape, lambda b, t: (0,) * len(shape))
    return pl.pallas_call(
        _out_proj_kernel,
        grid=(bsz, seq // tm),
        in_specs=[
            pl.BlockSpec((1, tm, FOX_WIDTH), lambda b, t: (b, t, 0)),
            pl.BlockSpec((1, tm, RWKV_WIDTH), lambda b, t: (b, t, 0)),
            pl.BlockSpec((1, tm, d), lambda b, t: (b, t, 0)),
            pl.BlockSpec((1, 6, d), lambda b, t: (b, 0, 0)),
            const((1, d)), const(w_top.shape), const(w_bot.shape),
        ],
        out_specs=[
            pl.BlockSpec((1, tm, d), lambda b, t: (b, t, 0)),
            pl.BlockSpec((1, tm, d), lambda b, t: (b, t, 0)),
        ],
        out_shape=[jax.ShapeDtypeStruct((bsz, seq, d), F32), jax.ShapeDtypeStruct((bsz, seq, d), BF16)],
        compiler_params=pltpu.CompilerParams(
            dimension_semantics=("arbitrary", "arbitrary"), vmem_limit_bytes=VMEM_LIMIT),
        name="out_proj",
    )(o_fox, o_rwkv, x, mod, g2, w_top, w_bot)


def _ffn_kernel(h2_ref, x1_ref, mod_ref, wg_ref, wu_ref, wd_ref, o_ref, acc_ref):
    j = pl.program_id(2)
    h2 = h2_ref[0]
    gate = _mm(h2, wg_ref[...])
    up = _mm(h2, wu_ref[...])
    act = (gate * _sigmoid(gate) * up).astype(BF16)
    part = _mm(act, wd_ref[...])

    @pl.when(j == 0)
    def _():
        acc_ref[...] = part

    @pl.when(j > 0)
    def _():
        acc_ref[...] = acc_ref[...] + part

    @pl.when(j == pl.num_programs(2) - 1)
    def _():
        gt2 = mod_ref[0][5:6, :]
        o_ref[0] = x1_ref[0] + gt2 * acc_ref[...]


def _ffn(h2, x1, mod, wg, wu, wd, tm, tf):
    bsz, seq, d = x1.shape
    dff = wg.shape[1]
    return pl.pallas_call(
        _ffn_kernel,
        grid=(bsz, seq // tm, dff // tf),
        in_specs=[
            pl.BlockSpec((1, tm, d), lambda b, t, j: (b, t, 0)),
            pl.BlockSpec((1, tm, d), lambda b, t, j: (b, t, 0)),
            pl.BlockSpec((1, 6, d), lambda b, t, j: (b, 0, 0)),
            pl.BlockSpec((d, tf), lambda b, t, j: (0, j)),
            pl.BlockSpec((d, tf), lambda b, t, j: (0, j)),
            pl.BlockSpec((tf, d), lambda b, t, j: (j, 0)),
        ],
        out_specs=pl.BlockSpec((1, tm, d), lambda b, t, j: (b, t, 0)),
        out_shape=jax.ShapeDtypeStruct((bsz, seq, d), F32),
        scratch_shapes=[pltpu.VMEM((tm, d), F32)],
        compiler_params=pltpu.CompilerParams(
            dimension_semantics=("arbitrary", "arbitrary", "arbitrary"),
            vmem_limit_bytes=VMEM_LIMIT),
        name="ffn",
    )(h2, x1, mod, wg, wu, wd)


def _pad_cols(w, n):
    return jnp.pad(w, ((0, 0), (0, n - w.shape[1])))


def _pad_rows(w, n):
    return jnp.pad(w, ((0, n - w.shape[0]), (0, 0)))


def _layer(x, mod, norm1_g, norm2_g, w_in, fox_f_bias, fox_q_gain, fox_k_gain, rwkv_mu, rwkv_w0,
           rwkv_w2, rwkv_a0, rwkv_a2, rwkv_g2, rwkv_k_k, rwkv_k_a, rwkv_r_k, rwkv_lnx_g,
           rwkv_lnx_b, w_out, ffn_w_gate, ffn_w_up, ffn_w_down, *, tm, tq, tf, prec):
    bsz, seq, d = x.shape
    w = RWKV_WIDTH
    nfox = 3 * FOX_WIDTH + FOX_HEADS

    wqkv = w_in[:, 0:3 * FOX_WIDTH].astype(BF16)
    wf = _pad_cols(w_in[:, 3 * FOX_WIDTH:nfox], LANES).astype(BF16)
    wft = _pad_rows(w_in[:, 3 * FOX_WIDTH:nfox].T, 16).astype(BF16)
    wr = w_in[:, nfox:]
    o_w, o_a, o_g = 3 * w, 3 * w + DECAY_LORA, 3 * w + DECAY_LORA + A_LORA
    wrw = jnp.concatenate([
        wr[:, 0:3 * w],
        _pad_cols(wr[:, o_w:o_a], LANES),
        _pad_cols(wr[:, o_a:o_g], LANES),
        _pad_cols(wr[:, o_g:], 2 * LANES)], axis=1).astype(BF16)
    mu = rwkv_mu.reshape(1, -1)
    mu_p = jnp.concatenate([
        mu[:, 0:3 * w],
        _pad_cols(mu[:, o_w:o_a], LANES),
        _pad_cols(mu[:, o_a:o_g], LANES),
        _pad_cols(mu[:, o_g:], 2 * LANES)], axis=1)
    fbrow = _pad_cols(fox_f_bias.reshape(1, -1), LANES)
    fbcol = _pad_rows(fox_f_bias.reshape(-1, 1), 16)
    qg = jnp.tile(fox_q_gain, (1, 1)).reshape(1, FOX_WIDTH)
    kg = fox_k_gain.reshape(1, FOX_WIDTH)
    hi = lax.broadcasted_iota(jnp.int32, (w, w), 0) // HEAD_DIM
    hj = lax.broadcasted_iota(jnp.int32, (w, w), 1) // HEAD_DIM
    e = (hi == hj).astype(BF16)

    q, k, v, ccol, crow, rw = _in_proj(
        x, mod, norm1_g.reshape(1, d), wqkv, wf, wft, wrw, fbrow, fbcol, qg, kg, e, tm, tq)
    o_fox = _fox(q, k, v, ccol, crow, tq)
    o_rwkv = _rwkv(
        rw, mu_p, rwkv_w0.reshape(1, w), _pad_rows(rwkv_w2, LANES).astype(BF16),
        rwkv_a0.reshape(1, w), _pad_rows(rwkv_a2, LANES).astype(BF16),
        _pad_rows(rwkv_g2, 2 * LANES).astype(BF16), rwkv_k_k.reshape(1, w), rwkv_k_a.reshape(1, w),
        rwkv_r_k.reshape(1, w), rwkv_lnx_g.reshape(1, w), rwkv_lnx_b.reshape(1, w), e, prec)
    wo = w_out.astype(BF16)
    x1, h2 = _out_proj(o_fox, o_rwkv, x, mod, norm2_g.reshape(1, d),
                       wo[0:FOX_WIDTH], wo[FOX_WIDTH:], tm)
    return _ffn(h2, x1, mod, ffn_w_gate.astype(BF16), ffn_w_up.astype(BF16),
                ffn_w_down.astype(BF16), tm, tf)


def kernel(x, c, ada_w, ada_b, norm1_g, norm2_g, w_in, fox_f_bias, fox_q_gain, fox_k_gain, rwkv_mu,
           rwkv_w0, rwkv_w2, rwkv_a0, rwkv_a2, rwkv_g2, rwkv_k_k, rwkv_k_a, rwkv_r_k, rwkv_lnx_g,
           rwkv_lnx_b, w_out, ffn_w_gate, ffn_w_up, ffn_w_down):
    bsz, seq, d = x.shape
    depth = ada_w.shape[0]
    tm = min(512, seq)
    tq = min(256, seq)
    dff = ffn_w_gate.shape[-1]
    tf = dff // 2 if dff % (2 * LANES) == 0 else dff
    for l in range(depth):
        mod = _ada(c, ada_w[l], ada_b[l]).reshape(bsz, 6, d)
        x = _layer(x, mod, norm1_g[l], norm2_g[l], w_in[l], fox_f_bias[l], fox_q_gain[l],
                   fox_k_gain[l], rwkv_mu[l], rwkv_w0[l], rwkv_w2[l], rwkv_a0[l], rwkv_a2[l],
                   rwkv_g2[l], rwkv_k_k[l], rwkv_k_a[l], rwkv_r_k[l], rwkv_lnx_g[l],
                   rwkv_lnx_b[l], w_out[l], ffn_w_gate[l], ffn_w_up[l], ffn_w_down[l],
                   tm=tm, tq=tq, tf=tf, prec=2)
    return x
```

```python
import functools

import jax
import jax.numpy as jnp
from jax import lax
from jax.experimental import pallas as pl
from jax.experimental.pallas import tpu as pltpu

F32 = jnp.float32
BF16 = jnp.bfloat16

HEAD_DIM = 64
FOX_HEADS = 8
RWKV_HEADS = 8
FOX_WIDTH = FOX_HEADS * HEAD_DIM
RWKV_WIDTH = RWKV_HEADS * HEAD_DIM
DECAY_LORA = 64
A_LORA = 64
GATE_LORA = 160
RMS_EPS = 1e-6
GN_EPS = 64e-5
L2_EPS = 1e-12

LANES = 128
PAIRS = FOX_HEADS // 2
RW_COLS = 2048
OFF_WDN, OFF_ADN, OFF_GDN = 1536, 1664, 1792
CHUNK = 128
NEG_BIG = -1e30
RWKV_PASSES = (1, 2, 1)
FOX_BOUND_MAX = 30.0
VMEM_LIMIT = 56 * 1024 * 1024


def _split_bf16(x, n):
    if x.dtype == BF16:
        return [x]
    parts = []
    r = x
    for i in range(n):
        p = r.astype(BF16)
        parts.append(p)
        if i < n - 1:
            r = r - p.astype(F32)
    return parts


_NN = (((1,), (0,)), ((), ()))
_NT = (((1,), (1,)), ((), ()))
_TN = (((0,), (0,)), ((), ()))


def _mm(a, b, pa=1, pb=1, dims=_NN):
    a_parts = _split_bf16(a, pa)
    b_parts = _split_bf16(b, pb)
    order = max(len(a_parts), len(b_parts))
    out = None
    for i, ai in enumerate(a_parts):
        for j, bj in enumerate(b_parts):
            if i + j >= order:
                continue
            t = lax.dot_general(ai, bj, dims, preferred_element_type=F32)
            out = t if out is None else out + t
    return out


def _log_sigmoid(z):
    return jnp.minimum(z, 0.0) - jnp.log(1.0 + jnp.exp(-jnp.abs(z)))


def _sigmoid(z):
    return 1.0 / (1.0 + jnp.exp(-z))


def _ada_kernel(c_ref, w_ref, b_ref, o_ref):
    c = c_ref[...]
    cond = c * _sigmoid(c)
    o_ref[...] = _mm(cond, w_ref[...], 2, 2) + b_ref[...]


def _ada(c, ada_w, ada_b):
    bsz, d = c.shape
    n = ada_w.shape[1]
    tn = 512
    return pl.pallas_call(
        _ada_kernel,
        grid=(n // tn,),
        in_specs=[
            pl.BlockSpec((bsz, d), lambda j: (0, 0)),
            pl.BlockSpec((d, tn), lambda j: (0, j)),
            pl.BlockSpec((1, tn), lambda j: (0, j)),
        ],
        out_specs=pl.BlockSpec((bsz, tn), lambda j: (0, j)),
        out_shape=jax.ShapeDtypeStruct((bsz, n), F32),
        compiler_params=pltpu.CompilerParams(dimension_semantics=("arbitrary",)),
        name="ada",
    )(c, ada_w, ada_b.reshape(1, n))


def _in_proj_kernel(x_ref, mod_ref, g1_ref, wqkv_ref, wf_ref, wft_ref, wrw_ref,
                    fbrow_ref, fbcol_ref, qg_ref, kg_ref, e_ref, trilo_ref, triup_ref,
                    q_ref, k_ref, v_ref, ccol_ref, crow_ref, rw_ref,
                    carry_row, carry_col, *, tm, tk):
    t = pl.program_id(1)

    @pl.when(t == 0)
    def _():
        carry_row[...] = jnp.zeros_like(carry_row)
        carry_col[...] = jnp.zeros_like(carry_col)

    x = x_ref[0]
    mod = mod_ref[0]
    sh1 = mod[0:1, :]
    sc1 = mod[1:2, :]
    ms = jnp.mean(x * x, axis=-1, keepdims=True)
    y = x * lax.rsqrt(ms + RMS_EPS) * g1_ref[...]
    hb = (y * (1.0 + sc1) + sh1).astype(BF16)

    qkv = _mm(hb, wqkv_ref[...])
    e = e_ref[...]
    q = qkv[:, 0:FOX_WIDTH]
    k = qkv[:, FOX_WIDTH:2 * FOX_WIDTH]
    qms = _mm(q * q, e) * (1.0 / HEAD_DIM)
    kms = _mm(k * k, e) * (1.0 / HEAD_DIM)
    q_ref[0] = (q * lax.rsqrt(qms + RMS_EPS) * qg_ref[...] * (HEAD_DIM ** -0.5)).astype(BF16)
    k_ref[0] = (k * lax.rsqrt(kms + RMS_EPS) * kg_ref[...]).astype(BF16)
    v_ref[0] = qkv[:, 2 * FOX_WIDTH:3 * FOX_WIDTH].astype(BF16)

    rw_ref[0] = _mm(hb, wrw_ref[...])

    lf_col = _log_sigmoid(_mm(hb, wf_ref[...]) + fbrow_ref[...])
    lf_row = _log_sigmoid(_mm(wft_ref[...], hb, dims=_NT) + fbcol_ref[...])
    c_col = _mm(trilo_ref[...], lf_col, 1, 3) + carry_row[...]
    c_row = _mm(lf_row, triup_ref[...], 3, 1) + carry_col[:, 0:1]
    ccol_ref[0] = c_col
    for jj in range(tm // tk):
        crow_ref[0, jj] = c_row[:, jj * tk:(jj + 1) * tk]
    carry_row[...] = c_col[tm - 1:tm, :]
    carry_col[...] = jnp.broadcast_to(c_row[:, tm - 1:tm], carry_col.shape)


def _in_proj(x, mod, g1, wqkv, wf, wft, wrw, fbrow, fbcol, qg, kg, e, tm, tk):
    bsz, seq, d = x.shape
    nt = seq // tm
    row = lax.broadcasted_iota(jnp.int32, (tm, tm), 0)
    col = lax.broadcasted_iota(jnp.int32, (tm, tm), 1)
    trilo = (col <= row).astype(BF16)
    triup = (row <= col).astype(BF16)
    const = lambda shape: pl.BlockSpec(shape, lambda b, t: (0,) * len(shape))
    kern = functools.partial(_in_proj_kernel, tm=tm, tk=tk)
    return pl.pallas_call(
        kern,
        grid=(bsz, nt),
        in_specs=[
            pl.BlockSpec((1, tm, d), lambda b, t: (b, t, 0)),
            pl.BlockSpec((1, 6, d), lambda b, t: (b, 0, 0)),
            const((1, d)),
            const(wqkv.shape), const(wf.shape), const(wft.shape), const(wrw.shape),
            const(fbrow.shape), const(fbcol.shape), const(qg.shape), const(kg.shape),
            const(e.shape), const((tm, tm)), const((tm, tm)),
        ],
        out_specs=[
            pl.BlockSpec((1, tm, FOX_WIDTH), lambda b, t: (b, t, 0)),
            pl.BlockSpec((1, tm, FOX_WIDTH), lambda b, t: (b, t, 0)),
            pl.BlockSpec((1, tm, FOX_WIDTH), lambda b, t: (b, t, 0)),
            pl.BlockSpec((1, tm, LANES), lambda b, t: (b, t, 0)),
            pl.BlockSpec((1, tm // tk, 16, tk), lambda b, t: (b, t, 0, 0)),
            pl.BlockSpec((1, tm, RW_COLS), lambda b, t: (b, t, 0)),
        ],
        out_shape=[
            jax.ShapeDtypeStruct((bsz, seq, FOX_WIDTH), BF16),
            jax.ShapeDtypeStruct((bsz, seq, FOX_WIDTH), BF16),
            jax.ShapeDtypeStruct((bsz, seq, FOX_WIDTH), BF16),
            jax.ShapeDtypeStruct((bsz, seq, LANES), F32),
            jax.ShapeDtypeStruct((bsz, seq // tk, 16, tk), F32),
            jax.ShapeDtypeStruct((bsz, seq, RW_COLS), F32),
        ],
        scratch_shapes=[pltpu.VMEM((1, LANES), F32), pltpu.VMEM((16, LANES), F32)],
        compiler_params=pltpu.CompilerParams(
            dimension_semantics=("arbitrary", "arbitrary"), vmem_limit_bytes=VMEM_LIMIT),
        name="in_proj",
    )(x, mod, g1, wqkv, wf, wft, wrw, fbrow, fbcol, qg, kg, e, trilo, triup)


def _fox_bounded(bound, q_ref, k_ref, v_ref, ccol_ref, crow_ref, o_ref, acc_ref, qm_ref, cb_ref,
                 *, tq, tk):
    i = pl.program_id(1)
    lane = lax.broadcasted_iota(jnp.int32, (tq, LANES), 1)
    low = lane < HEAD_DIM
    row = lax.broadcasted_iota(jnp.int32, (tq, tk), 0)
    col = lax.broadcasted_iota(jnp.int32, (tq, tk), 1)
    causal = col <= row
    klow = lax.broadcasted_iota(jnp.int32, (tk, LANES), 1) < HEAD_DIM
    one_lo = jnp.where(klow, 1.0, 0.0).astype(BF16)
    one_hi = jnp.where(klow, 0.0, 1.0).astype(BF16)

    ccol = ccol_ref[0] - bound
    for hp in range(PAIRS):
        q2 = q_ref[0, :, LANES * hp:LANES * (hp + 1)]
        zero = jnp.zeros_like(q2)
        qm_ref[2 * hp] = jnp.where(low, q2, zero)
        qm_ref[2 * hp + 1] = jnp.where(low, zero, q2)
        for hh in range(2):
            h = 2 * hp + hh
            cb_ref[h] = jnp.broadcast_to(ccol[:, h:h + 1], (tq, LANES))
    acc_ref[...] = jnp.zeros_like(acc_ref)

    def step(j, masked):
        ks = pl.multiple_of(j * tk, tk)
        scores = []
        for hp in range(PAIRS):
            k2 = k_ref[0, pl.ds(ks, tk), LANES * hp:LANES * (hp + 1)]
            for hh in range(2):
                scores.append(_mm(qm_ref[2 * hp + hh], k2, dims=_NT))
        probs = []
        for h in range(FOX_HEADS):
            cb = cb_ref[h]
            bias = jnp.concatenate([cb] * (tk // LANES), axis=1) - crow_ref[0, j, h:h + 1, :]
            s = scores[h] + bias
            if masked:
                s = jnp.where(causal, s, NEG_BIG)
            probs.append(jnp.exp(s).astype(BF16))
        for hp in range(PAIRS):
            v2 = v_ref[0, pl.ds(ks, tk), LANES * hp:LANES * (hp + 1)]
            zero = jnp.zeros_like(v2)
            vaug = jnp.concatenate([
                jnp.concatenate([jnp.where(klow, v2, zero), one_lo], axis=1),
                jnp.concatenate([jnp.where(klow, zero, v2), one_hi], axis=1)], axis=0)
            acc_ref[hp] += _mm(jnp.concatenate([probs[2 * hp], probs[2 * hp + 1]], axis=1), vaug)

    def body(j, carry):
        step(j, False)
        return carry

    lax.fori_loop(0, i, body, 0)
    step(i, True)
    for hp in range(PAIRS):
        a = acc_ref[hp]
        o = a[:, 0:LANES] / a[:, LANES:2 * LANES]
        o_ref[0, :, LANES * hp:LANES * (hp + 1)] = o.astype(BF16)


def _fox_kernel(flag_ref, bound_ref, q_ref, k_ref, v_ref, ccol_ref, crow_ref, o_ref, acc_ref,
                qm_ref, cb_ref, *, tq, tk):
    @pl.when(flag_ref[0] == 1)
    def _():
        _fox_bounded(bound_ref[0], q_ref, k_ref, v_ref, ccol_ref, crow_ref, o_ref, acc_ref,
                     qm_ref, cb_ref, tq=tq, tk=tk)

    @pl.when(flag_ref[0] == 0)
    def _():
        _fox_running_max(q_ref, k_ref, v_ref, ccol_ref, crow_ref, o_ref, tq=tq, tk=tk)


def _fox_running_max(q_ref, k_ref, v_ref, ccol_ref, crow_ref, o_ref, *, tq, tk):
    i = pl.program_id(1)
    lane = lax.broadcasted_iota(jnp.int32, (tq, LANES), 1)
    low = lane < HEAD_DIM
    row = lax.broadcasted_iota(jnp.int32, (tq, tk), 0)
    col = lax.broadcasted_iota(jnp.int32, (tq, tk), 1)
    causal = col <= row
    ccol = ccol_ref[0]

    for hp in range(PAIRS):
        ls = slice(LANES * hp, LANES * (hp + 1))
        q2 = q_ref[0, :, ls]
        zero = jnp.zeros_like(q2)
        qm = (jnp.where(low, q2, zero), jnp.where(low, zero, q2))
        cc = tuple(ccol[:, 2 * hp + hh:2 * hp + hh + 1] for hh in range(2))

        def step(j, carry, masked, ls=ls, qm=qm, cc=cc, hp=hp):
            ks = pl.multiple_of(j * tk, tk)
            k2 = k_ref[0, pl.ds(ks, tk), ls]
            v2 = v_ref[0, pl.ds(ks, tk), ls]
            new = []
            for hh in range(2):
                m, l, acc = carry[hh]
                s = _mm(qm[hh], k2, dims=_NT)
                cr = crow_ref[0, j, 2 * hp + hh:2 * hp + hh + 1, :]
                s = s + (cc[hh] - cr)
                if masked:
                    s = jnp.where(causal, s, NEG_BIG)
                m_new = jnp.maximum(m, jnp.max(s, axis=1, keepdims=True))
                alpha = jnp.exp(m - m_new)
                p = jnp.exp(s - m_new)
                l_new = alpha * l + jnp.sum(p, axis=1, keepdims=True)
                acc_new = alpha * acc + _mm(p.astype(BF16), v2)
                new.append((m_new, l_new, acc_new))
            return tuple(new)

        init_one = (jnp.full((tq, 1), NEG_BIG, F32), jnp.zeros((tq, 1), F32),
                    jnp.zeros((tq, LANES), F32))
        carry = lax.fori_loop(0, i, lambda j, c: step(j, c, False), (init_one, init_one))
        carry = step(i, carry, True)
        o0 = carry[0][2] / carry[0][1]
        o1 = carry[1][2] / carry[1][1]
        o_ref[0, :, ls] = jnp.where(low, o0, o1).astype(BF16)


def _fox(flag, bound, q, k, v, ccol, crow, tq):
    bsz, seq, _ = q.shape
    kern = functools.partial(_fox_kernel, tq=tq, tk=tq)
    return pl.pallas_call(
        kern,
        grid=(bsz, seq // tq),
        in_specs=[
            pl.BlockSpec(memory_space=pltpu.SMEM),
            pl.BlockSpec(memory_space=pltpu.SMEM),
            pl.BlockSpec((1, tq, FOX_WIDTH), lambda b, i: (b, i, 0)),
            pl.BlockSpec((1, seq, FOX_WIDTH), lambda b, i: (b, 0, 0)),
            pl.BlockSpec((1, seq, FOX_WIDTH), lambda b, i: (b, 0, 0)),
            pl.BlockSpec((1, tq, LANES), lambda b, i: (b, i, 0)),
            pl.BlockSpec((1, seq // tq, 16, tq), lambda b, i: (b, 0, 0, 0)),
        ],
        out_specs=pl.BlockSpec((1, tq, FOX_WIDTH), lambda b, i: (b, i, 0)),
        out_shape=jax.ShapeDtypeStruct((bsz, seq, FOX_WIDTH), BF16),
        scratch_shapes=[pltpu.VMEM((PAIRS, tq, 2 * LANES), F32),
                        pltpu.VMEM((FOX_HEADS, tq, LANES), BF16),
                        pltpu.VMEM((FOX_HEADS, tq, LANES), F32)],
        compiler_params=pltpu.CompilerParams(
            dimension_semantics=("arbitrary", "arbitrary"), vmem_limit_bytes=VMEM_LIMIT),
        name="fox",
    )(flag, bound, q, k, v, ccol, crow)


def _rwkv_kernel(rw_ref, mu_ref, w0_ref, w2_ref, a0_ref, a2_ref, g2_ref, kk_ref, ka_ref,
                 rk_ref, lg_ref, lb_ref, e_ref, tri_ref, o_ref, prev_ref, s_ref, *, prec):
    cch = CHUNK
    c = pl.program_id(1)

    @pl.when(c == 0)
    def _():
        prev_ref[...] = jnp.zeros_like(prev_ref)
        s_ref[...] = jnp.zeros_like(s_ref)

    p = rw_ref[0]
    rolled = pltpu.roll(p, 1, 0)
    first = lax.broadcasted_iota(jnp.int32, (cch, 1), 0) == 0
    p_prev = jnp.where(first, prev_ref[0:1, :], rolled)
    prev_ref[0:1, :] = p[cch - 1:cch, :]
    ps = p + (p_prev - p) * mu_ref[...]

    w = RWKV_WIDTH
    r = ps[:, 0:w]
    k = ps[:, w:2 * w]
    v = ps[:, 2 * w:3 * w]
    wdn = ps[:, OFF_WDN:OFF_WDN + LANES]
    adn = ps[:, OFF_ADN:OFF_ADN + LANES]
    gdn = ps[:, OFF_GDN:OFF_GDN + 2 * LANES]

    wlog = _log_sigmoid(w0_ref[...] + _mm(jnp.tanh(wdn), w2_ref[...])) - 0.5
    logd = -jnp.exp(wlog)
    a = _sigmoid(a0_ref[...] + _mm(adn, a2_ref[...]))
    g = _mm(_sigmoid(gdn), g2_ref[...])

    e = e_ref[...]
    kk = k * kk_ref[...]
    kk = kk / jnp.maximum(jnp.sqrt(_mm(kk * kk, e, 2, 1)), L2_EPS)
    k = k * (1.0 + (a - 1.0) * ka_ref[...])
    avec = -kk
    bvec = kk * a

    cum = _mm(tri_ref[...], logd, 1, 3)
    clast = cum[cch - 1:cch, :]
    rt = r * jnp.exp(cum)
    at = avec * jnp.exp(cum - logd)
    einv = jnp.exp(-cum)
    bt = bvec * einv
    kt = k * einv
    etail = jnp.exp(clast - cum)
    bh = bvec * etail
    kh = k * etail
    pc = jnp.exp(clast)

    lane = lax.broadcasted_iota(jnp.int32, (cch, LANES), 1)
    low = lane < HEAD_DIM
    ri = lax.broadcasted_iota(jnp.int32, (cch, cch), 0)
    ci = lax.broadcasted_iota(jnp.int32, (cch, cch), 1)
    strict = ci < ri
    incl = ci <= ri
    blockdiag = (ri < HEAD_DIM) == (ci < HEAD_DIM)

    def halves(z):
        zero = jnp.zeros_like(z)
        return jnp.concatenate([jnp.where(low, z, zero), jnp.where(low, zero, z)], axis=0)

    def blockdiag2(z):
        zero = jnp.zeros((cch, cch), F32)
        return jnp.concatenate([
            jnp.concatenate([z[:, 0:cch], zero], axis=1),
            jnp.concatenate([zero, z[:, cch:2 * cch]], axis=1)], axis=0)

    pg, pd, ps = prec
    pairs = range(PAIRS)
    sl = [slice(LANES * hp, LANES * (hp + 1)) for hp in pairs]

    aab, aak, arbk = [], [], []
    for hp in pairs:
        rb = jnp.concatenate([bt[:, sl[hp]], kt[:, sl[hp]]], axis=0)
        blocks = []
        for hh in range(2):
            sel = low if hh == 0 else jnp.logical_not(low)
            la = jnp.concatenate([jnp.where(sel, at[:, sl[hp]], 0.0),
                                  jnp.where(sel, rt[:, sl[hp]], 0.0)], axis=0)
            blocks.append(_mm(la, rb, pg, pg, dims=_NT))
        aab.append(jnp.concatenate(
            [jnp.where(strict, gm[0:cch, 0:cch], 0.0) for gm in blocks], axis=1))
        aak.append(jnp.concatenate(
            [jnp.where(strict, gm[0:cch, cch:2 * cch], 0.0) for gm in blocks], axis=1))
        arbk.append(jnp.concatenate(
            [jnp.where(incl, gm[cch:2 * cch, 0:cch], 0.0) for gm in blocks]
            + [jnp.where(incl, gm[cch:2 * cch, cch:2 * cch], 0.0) for gm in blocks], axis=1))

    qm = aab
    pk = [_mm(qm[hp], blockdiag2(qm[hp]), pd, pd) for hp in pairs]
    for _ in range(cch.bit_length() - 3):
        both = [_mm(jnp.concatenate([qm[hp], pk[hp]], axis=0), blockdiag2(pk[hp]), pd, pd)
                for hp in pairs]
        qm = [qm[hp] + pk[hp] + both[hp][0:cch] for hp in pairs]
        pk = [both[hp][cch:2 * cch] for hp in pairs]
    qm = [qm[hp] + pk[hp] + _mm(qm[hp], blockdiag2(pk[hp]), pd, pd) for hp in pairs]

    sp = [s_ref[hp] for hp in pairs]
    vst = [halves(v[:, sl[hp]]) for hp in pairs]
    rhs = [_mm(at[:, sl[hp]], sp[hp], ps, ps, dims=_NT) + _mm(aak[hp], vst[hp], ps, ps)
           for hp in pairs]
    u = [rhs[hp] + _mm(qm[hp], halves(rhs[hp]), ps, ps) for hp in pairs]
    ys = [_mm(rt[:, sl[hp]], sp[hp], ps, ps, dims=_NT)
          + _mm(arbk[hp], jnp.concatenate([halves(u[hp]), vst[hp]], axis=0), ps, ps)
          for hp in pairs]
    for hp in pairs:
        uv = jnp.concatenate([u[hp], v[:, sl[hp]]], axis=0)
        bk = jnp.concatenate([bh[:, sl[hp]], kh[:, sl[hp]]], axis=0)
        upd = _mm(uv.T, bk, ps, ps)
        s_ref[hp] = sp[hp] * pc[:, sl[hp]] + jnp.where(blockdiag, upd, 0.0)

    y = jnp.concatenate(ys, axis=1)
    inv_n = 1.0 / HEAD_DIM
    mean = _mm(y, e, 2, 1) * inv_n
    d = y - mean
    var = _mm(d * d, e, 2, 1) * inv_n
    yn = d * lax.rsqrt(var + GN_EPS) * lg_ref[...] + lb_ref[...]
    bonus = _mm(r * k * rk_ref[...], e, 2, 1)
    o_ref[0] = ((yn + bonus * v) * g).astype(BF16)


def _rwkv(rw, mu, w0, w2, a0, a2, g2, k_k, k_a, r_k, lnx_g, lnx_b, e, prec):
    bsz, seq, _ = rw.shape
    cch = CHUNK
    row = lax.broadcasted_iota(jnp.int32, (cch, cch), 0)
    col = lax.broadcasted_iota(jnp.int32, (cch, cch), 1)
    tri = (col <= row).astype(BF16)
    const = lambda shape: pl.BlockSpec(shape, lambda b, t: (0,) * len(shape))
    args = (mu, w0, w2, a0, a2, g2, k_k, k_a, r_k, lnx_g, lnx_b, e, tri)
    return pl.pallas_call(
        functools.partial(_rwkv_kernel, prec=prec),
        grid=(bsz, seq // cch),
        in_specs=[pl.BlockSpec((1, cch, RW_COLS), lambda b, t: (b, t, 0))]
        + [const(a.shape) for a in args],
        out_specs=pl.BlockSpec((1, cch, RWKV_WIDTH), lambda b, t: (b, t, 0)),
        out_shape=jax.ShapeDtypeStruct((bsz, seq, RWKV_WIDTH), BF16),
        scratch_shapes=[pltpu.VMEM((8, RW_COLS), F32), pltpu.VMEM((PAIRS, LANES, LANES), F32)],
        compiler_params=pltpu.CompilerParams(
            dimension_semantics=("arbitrary", "arbitrary"), vmem_limit_bytes=VMEM_LIMIT),
        name="rwkv",
    )(rw, *args)


def _out_proj_kernel(of_ref, or_ref, x_ref, mod_ref, g2_ref, wt_ref, wb_ref, x1_ref, h2_ref):
    mod = mod_ref[0]
    gt1, sh2, sc2 = mod[2:3, :], mod[3:4, :], mod[4:5, :]
    mix = _mm(of_ref[0], wt_ref[...]) + _mm(or_ref[0], wb_ref[...])
    x1 = x_ref[0] + gt1 * mix
    x1_ref[0] = x1
    ms = jnp.mean(x1 * x1, axis=-1, keepdims=True)
    y = x1 * lax.rsqrt(ms + RMS_EPS) * g2_ref[...]
    h2_ref[0] = (y * (1.0 + sc2) + sh2).astype(BF16)


def _out_proj(o_fox, o_rwkv, x, mod, g2, w_top, w_bot, tm):
    bsz, seq, d = x.shape
    const = lambda shape: pl.BlockSpec(shape, lambda b, t: (0,) * len(shape))
    return pl.pallas_call(
        _out_proj_kernel,
        grid=(bsz, seq // tm),
        in_specs=[
            pl.BlockSpec((1, tm, FOX_WIDTH), lambda b, t: (b, t, 0)),
            pl.BlockSpec((1, tm, RWKV_WIDTH), lambda b, t: (b, t, 0)),
            pl.BlockSpec((1, tm, d), lambda b, t: (b, t, 0)),
            pl.BlockSpec((1, 6, d), lambda b, t: (b, 0, 0)),
            const((1, d)), const(w_top.shape), const(w_bot.shape),
        ],
        out_specs=[
            pl.BlockSpec((1, tm, d), lambda b, t: (b, t, 0)),
            pl.BlockSpec((1, tm, d), lambda b, t: (b, t, 0)),
        ],
        out_shape=[jax.ShapeDtypeStruct((bsz, seq, d), F32), jax.ShapeDtypeStruct((bsz, seq, d), BF16)],
        compiler_params=pltpu.CompilerParams(
            dimension_semantics=("arbitrary", "arbitrary"), vmem_limit_bytes=VMEM_LIMIT),
        name="out_proj",
    )(o_fox, o_rwkv, x, mod, g2, w_top, w_bot)


def _ffn_kernel(h2_ref, x1_ref, mod_ref, wg_ref, wu_ref, wd_ref, o_ref, acc_ref):
    j = pl.program_id(2)
    h2 = h2_ref[0]
    gate = _mm(h2, wg_ref[...])
    up = _mm(h2, wu_ref[...])
    act = (gate * _sigmoid(gate) * up).astype(BF16)
    part = _mm(act, wd_ref[...])

    @pl.when(j == 0)
    def _():
        acc_ref[...] = part

    @pl.when(j > 0)
    def _():
        acc_ref[...] = acc_ref[...] + part

    @pl.when(j == pl.num_programs(2) - 1)
    def _():
        gt2 = mod_ref[0][5:6, :]
        o_ref[0] = x1_ref[0] + gt2 * acc_ref[...]


def _ffn(h2, x1, mod, wg, wu, wd, tm, tf):
    bsz, seq, d = x1.shape
    dff = wg.shape[1]
    return pl.pallas_call(
        _ffn_kernel,
        grid=(bsz, seq // tm, dff // tf),
        in_specs=[
            pl.BlockSpec((1, tm, d), lambda b, t, j: (b, t, 0)),
            pl.BlockSpec((1, tm, d), lambda b, t, j: (b, t, 0)),
            pl.BlockSpec((1, 6, d), lambda b, t, j: (b, 0, 0)),
            pl.BlockSpec((d, tf), lambda b, t, j: (0, j)),
            pl.BlockSpec((d, tf), lambda b, t, j: (0, j)),
            pl.BlockSpec((tf, d), lambda b, t, j: (j, 0)),
        ],
        out_specs=pl.BlockSpec((1, tm, d), lambda b, t, j: (b, t, 0)),
        out_shape=jax.ShapeDtypeStruct((bsz, seq, d), F32),
        scratch_shapes=[pltpu.VMEM((tm, d), F32)],
        compiler_params=pltpu.CompilerParams(
            dimension_semantics=("arbitrary", "arbitrary", "arbitrary"),
            vmem_limit_bytes=VMEM_LIMIT),
        name="ffn",
    )(h2, x1, mod, wg, wu, wd)


def _pad_cols(w, n):
    return jnp.pad(w, ((0, 0), (0, n - w.shape[1])))


def _pad_rows(w, n):
    return jnp.pad(w, ((0, n - w.shape[0]), (0, 0)))


def _layer(x, mod, norm1_g, norm2_g, w_in, fox_f_bias, fox_q_gain, fox_k_gain, rwkv_mu, rwkv_w0,
           rwkv_w2, rwkv_a0, rwkv_a2, rwkv_g2, rwkv_k_k, rwkv_k_a, rwkv_r_k, rwkv_lnx_g,
           rwkv_lnx_b, w_out, ffn_w_gate, ffn_w_up, ffn_w_down, *, tm, tq, tf, prec):
    bsz, seq, d = x.shape
    w = RWKV_WIDTH
    nfox = 3 * FOX_WIDTH + FOX_HEADS

    wqkv = w_in[:, 0:3 * FOX_WIDTH].astype(BF16)
    wf = _pad_cols(w_in[:, 3 * FOX_WIDTH:nfox], LANES).astype(BF16)
    wft = _pad_rows(w_in[:, 3 * FOX_WIDTH:nfox].T, 16).astype(BF16)
    wr = w_in[:, nfox:]
    o_w, o_a, o_g = 3 * w, 3 * w + DECAY_LORA, 3 * w + DECAY_LORA + A_LORA
    wrw = jnp.concatenate([
        wr[:, 0:3 * w],
        _pad_cols(wr[:, o_w:o_a], LANES),
        _pad_cols(wr[:, o_a:o_g], LANES),
        _pad_cols(wr[:, o_g:], 2 * LANES)], axis=1).astype(BF16)
    mu = rwkv_mu.reshape(1, -1)
    mu_p = jnp.concatenate([
        mu[:, 0:3 * w],
        _pad_cols(mu[:, o_w:o_a], LANES),
        _pad_cols(mu[:, o_a:o_g], LANES),
        _pad_cols(mu[:, o_g:], 2 * LANES)], axis=1)
    fbrow = _pad_cols(fox_f_bias.reshape(1, -1), LANES)
    fbcol = _pad_rows(fox_f_bias.reshape(-1, 1), 16)
    qg = jnp.tile(fox_q_gain, (1, 1)).reshape(1, FOX_WIDTH)
    kg = fox_k_gain.reshape(1, FOX_WIDTH)
    hi = lax.broadcasted_iota(jnp.int32, (w, w), 0) // HEAD_DIM
    hj = lax.broadcasted_iota(jnp.int32, (w, w), 1) // HEAD_DIM
    e = (hi == hj).astype(BF16)

    q, k, v, ccol, crow, rw = _in_proj(
        x, mod, norm1_g.reshape(1, d), wqkv, wf, wft, wrw, fbrow, fbcol, qg, kg, e, tm, tq)
    bound = (1.05 * HEAD_DIM ** 0.5) * jnp.max(jnp.abs(fox_q_gain)) * jnp.max(jnp.abs(fox_k_gain))
    flag = (bound <= FOX_BOUND_MAX).astype(jnp.int32)
    o_fox = _fox(flag.reshape(1), bound.astype(F32).reshape(1), q, k, v, ccol, crow, tq)
    o_rwkv = _rwkv(
        rw, mu_p, rwkv_w0.reshape(1, w), _pad_rows(rwkv_w2, LANES).astype(BF16),
        rwkv_a0.reshape(1, w), _pad_rows(rwkv_a2, LANES).astype(BF16),
        _pad_rows(rwkv_g2, 2 * LANES).astype(BF16), rwkv_k_k.reshape(1, w), rwkv_k_a.reshape(1, w),
        rwkv_r_k.reshape(1, w), rwkv_lnx_g.reshape(1, w), rwkv_lnx_b.reshape(1, w), e, prec)
    wo = w_out.astype(BF16)
    x1, h2 = _out_proj(o_fox, o_rwkv, x, mod, norm2_g.reshape(1, d),
                       wo[0:FOX_WIDTH], wo[FOX_WIDTH:], tm)
    return _ffn(h2, x1, mod, ffn_w_gate.astype(BF16), ffn_w_up.astype(BF16),
                ffn_w_down.astype(BF16), tm, tf)


def kernel(x, c, ada_w, ada_b, norm1_g, norm2_g, w_in, fox_f_bias, fox_q_gain, fox_k_gain, rwkv_mu,
           rwkv_w0, rwkv_w2, rwkv_a0, rwkv_a2, rwkv_g2, rwkv_k_k, rwkv_k_a, rwkv_r_k, rwkv_lnx_g,
           rwkv_lnx_b, w_out, ffn_w_gate, ffn_w_up, ffn_w_down):
    bsz, seq, d = x.shape
    depth = ada_w.shape[0]
    tm = min(512, seq)
    tq = min(256, seq)
    dff = ffn_w_gate.shape[-1]
    tf = dff // 2 if dff % (2 * LANES) == 0 else dff
    for l in range(depth):
        mod = _ada(c, ada_w[l], ada_b[l]).reshape(bsz, 6, d)
        x = _layer(x, mod, norm1_g[l], norm2_g[l], w_in[l], fox_f_bias[l], fox_q_gain[l],
                   fox_k_gain[l], rwkv_mu[l], rwkv_w0[l], rwkv_w2[l], rwkv_a0[l], rwkv_a2[l],
                   rwkv_g2[l], rwkv_k_k[l], rwkv_k_a[l], rwkv_r_k[l], rwkv_lnx_g[l],
                   rwkv_lnx_b[l], w_out[l], ffn_w_gate[l], ffn_w_up[l], ffn_w_down[l],
                   tm=tm, tq=tq, tf=tf, prec=RWKV_PASSES)
    return x
```

```python
import functools

import jax
import jax.numpy as jnp
from jax import lax
from jax.experimental import pallas as pl
from jax.experimental.pallas import tpu as pltpu

F32 = jnp.float32
BF16 = jnp.bfloat16

HEAD_DIM = 64
FOX_HEADS = 8
RWKV_HEADS = 8
FOX_WIDTH = FOX_HEADS * HEAD_DIM
RWKV_WIDTH = RWKV_HEADS * HEAD_DIM
DECAY_LORA = 64
A_LORA = 64
GATE_LORA = 160
RMS_EPS = 1e-6
GN_EPS = 64e-5
L2_EPS = 1e-12

LANES = 128
MXU_DIM = 256
PAIRS = FOX_HEADS // 2
RWKV_SEQS = 2
INV_BLOCK = 64
RW_COLS = 2048
OFF_WDN, OFF_ADN, OFF_GDN = 1536, 1664, 1792
CHUNK = 128
NEG_BIG = -1e30
RWKV_PASSES = (1, 2, 1)
FOX_BOUND_MAX = 30.0
VMEM_LIMIT = 56 * 1024 * 1024


def _split_bf16(x, n):
    if x.dtype == BF16:
        return [x]
    parts = []
    r = x
    for i in range(n):
        p = r.astype(BF16)
        parts.append(p)
        if i < n - 1:
            r = r - p.astype(F32)
    return parts


_NN = (((1,), (0,)), ((), ()))
_NT = (((1,), (1,)), ((), ()))
_TN = (((0,), (0,)), ((), ()))


def _mm(a, b, pa=1, pb=1, dims=_NN):
    a_parts = _split_bf16(a, pa)
    b_parts = _split_bf16(b, pb)
    order = max(len(a_parts), len(b_parts))
    out = None
    for i, ai in enumerate(a_parts):
        for j, bj in enumerate(b_parts):
            if i + j >= order:
                continue
            t = lax.dot_general(ai, bj, dims, preferred_element_type=F32)
            out = t if out is None else out + t
    return out


def _log_sigmoid(z):
    return jnp.minimum(z, 0.0) - jnp.log(1.0 + jnp.exp(-jnp.abs(z)))


def _sigmoid(z):
    return 1.0 / (1.0 + jnp.exp(-z))


def _ada_kernel(c_ref, w_ref, b_ref, o_ref):
    c = c_ref[...]
    cond = c * _sigmoid(c)
    o_ref[...] = _mm(cond, w_ref[...], 2, 2) + b_ref[...]


def _ada(c, ada_w, ada_b):
    bsz, d = c.shape
    n = ada_w.shape[1]
    tn = 512
    return pl.pallas_call(
        _ada_kernel,
        grid=(n // tn,),
        in_specs=[
            pl.BlockSpec((bsz, d), lambda j: (0, 0)),
            pl.BlockSpec((d, tn), lambda j: (0, j)),
            pl.BlockSpec((1, tn), lambda j: (0, j)),
        ],
        out_specs=pl.BlockSpec((bsz, tn), lambda j: (0, j)),
        out_shape=jax.ShapeDtypeStruct((bsz, n), F32),
        compiler_params=pltpu.CompilerParams(dimension_semantics=("arbitrary",)),
        name="ada",
    )(c, ada_w, ada_b.reshape(1, n))


def _in_proj_kernel(x_ref, mod_ref, g1_ref, wqkv_ref, wf_ref, wft_ref, wrw_ref,
                    fbrow_ref, fbcol_ref, qg_ref, kg_ref, e_ref, trilo_ref, triup_ref,
                    q_ref, k_ref, v_ref, ccol_ref, crow_ref, rw_ref,
                    carry_row, carry_col, *, tm, tk):
    t = pl.program_id(1)

    @pl.when(t == 0)
    def _():
        carry_row[...] = jnp.zeros_like(carry_row)
        carry_col[...] = jnp.zeros_like(carry_col)

    x = x_ref[0]
    mod = mod_ref[0]
    sh1 = mod[0:1, :]
    sc1 = mod[1:2, :]
    ms = jnp.mean(x * x, axis=-1, keepdims=True)
    y = x * lax.rsqrt(ms + RMS_EPS) * g1_ref[...]
    hb = (y * (1.0 + sc1) + sh1).astype(BF16)

    qkv = _mm(hb, wqkv_ref[...])
    e = e_ref[...]
    q = qkv[:, 0:FOX_WIDTH]
    k = qkv[:, FOX_WIDTH:2 * FOX_WIDTH]
    qms = _head_sums(q * q, e, 1) * (1.0 / HEAD_DIM)
    kms = _head_sums(k * k, e, 1) * (1.0 / HEAD_DIM)
    q_ref[0] = (q * lax.rsqrt(qms + RMS_EPS) * qg_ref[...] * (HEAD_DIM ** -0.5)).astype(BF16)
    k_ref[0] = (k * lax.rsqrt(kms + RMS_EPS) * kg_ref[...]).astype(BF16)
    v_ref[0] = qkv[:, 2 * FOX_WIDTH:3 * FOX_WIDTH].astype(BF16)

    rw_ref[0] = _mm(hb, wrw_ref[...])

    lf_col = _log_sigmoid(_mm(hb, wf_ref[...]) + fbrow_ref[...])
    lf_row = _log_sigmoid(_mm(wft_ref[...], hb, dims=_NT) + fbcol_ref[...])
    c_col = _mm(trilo_ref[...], lf_col, 1, 3) + carry_row[...]
    c_row = _mm(lf_row, triup_ref[...], 3, 1) + carry_col[:, 0:1]
    ccol_ref[0] = c_col
    for jj in range(tm // tk):
        crow_ref[0, jj] = c_row[:, jj * tk:(jj + 1) * tk]
    carry_row[...] = c_col[tm - 1:tm, :]
    carry_col[...] = jnp.broadcast_to(c_row[:, tm - 1:tm], carry_col.shape)


def _in_proj(x, mod, g1, wqkv, wf, wft, wrw, fbrow, fbcol, qg, kg, e, tm, tk):
    bsz, seq, d = x.shape
    nt = seq // tm
    row = lax.broadcasted_iota(jnp.int32, (tm, tm), 0)
    col = lax.broadcasted_iota(jnp.int32, (tm, tm), 1)
    trilo = (col <= row).astype(BF16)
    triup = (row <= col).astype(BF16)
    const = lambda shape: pl.BlockSpec(shape, lambda b, t: (0,) * len(shape))
    kern = functools.partial(_in_proj_kernel, tm=tm, tk=tk)
    return pl.pallas_call(
        kern,
        grid=(bsz, nt),
        in_specs=[
            pl.BlockSpec((1, tm, d), lambda b, t: (b, t, 0)),
            pl.BlockSpec((1, 6, d), lambda b, t: (b, 0, 0)),
            const((1, d)),
            const(wqkv.shape), const(wf.shape), const(wft.shape), const(wrw.shape),
            const(fbrow.shape), const(fbcol.shape), const(qg.shape), const(kg.shape),
            const(e.shape), const((tm, tm)), const((tm, tm)),
        ],
        out_specs=[
            pl.BlockSpec((1, tm, FOX_WIDTH), lambda b, t: (b, t, 0)),
            pl.BlockSpec((1, tm, FOX_WIDTH), lambda b, t: (b, t, 0)),
            pl.BlockSpec((1, tm, FOX_WIDTH), lambda b, t: (b, t, 0)),
            pl.BlockSpec((1, tm, LANES), lambda b, t: (b, t, 0)),
            pl.BlockSpec((1, tm // tk, 16, tk), lambda b, t: (b, t, 0, 0)),
            pl.BlockSpec((1, tm, RW_COLS), lambda b, t: (b, t, 0)),
        ],
        out_shape=[
            jax.ShapeDtypeStruct((bsz, seq, FOX_WIDTH), BF16),
            jax.ShapeDtypeStruct((bsz, seq, FOX_WIDTH), BF16),
            jax.ShapeDtypeStruct((bsz, seq, FOX_WIDTH), BF16),
            jax.ShapeDtypeStruct((bsz, seq, LANES), F32),
            jax.ShapeDtypeStruct((bsz, seq // tk, 16, tk), F32),
            jax.ShapeDtypeStruct((bsz, seq, RW_COLS), F32),
        ],
        scratch_shapes=[pltpu.VMEM((1, LANES), F32), pltpu.VMEM((16, LANES), F32)],
        compiler_params=pltpu.CompilerParams(
            dimension_semantics=("arbitrary", "arbitrary"), vmem_limit_bytes=VMEM_LIMIT),
        name="in_proj",
    )(x, mod, g1, wqkv, wf, wft, wrw, fbrow, fbcol, qg, kg, e, trilo, triup)


def _fox_bounded(bound, q_ref, k_ref, v_ref, ccol_ref, crow_ref, o_ref, acc_ref, qm_ref, cb_ref,
                 *, tq, tk):
    i = pl.program_id(1)
    lane = lax.broadcasted_iota(jnp.int32, (tq, LANES), 1)
    low = lane < HEAD_DIM
    row = lax.broadcasted_iota(jnp.int32, (tq, tk), 0)
    col = lax.broadcasted_iota(jnp.int32, (tq, tk), 1)
    causal = col <= row
    klow = lax.broadcasted_iota(jnp.int32, (tk, LANES), 1) < HEAD_DIM
    one_lo = jnp.where(klow, 1.0, 0.0).astype(BF16)
    one_hi = jnp.where(klow, 0.0, 1.0).astype(BF16)

    ccol = ccol_ref[0] - bound
    for hp in range(PAIRS):
        q2 = q_ref[0, :, LANES * hp:LANES * (hp + 1)]
        zero = jnp.zeros_like(q2)
        qm_ref[2 * hp] = jnp.where(low, q2, zero)
        qm_ref[2 * hp + 1] = jnp.where(low, zero, q2)
        for hh in range(2):
            h = 2 * hp + hh
            cb_ref[h] = jnp.broadcast_to(ccol[:, h:h + 1], (tq, LANES))
    acc_ref[...] = jnp.zeros_like(acc_ref)

    def step(j, masked):
        ks = pl.multiple_of(j * tk, tk)
        scores = []
        for hp in range(PAIRS):
            k2 = k_ref[0, pl.ds(ks, tk), LANES * hp:LANES * (hp + 1)]
            for hh in range(2):
                scores.append(_mm(qm_ref[2 * hp + hh], k2, dims=_NT))
        probs = []
        for h in range(FOX_HEADS):
            cb = cb_ref[h]
            bias = jnp.concatenate([cb] * (tk // LANES), axis=1) - crow_ref[0, j, h:h + 1, :]
            s = scores[h] + bias
            if masked:
                s = jnp.where(causal, s, NEG_BIG)
            probs.append(jnp.exp(s).astype(BF16))
        for hp in range(PAIRS):
            v2 = v_ref[0, pl.ds(ks, tk), LANES * hp:LANES * (hp + 1)]
            zero = jnp.zeros_like(v2)
            vaug = jnp.concatenate([
                jnp.concatenate([jnp.where(klow, v2, zero), one_lo], axis=1),
                jnp.concatenate([jnp.where(klow, zero, v2), one_hi], axis=1)], axis=0)
            acc_ref[hp] += _mm(jnp.concatenate([probs[2 * hp], probs[2 * hp + 1]], axis=1), vaug)

    def body(j, carry):
        step(j, False)
        return carry

    lax.fori_loop(0, i, body, 0)
    step(i, True)
    for hp in range(PAIRS):
        a = acc_ref[hp]
        o = a[:, 0:LANES] / a[:, LANES:2 * LANES]
        o_ref[0, :, LANES * hp:LANES * (hp + 1)] = o.astype(BF16)


def _fox_kernel(flag_ref, bound_ref, q_ref, k_ref, v_ref, ccol_ref, crow_ref, o_ref, acc_ref,
                qm_ref, cb_ref, *, tq, tk):
    @pl.when(flag_ref[0] == 1)
    def _():
        _fox_bounded(bound_ref[0], q_ref, k_ref, v_ref, ccol_ref, crow_ref, o_ref, acc_ref,
                     qm_ref, cb_ref, tq=tq, tk=tk)

    @pl.when(flag_ref[0] == 0)
    def _():
        _fox_running_max(q_ref, k_ref, v_ref, ccol_ref, crow_ref, o_ref, tq=tq, tk=tk)


def _fox_running_max(q_ref, k_ref, v_ref, ccol_ref, crow_ref, o_ref, *, tq, tk):
    i = pl.program_id(1)
    lane = lax.broadcasted_iota(jnp.int32, (tq, LANES), 1)
    low = lane < HEAD_DIM
    row = lax.broadcasted_iota(jnp.int32, (tq, tk), 0)
    col = lax.broadcasted_iota(jnp.int32, (tq, tk), 1)
    causal = col <= row
    ccol = ccol_ref[0]

    for hp in range(PAIRS):
        ls = slice(LANES * hp, LANES * (hp + 1))
        q2 = q_ref[0, :, ls]
        zero = jnp.zeros_like(q2)
        qm = (jnp.where(low, q2, zero), jnp.where(low, zero, q2))
        cc = tuple(ccol[:, 2 * hp + hh:2 * hp + hh + 1] for hh in range(2))

        def step(j, carry, masked, ls=ls, qm=qm, cc=cc, hp=hp):
            ks = pl.multiple_of(j * tk, tk)
            k2 = k_ref[0, pl.ds(ks, tk), ls]
            v2 = v_ref[0, pl.ds(ks, tk), ls]
            new = []
            for hh in range(2):
                m, l, acc = carry[hh]
                s = _mm(qm[hh], k2, dims=_NT)
                cr = crow_ref[0, j, 2 * hp + hh:2 * hp + hh + 1, :]
                s = s + (cc[hh] - cr)
                if masked:
                    s = jnp.where(causal, s, NEG_BIG)
                m_new = jnp.maximum(m, jnp.max(s, axis=1, keepdims=True))
                alpha = jnp.exp(m - m_new)
                p = jnp.exp(s - m_new)
                l_new = alpha * l + jnp.sum(p, axis=1, keepdims=True)
                acc_new = alpha * acc + _mm(p.astype(BF16), v2)
                new.append((m_new, l_new, acc_new))
            return tuple(new)

        init_one = (jnp.full((tq, 1), NEG_BIG, F32), jnp.zeros((tq, 1), F32),
                    jnp.zeros((tq, LANES), F32))
        carry = lax.fori_loop(0, i, lambda j, c: step(j, c, False), (init_one, init_one))
        carry = step(i, carry, True)
        o0 = carry[0][2] / carry[0][1]
        o1 = carry[1][2] / carry[1][1]
        o_ref[0, :, ls] = jnp.where(low, o0, o1).astype(BF16)


def _fox(flag, bound, q, k, v, ccol, crow, tq):
    bsz, seq, _ = q.shape
    kern = functools.partial(_fox_kernel, tq=tq, tk=tq)
    return pl.pallas_call(
        kern,
        grid=(bsz, seq // tq),
        in_specs=[
            pl.BlockSpec(memory_space=pltpu.SMEM),
            pl.BlockSpec(memory_space=pltpu.SMEM),
            pl.BlockSpec((1, tq, FOX_WIDTH), lambda b, i: (b, i, 0)),
            pl.BlockSpec((1, seq, FOX_WIDTH), lambda b, i: (b, 0, 0)),
            pl.BlockSpec((1, seq, FOX_WIDTH), lambda b, i: (b, 0, 0)),
            pl.BlockSpec((1, tq, LANES), lambda b, i: (b, i, 0)),
            pl.BlockSpec((1, seq // tq, 16, tq), lambda b, i: (b, 0, 0, 0)),
        ],
        out_specs=pl.BlockSpec((1, tq, FOX_WIDTH), lambda b, i: (b, i, 0)),
        out_shape=jax.ShapeDtypeStruct((bsz, seq, FOX_WIDTH), BF16),
        scratch_shapes=[pltpu.VMEM((PAIRS, tq, 2 * LANES), F32),
                        pltpu.VMEM((FOX_HEADS, tq, LANES), BF16),
                        pltpu.VMEM((FOX_HEADS, tq, LANES), F32)],
        compiler_params=pltpu.CompilerParams(
            dimension_semantics=("arbitrary", "arbitrary"), vmem_limit_bytes=VMEM_LIMIT),
        name="fox",
    )(flag, bound, q, k, v, ccol, crow)


def _unit_tri_inverses(mats, bs, passes):
    c, w = mats[0].shape
    n = range(len(mats))
    lane_cache = {}

    def lane_ids(s):
        if s not in lane_cache:
            lane_cache[s] = lax.broadcasted_iota(jnp.int32, (s, w), 1)
        return lane_cache[s]

    def terms(x):
        return _split_bf16(x, passes)

    def dot_terms(a_t, b_t):
        out = None
        for i, ai in enumerate(a_t):
            for j, bj in enumerate(b_t):
                if i + j < max(len(a_t), len(b_t)):
                    t = lax.dot_general(ai, bj, _NN, preferred_element_type=F32)
                    out = t if out is None else out + t
        return out

    def block_rows(x, s, offset):
        lb = jnp.right_shift(lane_ids(s), s.bit_length() - 1)
        zero = jnp.zeros((s, w), BF16)
        keep = [lb == j for j in range(w // s)]
        out = []
        for t in terms(x):
            rows = []
            for j in range(w // s):
                if offset and j % 2 == 0:
                    rows.append(zero)
                else:
                    rows.append(jnp.where(keep[j - offset], t, zero))
            out.append(jnp.concatenate(rows, axis=0))
        return out

    def mm(a, b_terms):
        return dot_terms(terms(a), b_terms)

    s = bs
    sh = s.bit_length() - 1
    in_mat = jnp.bitwise_and(lane_ids(s), c - 1)
    q = []
    for a2 in mats:
        d = a2[0:s, :]
        for r in range(1, c // s):
            d = jnp.where(jnp.right_shift(in_mat, sh) == r, a2[r * s:(r + 1) * s, :], d)
        q.append(d)
    p = [mm(q[i], block_rows(q[i], s, 0)) for i in n]
    for _ in range(s.bit_length() - 3):
        both = [mm(jnp.concatenate([q[i], p[i]], axis=0), block_rows(p[i], s, 0)) for i in n]
        q = [q[i] + p[i] + both[i][0:s] for i in n]
        p = [both[i][s:2 * s] for i in n]
    q = [q[i] + p[i] + mm(q[i], block_rows(p[i], s, 0)) for i in n]

    while s < c:
        sh = s.bit_length() - 1
        lane = lane_ids(s)
        first = jnp.bitwise_and(lane, s) == 0
        pair_id = jnp.right_shift(jnp.bitwise_and(lane, c - 1), sh + 1)
        l21 = []
        for a2 in mats:
            z = jnp.zeros((s, w), F32)
            for m in range(c // (2 * s)):
                rows = a2[(2 * m + 1) * s:(2 * m + 2) * s, :]
                z = jnp.where(pair_id == m, jnp.where(first, rows, 0.0), z)
            l21.append(z)
        x = [l21[i] + mm(l21[i], block_rows(q[i], s, 0)) for i in n]
        t21 = [x[i] + mm(q[i], block_rows(x[i], s, 1)) for i in n]
        q = [jnp.concatenate([jnp.where(first, q[i], 0.0), jnp.where(first, t21[i], q[i])], axis=0)
             for i in n]
        s *= 2
    return q


def _head_sums(x, e, pa):
    wd = e.shape[0]
    return jnp.concatenate(
        [_mm(x[:, j:j + wd], e, pa, 1) for j in range(0, x.shape[1], wd)], axis=1)


def _rwkv_kernel(rw_ref, mu_ref, w0_ref, w2_ref, a0_ref, a2_ref, g2_ref, kk_ref, ka_ref,
                 rk_ref, lg_ref, lb_ref, e_ref, tri_ref, o_ref, prev_ref, s_ref, *, prec, nb):
    cch = CHUNK
    c = pl.program_id(1)
    seqs = range(nb)
    rs = [slice(cch * i, cch * (i + 1)) for i in seqs]

    @pl.when(c == 0)
    def _():
        prev_ref[...] = jnp.zeros_like(prev_ref)
        s_ref[...] = jnp.zeros_like(s_ref)

    p = rw_ref[...].reshape(nb * cch, RW_COLS)
    row_id = lax.broadcasted_iota(jnp.int32, (nb * cch, 1), 0)
    p_prev = pltpu.roll(p, 1, 0)
    for i in seqs:
        p_prev = jnp.where(row_id == cch * i, prev_ref[8 * i:8 * i + 1, :], p_prev)
        prev_ref[8 * i:8 * i + 1, :] = p[cch * (i + 1) - 1:cch * (i + 1), :]
    ps = p + (p_prev - p) * mu_ref[...]

    w = RWKV_WIDTH
    r = ps[:, 0:w]
    k = ps[:, w:2 * w]
    v = ps[:, 2 * w:3 * w]
    wdn = ps[:, OFF_WDN:OFF_WDN + LANES]
    adn = ps[:, OFF_ADN:OFF_ADN + LANES]
    gdn = ps[:, OFF_GDN:OFF_GDN + 2 * LANES]

    wlog = _log_sigmoid(w0_ref[...] + _mm(jnp.tanh(wdn), w2_ref[...])) - 0.5
    logd = -jnp.exp(wlog)
    a = _sigmoid(a0_ref[...] + _mm(adn, a2_ref[...]))
    g = _mm(_sigmoid(gdn), g2_ref[...])

    e = e_ref[...]
    kk = k * kk_ref[...]
    kk = kk / jnp.maximum(jnp.sqrt(_head_sums(kk * kk, e, 2)), L2_EPS)
    k = k * (1.0 + (a - 1.0) * ka_ref[...])
    avec = -kk
    bvec = kk * a

    cum = _mm(tri_ref[...], logd, 1, 3)
    lasts = [cum[cch * (i + 1) - 1:cch * (i + 1), :] for i in seqs]
    clast = jnp.concatenate([jnp.broadcast_to(z, (cch, w)) for z in lasts], axis=0)
    rt = r * jnp.exp(cum)
    at = avec * jnp.exp(cum - logd)
    einv = jnp.exp(-cum)
    bt = bvec * einv
    kt = k * einv
    etail = jnp.exp(clast - cum)
    bh = bvec * etail
    kh = k * etail
    pc = [jnp.exp(z) for z in lasts]

    lane = lax.broadcasted_iota(jnp.int32, (cch, LANES), 1)
    low = lane < HEAD_DIM
    ri = lax.broadcasted_iota(jnp.int32, (cch, cch), 0)
    ci = lax.broadcasted_iota(jnp.int32, (cch, cch), 1)
    strict = ci < ri
    incl = ci <= ri
    blockdiag = (ri < HEAD_DIM) == (ci < HEAD_DIM)

    def halves(z):
        zero = jnp.zeros_like(z)
        return jnp.concatenate([jnp.where(low, z, zero), jnp.where(low, zero, z)], axis=0)

    pg, pd, ps = prec
    sl = [slice(LANES * hp, LANES * (hp + 1)) for hp in range(PAIRS)]
    combos = [(i, hp) for i in seqs for hp in range(PAIRS)]
    n = range(len(combos))

    def blk(z, j):
        i, hp = combos[j]
        return z[rs[i], sl[hp]]

    aab, aak, arbk = [], [], []
    for j in n:
        rb = jnp.concatenate([blk(bt, j), blk(kt, j)], axis=0)
        blocks = []
        for hh in range(2):
            sel = low if hh == 0 else jnp.logical_not(low)
            la = jnp.concatenate([jnp.where(sel, blk(at, j), 0.0),
                                  jnp.where(sel, blk(rt, j), 0.0)], axis=0)
            blocks.append(_mm(la, rb, pg, pg, dims=_NT))
        aab.append(jnp.concatenate(
            [jnp.where(strict, gm[0:cch, 0:cch], 0.0) for gm in blocks], axis=1))
        aak.append(jnp.concatenate(
            [jnp.where(strict, gm[0:cch, cch:2 * cch], 0.0) for gm in blocks], axis=1))
        arbk.append(jnp.concatenate(
            [jnp.where(incl, gm[cch:2 * cch, 0:cch], 0.0) for gm in blocks]
            + [jnp.where(incl, gm[cch:2 * cch, cch:2 * cch], 0.0) for gm in blocks], axis=1))

    qm = _unit_tri_inverses(aab, INV_BLOCK, pd)

    sp = [s_ref[i, hp] for i, hp in combos]
    vst = [halves(blk(v, j)) for j in n]
    rhs = [_mm(blk(at, j), sp[j], ps, ps, dims=_NT) + _mm(aak[j], vst[j], ps, ps) for j in n]
    u = [rhs[j] + _mm(qm[j], halves(rhs[j]), ps, ps) for j in n]
    ys = [_mm(blk(rt, j), sp[j], ps, ps, dims=_NT)
          + _mm(arbk[j], jnp.concatenate([halves(u[j]), vst[j]], axis=0), ps, ps) for j in n]
    for j in n:
        i, hp = combos[j]
        uv = jnp.concatenate([u[j], blk(v, j)], axis=0)
        bk = jnp.concatenate([blk(bh, j), blk(kh, j)], axis=0)
        upd = _mm(uv.T, bk, ps, ps)
        s_ref[i, hp] = sp[j] * pc[i][:, sl[hp]] + jnp.where(blockdiag, upd, 0.0)

    y = jnp.concatenate(
        [jnp.concatenate(ys[PAIRS * i:PAIRS * (i + 1)], axis=1) for i in seqs], axis=0)
    inv_n = 1.0 / HEAD_DIM
    mean = _head_sums(y, e, 2) * inv_n
    d = y - mean
    var = _head_sums(d * d, e, 2) * inv_n
    yn = d * lax.rsqrt(var + GN_EPS) * lg_ref[...] + lb_ref[...]
    bonus = _head_sums(r * k * rk_ref[...], e, 2)
    o_ref[...] = ((yn + bonus * v) * g).astype(BF16).reshape(nb, cch, w)


def _rwkv(rw, mu, w0, w2, a0, a2, g2, k_k, k_a, r_k, lnx_g, lnx_b, e, prec):
    bsz, seq, _ = rw.shape
    cch = CHUNK
    nb = RWKV_SEQS if bsz % RWKV_SEQS == 0 else 1
    row = lax.broadcasted_iota(jnp.int32, (nb * cch, nb * cch), 0)
    col = lax.broadcasted_iota(jnp.int32, (nb * cch, nb * cch), 1)
    tri = ((col <= row) & (row // cch == col // cch)).astype(BF16)
    const = lambda shape: pl.BlockSpec(shape, lambda b, t: (0,) * len(shape))
    args = (mu, w0, w2, a0, a2, g2, k_k, k_a, r_k, lnx_g, lnx_b, e, tri)
    return pl.pallas_call(
        functools.partial(_rwkv_kernel, prec=prec, nb=nb),
        grid=(bsz // nb, seq // cch),
        in_specs=[pl.BlockSpec((nb, cch, RW_COLS), lambda b, t: (b, t, 0))]
        + [const(a.shape) for a in args],
        out_specs=pl.BlockSpec((nb, cch, RWKV_WIDTH), lambda b, t: (b, t, 0)),
        out_shape=jax.ShapeDtypeStruct((bsz, seq, RWKV_WIDTH), BF16),
        scratch_shapes=[pltpu.VMEM((8 * nb, RW_COLS), F32),
                        pltpu.VMEM((nb, PAIRS, LANES, LANES), F32)],
        compiler_params=pltpu.CompilerParams(
            dimension_semantics=("arbitrary", "arbitrary"), vmem_limit_bytes=VMEM_LIMIT),
        name="rwkv",
    )(rw, *args)


def _out_ffn_kernel(of_ref, or_ref, x_ref, mod_ref, g2_ref, wt_ref, wb_ref, wg_ref, wu_ref, wd_ref,
                    o_ref, x1_ref, h2_ref, acc_ref):
    j = pl.program_id(2)
    mod = mod_ref[0]

    @pl.when(j == 0)
    def _():
        gt1, sh2, sc2 = mod[2:3, :], mod[3:4, :], mod[4:5, :]
        mix = _mm(of_ref[0], wt_ref[...]) + _mm(or_ref[0], wb_ref[...])
        x1 = x_ref[0] + gt1 * mix
        x1_ref[...] = x1
        ms = jnp.mean(x1 * x1, axis=-1, keepdims=True)
        y = x1 * lax.rsqrt(ms + RMS_EPS) * g2_ref[...]
        h2_ref[...] = (y * (1.0 + sc2) + sh2).astype(BF16)

    h2 = h2_ref[...]
    gate = _mm(h2, wg_ref[...])
    up = _mm(h2, wu_ref[...])
    act = (gate * _sigmoid(gate) * up).astype(BF16)
    part = _mm(act, wd_ref[...])

    @pl.when(j == 0)
    def _():
        acc_ref[...] = part

    @pl.when(j > 0)
    def _():
        acc_ref[...] = acc_ref[...] + part

    @pl.when(j == pl.num_programs(2) - 1)
    def _():
        o_ref[0] = x1_ref[...] + mod[5:6, :] * acc_ref[...]


def _out_ffn(o_fox, o_rwkv, x, mod, g2, w_top, w_bot, wg, wu, wd, tm, tf):
    bsz, seq, d = x.shape
    dff = wg.shape[1]
    const = lambda shape: pl.BlockSpec(shape, lambda b, t, j: (0,) * len(shape))
    return pl.pallas_call(
        _out_ffn_kernel,
        grid=(bsz, seq // tm, dff // tf),
        in_specs=[
            pl.BlockSpec((1, tm, FOX_WIDTH), lambda b, t, j: (b, t, 0)),
            pl.BlockSpec((1, tm, RWKV_WIDTH), lambda b, t, j: (b, t, 0)),
            pl.BlockSpec((1, tm, d), lambda b, t, j: (b, t, 0)),
            pl.BlockSpec((1, 6, d), lambda b, t, j: (b, 0, 0)),
            const((1, d)), const(w_top.shape), const(w_bot.shape),
            pl.BlockSpec((d, tf), lambda b, t, j: (0, j)),
            pl.BlockSpec((d, tf), lambda b, t, j: (0, j)),
            pl.BlockSpec((tf, d), lambda b, t, j: (j, 0)),
        ],
        out_specs=pl.BlockSpec((1, tm, d), lambda b, t, j: (b, t, 0)),
        out_shape=jax.ShapeDtypeStruct((bsz, seq, d), F32),
        scratch_shapes=[pltpu.VMEM((tm, d), F32), pltpu.VMEM((tm, d), BF16),
                        pltpu.VMEM((tm, d), F32)],
        compiler_params=pltpu.CompilerParams(
            dimension_semantics=("arbitrary", "arbitrary", "arbitrary"),
            vmem_limit_bytes=VMEM_LIMIT),
        name="out_ffn",
    )(o_fox, o_rwkv, x, mod, g2, w_top, w_bot, wg, wu, wd)


def _pad_cols(w, n):
    return jnp.pad(w, ((0, 0), (0, n - w.shape[1])))


def _pad_rows(w, n):
    return jnp.pad(w, ((0, n - w.shape[0]), (0, 0)))


def _layer(x, mod, norm1_g, norm2_g, w_in, fox_f_bias, fox_q_gain, fox_k_gain, rwkv_mu, rwkv_w0,
           rwkv_w2, rwkv_a0, rwkv_a2, rwkv_g2, rwkv_k_k, rwkv_k_a, rwkv_r_k, rwkv_lnx_g,
           rwkv_lnx_b, w_out, ffn_w_gate, ffn_w_up, ffn_w_down, *, tm, tq, tf, prec):
    bsz, seq, d = x.shape
    w = RWKV_WIDTH
    nfox = 3 * FOX_WIDTH + FOX_HEADS

    wqkv = w_in[:, 0:3 * FOX_WIDTH].astype(BF16)
    wf = _pad_cols(w_in[:, 3 * FOX_WIDTH:nfox], LANES).astype(BF16)
    wft = _pad_rows(w_in[:, 3 * FOX_WIDTH:nfox].T, 16).astype(BF16)
    wr = w_in[:, nfox:]
    o_w, o_a, o_g = 3 * w, 3 * w + DECAY_LORA, 3 * w + DECAY_LORA + A_LORA
    wrw = jnp.concatenate([
        wr[:, 0:3 * w],
        _pad_cols(wr[:, o_w:o_a], LANES),
        _pad_cols(wr[:, o_a:o_g], LANES),
        _pad_cols(wr[:, o_g:], 2 * LANES)], axis=1).astype(BF16)
    mu = rwkv_mu.reshape(1, -1)
    mu_p = jnp.concatenate([
        mu[:, 0:3 * w],
        _pad_cols(mu[:, o_w:o_a], LANES),
        _pad_cols(mu[:, o_a:o_g], LANES),
        _pad_cols(mu[:, o_g:], 2 * LANES)], axis=1)
    fbrow = _pad_cols(fox_f_bias.reshape(1, -1), LANES)
    fbcol = _pad_rows(fox_f_bias.reshape(-1, 1), 16)
    qg = jnp.tile(fox_q_gain, (1, 1)).reshape(1, FOX_WIDTH)
    kg = fox_k_gain.reshape(1, FOX_WIDTH)
    hi = lax.broadcasted_iota(jnp.int32, (MXU_DIM, MXU_DIM), 0) // HEAD_DIM
    hj = lax.broadcasted_iota(jnp.int32, (MXU_DIM, MXU_DIM), 1) // HEAD_DIM
    e = (hi == hj).astype(BF16)

    q, k, v, ccol, crow, rw = _in_proj(
        x, mod, norm1_g.reshape(1, d), wqkv, wf, wft, wrw, fbrow, fbcol, qg, kg, e, tm, tq)
    bound = (1.05 * HEAD_DIM ** 0.5) * jnp.max(jnp.abs(fox_q_gain)) * jnp.max(jnp.abs(fox_k_gain))
    flag = (bound <= FOX_BOUND_MAX).astype(jnp.int32)
    o_fox = _fox(flag.reshape(1), bound.astype(F32).reshape(1), q, k, v, ccol, crow, tq)
    o_rwkv = _rwkv(
        rw, mu_p, rwkv_w0.reshape(1, w), _pad_rows(rwkv_w2, LANES).astype(BF16),
        rwkv_a0.reshape(1, w), _pad_rows(rwkv_a2, LANES).astype(BF16),
        _pad_rows(rwkv_g2, 2 * LANES).astype(BF16), rwkv_k_k.reshape(1, w), rwkv_k_a.reshape(1, w),
        rwkv_r_k.reshape(1, w), rwkv_lnx_g.reshape(1, w), rwkv_lnx_b.reshape(1, w), e, prec)
    wo = w_out.astype(BF16)
    return _out_ffn(o_fox, o_rwkv, x, mod, norm2_g.reshape(1, d), wo[0:FOX_WIDTH], wo[FOX_WIDTH:],
                    ffn_w_gate.astype(BF16), ffn_w_up.astype(BF16), ffn_w_down.astype(BF16),
                    tm, tf)


def kernel(x, c, ada_w, ada_b, norm1_g, norm2_g, w_in, fox_f_bias, fox_q_gain, fox_k_gain, rwkv_mu,
           rwkv_w0, rwkv_w2, rwkv_a0, rwkv_a2, rwkv_g2, rwkv_k_k, rwkv_k_a, rwkv_r_k, rwkv_lnx_g,
           rwkv_lnx_b, w_out, ffn_w_gate, ffn_w_up, ffn_w_down):
    bsz, seq, d = x.shape
    depth = ada_w.shape[0]
    tm = min(512, seq)
    tq = min(256, seq)
    dff = ffn_w_gate.shape[-1]
    tf = dff // 2 if dff % (2 * LANES) == 0 else dff
    for l in range(depth):
        mod = _ada(c, ada_w[l], ada_b[l]).reshape(bsz, 6, d)
        x = _layer(x, mod, norm1_g[l], norm2_g[l], w_in[l], fox_f_bias[l], fox_q_gain[l],
                   fox_k_gain[l], rwkv_mu[l], rwkv_w0[l], rwkv_w2[l], rwkv_a0[l], rwkv_a2[l],
                   rwkv_g2[l], rwkv_k_k[l], rwkv_k_a[l], rwkv_r_k[l], rwkv_lnx_g[l],
                   rwkv_lnx_b[l], w_out[l], ffn_w_gate[l], ffn_w_up[l], ffn_w_down[l],
                   tm=tm, tq=tq, tf=tf, prec=RWKV_PASSES)
    return x
```

```python
import functools

import jax
import jax.numpy as jnp
from jax import lax
from jax.experimental import pallas as pl
from jax.experimental.pallas import tpu as pltpu

F32 = jnp.float32
BF16 = jnp.bfloat16

HEAD_DIM = 64
FOX_HEADS = 8
RWKV_HEADS = 8
FOX_WIDTH = FOX_HEADS * HEAD_DIM
RWKV_WIDTH = RWKV_HEADS * HEAD_DIM
DECAY_LORA = 64
A_LORA = 64
GATE_LORA = 160
RMS_EPS = 1e-6
GN_EPS = 64e-5
L2_EPS = 1e-12

LANES = 128
MXU_DIM = 256
PAIRS = FOX_HEADS // 2
RWKV_SEQS = 2
INV_BLOCK = 64
RW_COLS = 2048
OFF_WDN, OFF_ADN, OFF_GDN = 1536, 1664, 1792
CHUNK = 128
NEG_BIG = -1e30
RWKV_PASSES = (1, 2, 1)
FOX_BOUND_MAX = 30.0
ROW_TILE = 512
QUERY_TILE = 512
KEY_TILE = 256
VMEM_LIMIT = 56 * 1024 * 1024


def _split_bf16(x, n):
    if x.dtype == BF16:
        return [x]
    parts = []
    r = x
    for i in range(n):
        p = r.astype(BF16)
        parts.append(p)
        if i < n - 1:
            r = r - p.astype(F32)
    return parts


_NN = (((1,), (0,)), ((), ()))
_NT = (((1,), (1,)), ((), ()))
_TN = (((0,), (0,)), ((), ()))


def _mm(a, b, pa=1, pb=1, dims=_NN):
    a_parts = _split_bf16(a, pa)
    b_parts = _split_bf16(b, pb)
    order = max(len(a_parts), len(b_parts))
    out = None
    for i, ai in enumerate(a_parts):
        for j, bj in enumerate(b_parts):
            if i + j >= order:
                continue
            t = lax.dot_general(ai, bj, dims, preferred_element_type=F32)
            out = t if out is None else out + t
    return out


def _log_sigmoid(z):
    return jnp.minimum(z, 0.0) - jnp.log(1.0 + jnp.exp(-jnp.abs(z)))


def _sigmoid(z):
    return 1.0 / (1.0 + jnp.exp(-z))


def _ada_kernel(c_ref, w_ref, b_ref, o_ref):
    c = c_ref[...]
    cond = c * _sigmoid(c)
    o_ref[...] = _mm(cond, w_ref[...], 2, 2) + b_ref[...]


def _ada(c, ada_w, ada_b):
    bsz, d = c.shape
    n = ada_w.shape[1]
    tn = 512
    return pl.pallas_call(
        _ada_kernel,
        grid=(n // tn,),
        in_specs=[
            pl.BlockSpec((bsz, d), lambda j: (0, 0)),
            pl.BlockSpec((d, tn), lambda j: (0, j)),
            pl.BlockSpec((1, tn), lambda j: (0, j)),
        ],
        out_specs=pl.BlockSpec((bsz, tn), lambda j: (0, j)),
        out_shape=jax.ShapeDtypeStruct((bsz, n), F32),
        compiler_params=pltpu.CompilerParams(dimension_semantics=("arbitrary",)),
        name="ada",
    )(c, ada_w, ada_b.reshape(1, n))


def _in_proj_kernel(x_ref, mod_ref, g1_ref, wqkv_ref, wf_ref, wft_ref, wrw_ref,
                    fbrow_ref, fbcol_ref, qg_ref, kg_ref, e_ref, trilo_ref, triup_ref,
                    q_ref, k_ref, v_ref, ccol_ref, crow_ref, rw_ref,
                    carry_row, carry_col, *, tm, tk):
    t = pl.program_id(1)

    @pl.when(t == 0)
    def _():
        carry_row[...] = jnp.zeros_like(carry_row)
        carry_col[...] = jnp.zeros_like(carry_col)

    x = x_ref[0]
    mod = mod_ref[0]
    sh1 = mod[0:1, :]
    sc1 = mod[1:2, :]
    ms = jnp.mean(x * x, axis=-1, keepdims=True)
    y = x * lax.rsqrt(ms + RMS_EPS) * g1_ref[...]
    hb = (y * (1.0 + sc1) + sh1).astype(BF16)

    qkv = _mm(hb, wqkv_ref[...])
    e = e_ref[...]
    q = qkv[:, 0:FOX_WIDTH]
    k = qkv[:, FOX_WIDTH:2 * FOX_WIDTH]
    qms = _head_sums(q * q, e, 1) * (1.0 / HEAD_DIM)
    kms = _head_sums(k * k, e, 1) * (1.0 / HEAD_DIM)
    q_ref[0] = (q * lax.rsqrt(qms + RMS_EPS) * qg_ref[...] * (HEAD_DIM ** -0.5)).astype(BF16)
    k_ref[0] = (k * lax.rsqrt(kms + RMS_EPS) * kg_ref[...]).astype(BF16)
    v_ref[0] = qkv[:, 2 * FOX_WIDTH:3 * FOX_WIDTH].astype(BF16)

    rw_ref[0] = _mm(hb, wrw_ref[...])

    lf_col = _log_sigmoid(_mm(hb, wf_ref[...]) + fbrow_ref[...])
    lf_row = _log_sigmoid(_mm(wft_ref[...], hb, dims=_NT) + fbcol_ref[...])
    c_col = _mm(trilo_ref[...], lf_col, 1, 3) + carry_row[...]
    c_row = _mm(lf_row, triup_ref[...], 3, 1) + carry_col[:, 0:1]
    ccol_ref[0] = c_col
    for jj in range(tm // tk):
        crow_ref[0, jj] = c_row[:, jj * tk:(jj + 1) * tk]
    carry_row[...] = c_col[tm - 1:tm, :]
    carry_col[...] = jnp.broadcast_to(c_row[:, tm - 1:tm], carry_col.shape)


def _in_proj(x, mod, g1, wqkv, wf, wft, wrw, fbrow, fbcol, qg, kg, e, tm, tk):
    bsz, seq, d = x.shape
    nt = seq // tm
    row = lax.broadcasted_iota(jnp.int32, (tm, tm), 0)
    col = lax.broadcasted_iota(jnp.int32, (tm, tm), 1)
    trilo = (col <= row).astype(BF16)
    triup = (row <= col).astype(BF16)
    const = lambda shape: pl.BlockSpec(shape, lambda b, t: (0,) * len(shape))
    kern = functools.partial(_in_proj_kernel, tm=tm, tk=tk)
    return pl.pallas_call(
        kern,
        grid=(bsz, nt),
        in_specs=[
            pl.BlockSpec((1, tm, d), lambda b, t: (b, t, 0)),
            pl.BlockSpec((1, 6, d), lambda b, t: (b, 0, 0)),
            const((1, d)),
            const(wqkv.shape), const(wf.shape), const(wft.shape), const(wrw.shape),
            const(fbrow.shape), const(fbcol.shape), const(qg.shape), const(kg.shape),
            const(e.shape), const((tm, tm)), const((tm, tm)),
        ],
        out_specs=[
            pl.BlockSpec((1, tm, FOX_WIDTH), lambda b, t: (b, t, 0)),
            pl.BlockSpec((1, tm, FOX_WIDTH), lambda b, t: (b, t, 0)),
            pl.BlockSpec((1, tm, FOX_WIDTH), lambda b, t: (b, t, 0)),
            pl.BlockSpec((1, tm, LANES), lambda b, t: (b, t, 0)),
            pl.BlockSpec((1, tm // tk, 16, tk), lambda b, t: (b, t, 0, 0)),
            pl.BlockSpec((1, tm, RW_COLS), lambda b, t: (b, t, 0)),
        ],
        out_shape=[
            jax.ShapeDtypeStruct((bsz, seq, FOX_WIDTH), BF16),
            jax.ShapeDtypeStruct((bsz, seq, FOX_WIDTH), BF16),
            jax.ShapeDtypeStruct((bsz, seq, FOX_WIDTH), BF16),
            jax.ShapeDtypeStruct((bsz, seq, LANES), F32),
            jax.ShapeDtypeStruct((bsz, seq // tk, 16, tk), F32),
            jax.ShapeDtypeStruct((bsz, seq, RW_COLS), F32),
        ],
        scratch_shapes=[pltpu.VMEM((1, LANES), F32), pltpu.VMEM((16, LANES), F32)],
        compiler_params=pltpu.CompilerParams(
            dimension_semantics=("arbitrary", "arbitrary"), vmem_limit_bytes=VMEM_LIMIT),
        name="in_proj",
    )(x, mod, g1, wqkv, wf, wft, wrw, fbrow, fbcol, qg, kg, e, trilo, triup)


def _fox_bounded(bound, q_ref, k_ref, v_ref, ccol_ref, crow_ref, o_ref, acc_ref, qm_ref, cb_ref,
                 *, tq, tk):
    i = pl.program_id(1)
    nsub = tq // tk
    lane = lax.broadcasted_iota(jnp.int32, (tq, LANES), 1)
    low = lane < HEAD_DIM
    klow = lax.broadcasted_iota(jnp.int32, (tk, LANES), 1) < HEAD_DIM
    one_lo = jnp.where(klow, 1.0, 0.0).astype(BF16)
    one_hi = jnp.where(klow, 0.0, 1.0).astype(BF16)

    ccol = ccol_ref[0] - bound
    for hp in range(PAIRS):
        q2 = q_ref[0, :, LANES * hp:LANES * (hp + 1)]
        zero = jnp.zeros_like(q2)
        qm_ref[2 * hp] = jnp.where(low, q2, zero)
        qm_ref[2 * hp + 1] = jnp.where(low, zero, q2)
        for hh in range(2):
            h = 2 * hp + hh
            cb_ref[h] = jnp.broadcast_to(ccol[:, h:h + 1], (tq, LANES))
    acc_ref[...] = jnp.zeros_like(acc_ref)

    def step(j, sub):
        r0 = 0 if sub is None else sub * tk
        rows = tq - r0
        ks = pl.multiple_of(j * tk, tk)
        scores = []
        for hp in range(PAIRS):
            k2 = k_ref[0, pl.ds(ks, tk), LANES * hp:LANES * (hp + 1)]
            for hh in range(2):
                scores.append(_mm(qm_ref[2 * hp + hh, r0:tq, :], k2, dims=_NT))
        if sub is not None:
            causal = (lax.broadcasted_iota(jnp.int32, (rows, tk), 1)
                      <= lax.broadcasted_iota(jnp.int32, (rows, tk), 0))
        probs = []
        for h in range(FOX_HEADS):
            cb = cb_ref[h, r0:tq, :]
            bias = jnp.concatenate([cb] * (tk // LANES), axis=1) - crow_ref[0, j, h:h + 1, :]
            s = scores[h] + bias
            if sub is not None:
                s = jnp.where(causal, s, NEG_BIG)
            probs.append(jnp.exp(s).astype(BF16))
        for hp in range(PAIRS):
            v2 = v_ref[0, pl.ds(ks, tk), LANES * hp:LANES * (hp + 1)]
            zero = jnp.zeros_like(v2)
            vaug = jnp.concatenate([
                jnp.concatenate([jnp.where(klow, v2, zero), one_lo], axis=1),
                jnp.concatenate([jnp.where(klow, zero, v2), one_hi], axis=1)], axis=0)
            acc_ref[hp, r0:tq, :] += _mm(
                jnp.concatenate([probs[2 * hp], probs[2 * hp + 1]], axis=1), vaug)

    def body(j, carry):
        step(j, None)
        return carry

    lax.fori_loop(0, i * nsub, body, 0)
    for m in range(nsub):
        step(i * nsub + m, m)
    for hp in range(PAIRS):
        a = acc_ref[hp]
        o = a[:, 0:LANES] / a[:, LANES:2 * LANES]
        o_ref[0, :, LANES * hp:LANES * (hp + 1)] = o.astype(BF16)


def _fox_kernel(flag_ref, bound_ref, q_ref, k_ref, v_ref, ccol_ref, crow_ref, o_ref, acc_ref,
                qm_ref, cb_ref, *, tq, tk):
    @pl.when(flag_ref[0] == 1)
    def _():
        _fox_bounded(bound_ref[0], q_ref, k_ref, v_ref, ccol_ref, crow_ref, o_ref, acc_ref,
                     qm_ref, cb_ref, tq=tq, tk=tk)

    @pl.when(flag_ref[0] == 0)
    def _():
        _fox_running_max(q_ref, k_ref, v_ref, ccol_ref, crow_ref, o_ref, tq=tq, tk=tk)


def _fox_running_max(q_ref, k_ref, v_ref, ccol_ref, crow_ref, o_ref, *, tq, tk):
    i = pl.program_id(1)
    nsub = tq // tk
    lane = lax.broadcasted_iota(jnp.int32, (tq, LANES), 1)
    low = lane < HEAD_DIM
    row = lax.broadcasted_iota(jnp.int32, (tq, tk), 0)
    col = lax.broadcasted_iota(jnp.int32, (tq, tk), 1)
    ccol = ccol_ref[0]

    for hp in range(PAIRS):
        ls = slice(LANES * hp, LANES * (hp + 1))
        q2 = q_ref[0, :, ls]
        zero = jnp.zeros_like(q2)
        qm = (jnp.where(low, q2, zero), jnp.where(low, zero, q2))
        cc = tuple(ccol[:, 2 * hp + hh:2 * hp + hh + 1] for hh in range(2))

        def step(j, carry, sub, ls=ls, qm=qm, cc=cc, hp=hp):
            ks = pl.multiple_of(j * tk, tk)
            k2 = k_ref[0, pl.ds(ks, tk), ls]
            v2 = v_ref[0, pl.ds(ks, tk), ls]
            new = []
            for hh in range(2):
                m, l, acc = carry[hh]
                s = _mm(qm[hh], k2, dims=_NT)
                cr = crow_ref[0, j, 2 * hp + hh:2 * hp + hh + 1, :]
                s = s + (cc[hh] - cr)
                if sub is not None:
                    s = jnp.where(col + sub * tk <= row, s, NEG_BIG)
                m_new = jnp.maximum(m, jnp.max(s, axis=1, keepdims=True))
                alpha = jnp.exp(m - m_new)
                p = jnp.exp(s - m_new)
                l_new = alpha * l + jnp.sum(p, axis=1, keepdims=True)
                acc_new = alpha * acc + _mm(p.astype(BF16), v2)
                new.append((m_new, l_new, acc_new))
            return tuple(new)

        init_one = (jnp.full((tq, 1), NEG_BIG, F32), jnp.zeros((tq, 1), F32),
                    jnp.zeros((tq, LANES), F32))
        carry = lax.fori_loop(0, i * nsub, lambda j, c: step(j, c, None), (init_one, init_one))
        for sub in range(nsub):
            carry = step(i * nsub + sub, carry, sub)
        o0 = carry[0][2] / carry[0][1]
        o1 = carry[1][2] / carry[1][1]
        o_ref[0, :, ls] = jnp.where(low, o0, o1).astype(BF16)


def _fox(flag, bound, q, k, v, ccol, crow, tq, tk):
    bsz, seq, _ = q.shape
    kern = functools.partial(_fox_kernel, tq=tq, tk=tk)
    return pl.pallas_call(
        kern,
        grid=(bsz, seq // tq),
        in_specs=[
            pl.BlockSpec(memory_space=pltpu.SMEM),
            pl.BlockSpec(memory_space=pltpu.SMEM),
            pl.BlockSpec((1, tq, FOX_WIDTH), lambda b, i: (b, i, 0)),
            pl.BlockSpec((1, seq, FOX_WIDTH), lambda b, i: (b, 0, 0)),
            pl.BlockSpec((1, seq, FOX_WIDTH), lambda b, i: (b, 0, 0)),
            pl.BlockSpec((1, tq, LANES), lambda b, i: (b, i, 0)),
            pl.BlockSpec((1, seq // tk, 16, tk), lambda b, i: (b, 0, 0, 0)),
        ],
        out_specs=pl.BlockSpec((1, tq, FOX_WIDTH), lambda b, i: (b, i, 0)),
        out_shape=jax.ShapeDtypeStruct((bsz, seq, FOX_WIDTH), BF16),
        scratch_shapes=[pltpu.VMEM((PAIRS, tq, 2 * LANES), F32),
                        pltpu.VMEM((FOX_HEADS, tq, LANES), BF16),
                        pltpu.VMEM((FOX_HEADS, tq, LANES), F32)],
        compiler_params=pltpu.CompilerParams(
            dimension_semantics=("arbitrary", "arbitrary"), vmem_limit_bytes=VMEM_LIMIT),
        name="fox",
    )(flag, bound, q, k, v, ccol, crow)


def _unit_tri_inverses(mats, bs, passes):
    c, w = mats[0].shape
    n = range(len(mats))
    lane_cache = {}

    def lane_ids(s):
        if s not in lane_cache:
            lane_cache[s] = lax.broadcasted_iota(jnp.int32, (s, w), 1)
        return lane_cache[s]

    def terms(x):
        return _split_bf16(x, passes)

    def dot_terms(a_t, b_t):
        out = None
        for i, ai in enumerate(a_t):
            for j, bj in enumerate(b_t):
                if i + j < max(len(a_t), len(b_t)):
                    t = lax.dot_general(ai, bj, _NN, preferred_element_type=F32)
                    out = t if out is None else out + t
        return out

    def block_rows(x, s, offset):
        lb = jnp.right_shift(lane_ids(s), s.bit_length() - 1)
        zero = jnp.zeros((s, w), BF16)
        keep = [lb == j for j in range(w // s)]
        out = []
        for t in terms(x):
            rows = []
            for j in range(w // s):
                if offset and j % 2 == 0:
                    rows.append(zero)
                else:
                    rows.append(jnp.where(keep[j - offset], t, zero))
            out.append(jnp.concatenate(rows, axis=0))
        return out

    def mm(a, b_terms):
        return dot_terms(terms(a), b_terms)

    s = bs
    sh = s.bit_length() - 1
    in_mat = jnp.bitwise_and(lane_ids(s), c - 1)
    q = []
    for a2 in mats:
        d = a2[0:s, :]
        for r in range(1, c // s):
            d = jnp.where(jnp.right_shift(in_mat, sh) == r, a2[r * s:(r + 1) * s, :], d)
        q.append(d)
    p = [mm(q[i], block_rows(q[i], s, 0)) for i in n]
    for _ in range(s.bit_length() - 3):
        both = [mm(jnp.concatenate([q[i], p[i]], axis=0), block_rows(p[i], s, 0)) for i in n]
        q = [q[i] + p[i] + both[i][0:s] for i in n]
        p = [both[i][s:2 * s] for i in n]
    q = [q[i] + p[i] + mm(q[i], block_rows(p[i], s, 0)) for i in n]

    while s < c:
        sh = s.bit_length() - 1
        lane = lane_ids(s)
        first = jnp.bitwise_and(lane, s) == 0
        pair_id = jnp.right_shift(jnp.bitwise_and(lane, c - 1), sh + 1)
        l21 = []
        for a2 in mats:
            z = jnp.zeros((s, w), F32)
            for m in range(c // (2 * s)):
                rows = a2[(2 * m + 1) * s:(2 * m + 2) * s, :]
                z = jnp.where(pair_id == m, jnp.where(first, rows, 0.0), z)
            l21.append(z)
        x = [l21[i] + mm(l21[i], block_rows(q[i], s, 0)) for i in n]
        t21 = [x[i] + mm(q[i], block_rows(x[i], s, 1)) for i in n]
        q = [jnp.concatenate([jnp.where(first, q[i], 0.0), jnp.where(first, t21[i], q[i])], axis=0)
             for i in n]
        s *= 2
    return q


def _head_sums(x, e, pa):
    wd = e.shape[0]
    return jnp.concatenate(
        [_mm(x[:, j:j + wd], e, pa, 1) for j in range(0, x.shape[1], wd)], axis=1)


def _rwkv_kernel(rw_ref, mu_ref, w0_ref, w2_ref, a0_ref, a2_ref, g2_ref, kk_ref, ka_ref,
                 rk_ref, lg_ref, lb_ref, e_ref, tri_ref, o_ref, prev_ref, s_ref, *, prec, nb):
    cch = CHUNK
    c = pl.program_id(1)
    seqs = range(nb)
    rs = [slice(cch * i, cch * (i + 1)) for i in seqs]

    @pl.when(c == 0)
    def _():
        prev_ref[...] = jnp.zeros_like(prev_ref)
        s_ref[...] = jnp.zeros_like(s_ref)

    p = rw_ref[...].reshape(nb * cch, RW_COLS)
    row_id = lax.broadcasted_iota(jnp.int32, (nb * cch, 1), 0)
    p_prev = pltpu.roll(p, 1, 0)
    for i in seqs:
        p_prev = jnp.where(row_id == cch * i, prev_ref[8 * i:8 * i + 1, :], p_prev)
        prev_ref[8 * i:8 * i + 1, :] = p[cch * (i + 1) - 1:cch * (i + 1), :]
    ps = p + (p_prev - p) * mu_ref[...]

    w = RWKV_WIDTH
    r = ps[:, 0:w]
    k = ps[:, w:2 * w]
    v = ps[:, 2 * w:3 * w]
    wdn = ps[:, OFF_WDN:OFF_WDN + LANES]
    adn = ps[:, OFF_ADN:OFF_ADN + LANES]
    gdn = ps[:, OFF_GDN:OFF_GDN + 2 * LANES]

    wlog = _log_sigmoid(w0_ref[...] + _mm(jnp.tanh(wdn), w2_ref[...])) - 0.5
    logd = -jnp.exp(wlog)
    a = _sigmoid(a0_ref[...] + _mm(adn, a2_ref[...]))
    g = _mm(_sigmoid(gdn), g2_ref[...])

    e = e_ref[...]
    kk = k * kk_ref[...]
    kk = kk / jnp.maximum(jnp.sqrt(_head_sums(kk * kk, e, 2)), L2_EPS)
    k = k * (1.0 + (a - 1.0) * ka_ref[...])
    avec = -kk
    bvec = kk * a

    cum = _mm(tri_ref[...], logd, 1, 3)
    lasts = [cum[cch * (i + 1) - 1:cch * (i + 1), :] for i in seqs]
    clast = jnp.concatenate([jnp.broadcast_to(z, (cch, w)) for z in lasts], axis=0)
    rt = r * jnp.exp(cum)
    at = avec * jnp.exp(cum - logd)
    einv = jnp.exp(-cum)
    bt = bvec * einv
    kt = k * einv
    etail = jnp.exp(clast - cum)
    bh = bvec * etail
    kh = k * etail
    pc = [jnp.exp(z) for z in lasts]

    lane = lax.broadcasted_iota(jnp.int32, (cch, LANES), 1)
    low = lane < HEAD_DIM
    ri = lax.broadcasted_iota(jnp.int32, (cch, cch), 0)
    ci = lax.broadcasted_iota(jnp.int32, (cch, cch), 1)
    strict = ci < ri
    incl = ci <= ri
    blockdiag = (ri < HEAD_DIM) == (ci < HEAD_DIM)

    def halves(z):
        zero = jnp.zeros_like(z)
        return jnp.concatenate([jnp.where(low, z, zero), jnp.where(low, zero, z)], axis=0)

    pg, pd, ps = prec
    sl = [slice(LANES * hp, LANES * (hp + 1)) for hp in range(PAIRS)]
    combos = [(i, hp) for i in seqs for hp in range(PAIRS)]
    n = range(len(combos))

    def blk(z, j):
        i, hp = combos[j]
        return z[rs[i], sl[hp]]

    aab, aak, arbk = [], [], []
    for j in n:
        rb = jnp.concatenate([blk(bt, j), blk(kt, j)], axis=0)
        blocks = []
        for hh in range(2):
            sel = low if hh == 0 else jnp.logical_not(low)
            la = jnp.concatenate([jnp.where(sel, blk(at, j), 0.0),
                                  jnp.where(sel, blk(rt, j), 0.0)], axis=0)
            blocks.append(_mm(la, rb, pg, pg, dims=_NT))
        aab.append(jnp.concatenate(
            [jnp.where(strict, gm[0:cch, 0:cch], 0.0) for gm in blocks], axis=1))
        aak.append(jnp.concatenate(
            [jnp.where(strict, gm[0:cch, cch:2 * cch], 0.0) for gm in blocks], axis=1))
        arbk.append(jnp.concatenate(
            [jnp.where(incl, gm[cch:2 * cch, 0:cch], 0.0) for gm in blocks]
            + [jnp.where(incl, gm[cch:2 * cch, cch:2 * cch], 0.0) for gm in blocks], axis=1))

    qm = _unit_tri_inverses(aab, INV_BLOCK, pd)

    sp = [s_ref[i, hp] for i, hp in combos]
    vst = [halves(blk(v, j)) for j in n]
    rhs = [_mm(blk(at, j), sp[j], ps, ps, dims=_NT) + _mm(aak[j], vst[j], ps, ps) for j in n]
    u = [rhs[j] + _mm(qm[j], halves(rhs[j]), ps, ps) for j in n]
    ys = [_mm(blk(rt, j), sp[j], ps, ps, dims=_NT)
          + _mm(arbk[j], jnp.concatenate([halves(u[j]), vst[j]], axis=0), ps, ps) for j in n]
    for j in n:
        i, hp = combos[j]
        uv = jnp.concatenate([u[j], blk(v, j)], axis=0)
        bk = jnp.concatenate([blk(bh, j), blk(kh, j)], axis=0)
        upd = _mm(uv.T, bk, ps, ps)
        s_ref[i, hp] = sp[j] * pc[i][:, sl[hp]] + jnp.where(blockdiag, upd, 0.0)

    y = jnp.concatenate(
        [jnp.concatenate(ys[PAIRS * i:PAIRS * (i + 1)], axis=1) for i in seqs], axis=0)
    inv_n = 1.0 / HEAD_DIM
    mean = _head_sums(y, e, 2) * inv_n
    d = y - mean
    var = _head_sums(d * d, e, 2) * inv_n
    yn = d * lax.rsqrt(var + GN_EPS) * lg_ref[...] + lb_ref[...]
    bonus = _head_sums(r * k * rk_ref[...], e, 2)
    o_ref[...] = ((yn + bonus * v) * g).astype(BF16).reshape(nb, cch, w)


def _rwkv(rw, mu, w0, w2, a0, a2, g2, k_k, k_a, r_k, lnx_g, lnx_b, e, prec):
    bsz, seq, _ = rw.shape
    cch = CHUNK
    nb = RWKV_SEQS if bsz % RWKV_SEQS == 0 else 1
    row = lax.broadcasted_iota(jnp.int32, (nb * cch, nb * cch), 0)
    col = lax.broadcasted_iota(jnp.int32, (nb * cch, nb * cch), 1)
    tri = ((col <= row) & (row // cch == col // cch)).astype(BF16)
    const = lambda shape: pl.BlockSpec(shape, lambda b, t: (0,) * len(shape))
    args = (mu, w0, w2, a0, a2, g2, k_k, k_a, r_k, lnx_g, lnx_b, e, tri)
    return pl.pallas_call(
        functools.partial(_rwkv_kernel, prec=prec, nb=nb),
        grid=(bsz // nb, seq // cch),
        in_specs=[pl.BlockSpec((nb, cch, RW_COLS), lambda b, t: (b, t, 0))]
        + [const(a.shape) for a in args],
        out_specs=pl.BlockSpec((nb, cch, RWKV_WIDTH), lambda b, t: (b, t, 0)),
        out_shape=jax.ShapeDtypeStruct((bsz, seq, RWKV_WIDTH), BF16),
        scratch_shapes=[pltpu.VMEM((8 * nb, RW_COLS), F32),
                        pltpu.VMEM((nb, PAIRS, LANES, LANES), F32)],
        compiler_params=pltpu.CompilerParams(
            dimension_semantics=("arbitrary", "arbitrary"), vmem_limit_bytes=VMEM_LIMIT),
        name="rwkv",
    )(rw, *args)


def _out_ffn_kernel(of_ref, or_ref, x_ref, mod_ref, g2_ref, wt_ref, wb_ref, wg_ref, wu_ref, wd_ref,
                    o_ref, *, slabs):
    mod = mod_ref[0]
    gt1, sh2, sc2, gt2 = mod[2:3, :], mod[3:4, :], mod[4:5, :], mod[5:6, :]
    mix = _mm(of_ref[0], wt_ref[...]) + _mm(or_ref[0], wb_ref[...])
    x1 = x_ref[0] + gt1 * mix
    ms = jnp.mean(x1 * x1, axis=-1, keepdims=True)
    y = x1 * lax.rsqrt(ms + RMS_EPS) * g2_ref[...]
    h2 = (y * (1.0 + sc2) + sh2).astype(BF16)
    acc = None
    for lo, hi in slabs:
        gate = _mm(h2, wg_ref[:, lo:hi])
        up = _mm(h2, wu_ref[:, lo:hi])
        act = (gate * _sigmoid(gate) * up).astype(BF16)
        part = _mm(act, wd_ref[lo:hi, :])
        acc = part if acc is None else acc + part
    o_ref[0] = x1 + gt2 * acc


def _out_ffn(o_fox, o_rwkv, x, mod, g2, w_top, w_bot, wg, wu, wd, tm):
    bsz, seq, d = x.shape
    dff = wg.shape[1]
    tiles = dff // MXU_DIM if dff % MXU_DIM == 0 else 1
    cut = (tiles // 2) * (dff // tiles)
    slabs = ((0, cut), (cut, dff)) if cut else ((0, dff),)
    resident = lambda shape: pl.BlockSpec(shape, lambda b, t: (0,) * len(shape),
                                          pipeline_mode=pl.Buffered(1))
    return pl.pallas_call(
        functools.partial(_out_ffn_kernel, slabs=slabs),
        grid=(bsz, seq // tm),
        in_specs=[
            pl.BlockSpec((1, tm, FOX_WIDTH), lambda b, t: (b, t, 0)),
            pl.BlockSpec((1, tm, RWKV_WIDTH), lambda b, t: (b, t, 0)),
            pl.BlockSpec((1, tm, d), lambda b, t: (b, t, 0)),
            pl.BlockSpec((1, 6, d), lambda b, t: (b, 0, 0)),
            resident((1, d)), resident(w_top.shape), resident(w_bot.shape),
            resident(wg.shape), resident(wu.shape), resident(wd.shape),
        ],
        out_specs=pl.BlockSpec((1, tm, d), lambda b, t: (b, t, 0)),
        out_shape=jax.ShapeDtypeStruct((bsz, seq, d), F32),
        compiler_params=pltpu.CompilerParams(
            dimension_semantics=("arbitrary", "arbitrary"), vmem_limit_bytes=VMEM_LIMIT),
        name="out_ffn",
    )(o_fox, o_rwkv, x, mod, g2, w_top, w_bot, wg, wu, wd)


def _pad_cols(w, n):
    return jnp.pad(w, ((0, 0), (0, n - w.shape[1])))


def _pad_rows(w, n):
    return jnp.pad(w, ((0, n - w.shape[0]), (0, 0)))


def _layer(x, mod, norm1_g, norm2_g, w_in, fox_f_bias, fox_q_gain, fox_k_gain, rwkv_mu, rwkv_w0,
           rwkv_w2, rwkv_a0, rwkv_a2, rwkv_g2, rwkv_k_k, rwkv_k_a, rwkv_r_k, rwkv_lnx_g,
           rwkv_lnx_b, w_out, ffn_w_gate, ffn_w_up, ffn_w_down, *, tm, tq, tk, prec):
    bsz, seq, d = x.shape
    w = RWKV_WIDTH
    nfox = 3 * FOX_WIDTH + FOX_HEADS

    wqkv = w_in[:, 0:3 * FOX_WIDTH].astype(BF16)
    wf = _pad_cols(w_in[:, 3 * FOX_WIDTH:nfox], LANES).astype(BF16)
    wft = _pad_rows(w_in[:, 3 * FOX_WIDTH:nfox].T, 16).astype(BF16)
    wr = w_in[:, nfox:]
    o_w, o_a, o_g = 3 * w, 3 * w + DECAY_LORA, 3 * w + DECAY_LORA + A_LORA
    wrw = jnp.concatenate([
        wr[:, 0:3 * w],
        _pad_cols(wr[:, o_w:o_a], LANES),
        _pad_cols(wr[:, o_a:o_g], LANES),
        _pad_cols(wr[:, o_g:], 2 * LANES)], axis=1).astype(BF16)
    mu = rwkv_mu.reshape(1, -1)
    mu_p = jnp.concatenate([
        mu[:, 0:3 * w],
        _pad_cols(mu[:, o_w:o_a], LANES),
        _pad_cols(mu[:, o_a:o_g], LANES),
        _pad_cols(mu[:, o_g:], 2 * LANES)], axis=1)
    fbrow = _pad_cols(fox_f_bias.reshape(1, -1), LANES)
    fbcol = _pad_rows(fox_f_bias.reshape(-1, 1), 16)
    qg = jnp.tile(fox_q_gain, (1, 1)).reshape(1, FOX_WIDTH)
    kg = fox_k_gain.reshape(1, FOX_WIDTH)
    hi = lax.broadcasted_iota(jnp.int32, (MXU_DIM, MXU_DIM), 0) // HEAD_DIM
    hj = lax.broadcasted_iota(jnp.int32, (MXU_DIM, MXU_DIM), 1) // HEAD_DIM
    e = (hi == hj).astype(BF16)

    q, k, v, ccol, crow, rw = _in_proj(
        x, mod, norm1_g.reshape(1, d), wqkv, wf, wft, wrw, fbrow, fbcol, qg, kg, e, tm, tk)
    bound = (1.05 * HEAD_DIM ** 0.5) * jnp.max(jnp.abs(fox_q_gain)) * jnp.max(jnp.abs(fox_k_gain))
    flag = (bound <= FOX_BOUND_MAX).astype(jnp.int32)
    o_fox = _fox(flag.reshape(1), bound.astype(F32).reshape(1), q, k, v, ccol, crow, tq, tk)
    o_rwkv = _rwkv(
        rw, mu_p, rwkv_w0.reshape(1, w), _pad_rows(rwkv_w2, LANES).astype(BF16),
        rwkv_a0.reshape(1, w), _pad_rows(rwkv_a2, LANES).astype(BF16),
        _pad_rows(rwkv_g2, 2 * LANES).astype(BF16), rwkv_k_k.reshape(1, w), rwkv_k_a.reshape(1, w),
        rwkv_r_k.reshape(1, w), rwkv_lnx_g.reshape(1, w), rwkv_lnx_b.reshape(1, w), e, prec)
    wo = w_out.astype(BF16)
    return _out_ffn(o_fox, o_rwkv, x, mod, norm2_g.reshape(1, d), wo[0:FOX_WIDTH], wo[FOX_WIDTH:],
                    ffn_w_gate.astype(BF16), ffn_w_up.astype(BF16), ffn_w_down.astype(BF16),
                    tm)


def kernel(x, c, ada_w, ada_b, norm1_g, norm2_g, w_in, fox_f_bias, fox_q_gain, fox_k_gain, rwkv_mu,
           rwkv_w0, rwkv_w2, rwkv_a0, rwkv_a2, rwkv_g2, rwkv_k_k, rwkv_k_a, rwkv_r_k, rwkv_lnx_g,
           rwkv_lnx_b, w_out, ffn_w_gate, ffn_w_up, ffn_w_down):
    bsz, seq, d = x.shape
    depth = ada_w.shape[0]
    dff = ffn_w_gate.shape[-1]
    tm = min(ROW_TILE, seq)
    tk = min(KEY_TILE, seq)
    tq = min(QUERY_TILE, seq)
    assert seq % tm == 0 and seq % tq == 0 and tq % tk == 0 and seq % CHUNK == 0
    for l in range(depth):
        mod = _ada(c, ada_w[l], ada_b[l]).reshape(bsz, 6, d)
        x = _layer(x, mod, norm1_g[l], norm2_g[l], w_in[l], fox_f_bias[l], fox_q_gain[l],
                   fox_k_gain[l], rwkv_mu[l], rwkv_w0[l], rwkv_w2[l], rwkv_a0[l], rwkv_a2[l],
                   rwkv_g2[l], rwkv_k_k[l], rwkv_k_a[l], rwkv_r_k[l], rwkv_lnx_g[l],
                   rwkv_lnx_b[l], w_out[l], ffn_w_gate[l], ffn_w_up[l], ffn_w_down[l],
                   tm=tm, tq=tq, tk=tk, prec=RWKV_PASSES)
    return x
```

```python
import functools

import jax
import jax.numpy as jnp
from jax import lax
from jax.experimental import pallas as pl
from jax.experimental.pallas import tpu as pltpu

F32 = jnp.float32
BF16 = jnp.bfloat16

HEAD_DIM = 64
FOX_HEADS = 8
RWKV_HEADS = 8
FOX_WIDTH = FOX_HEADS * HEAD_DIM
RWKV_WIDTH = RWKV_HEADS * HEAD_DIM
DECAY_LORA = 64
A_LORA = 64
GATE_LORA = 160
RMS_EPS = 1e-6
GN_EPS = 64e-5
L2_EPS = 1e-12

LANES = 128
MXU_DIM = 256
PAIRS = FOX_HEADS // 2
RWKV_SEQS = 4
RWKV_GROUP = 2
INV_BLOCK = 64
RW_COLS = 2048
OFF_WDN, OFF_ADN, OFF_GDN = 1536, 1664, 1792
CHUNK = 128
NEG_BIG = -1e30
RWKV_PASSES = (1, 1, 1)
FOX_BOUND_MAX = 30.0
ROW_TILE = 512
QUERY_TILE = 512
KEY_TILE = 256
VMEM_LIMIT = 56 * 1024 * 1024


def _split_bf16(x, n):
    if x.dtype == BF16:
        return [x]
    parts = []
    r = x
    for i in range(n):
        p = r.astype(BF16)
        parts.append(p)
        if i < n - 1:
            r = r - p.astype(F32)
    return parts


_NN = (((1,), (0,)), ((), ()))
_NT = (((1,), (1,)), ((), ()))
_TN = (((0,), (0,)), ((), ()))


def _mm(a, b, pa=1, pb=1, dims=_NN):
    a_parts = _split_bf16(a, pa)
    b_parts = _split_bf16(b, pb)
    order = max(len(a_parts), len(b_parts))
    out = None
    for i, ai in enumerate(a_parts):
        for j, bj in enumerate(b_parts):
            if i + j >= order:
                continue
            t = lax.dot_general(ai, bj, dims, preferred_element_type=F32)
            out = t if out is None else out + t
    return out


def _log_sigmoid(z):
    return jnp.minimum(z, 0.0) - jnp.log(1.0 + jnp.exp(-jnp.abs(z)))


def _sigmoid(z):
    return 1.0 / (1.0 + jnp.exp(-z))


def _ada_kernel(c_ref, w_ref, b_ref, o_ref):
    c = c_ref[...]
    cond = c * _sigmoid(c)
    o_ref[...] = _mm(cond, w_ref[...], 2, 2) + b_ref[...]


def _ada(c, ada_w, ada_b):
    bsz, d = c.shape
    n = ada_w.shape[1]
    tn = 512
    return pl.pallas_call(
        _ada_kernel,
        grid=(n // tn,),
        in_specs=[
            pl.BlockSpec((bsz, d), lambda j: (0, 0)),
            pl.BlockSpec((d, tn), lambda j: (0, j)),
            pl.BlockSpec((1, tn), lambda j: (0, j)),
        ],
        out_specs=pl.BlockSpec((bsz, tn), lambda j: (0, j)),
        out_shape=jax.ShapeDtypeStruct((bsz, n), F32),
        compiler_params=pltpu.CompilerParams(dimension_semantics=("arbitrary",)),
        name="ada",
    )(c, ada_w, ada_b.reshape(1, n))


def _in_proj_kernel(x_ref, mod_ref, g1_ref, wqkv_ref, wf_ref, wft_ref, wrw_ref,
                    fbrow_ref, fbcol_ref, qg_ref, kg_ref, e_ref, trilo_ref, triup_ref, mu_ref,
                    q_ref, k_ref, v_ref, ccol_ref, crow_ref, rw_ref,
                    carry_row, carry_col, prev_ref, *, tm, tk):
    t = pl.program_id(1)

    @pl.when(t == 0)
    def _():
        carry_row[...] = jnp.zeros_like(carry_row)
        carry_col[...] = jnp.zeros_like(carry_col)
        prev_ref[...] = jnp.zeros_like(prev_ref)

    x = x_ref[0]
    mod = mod_ref[0]
    sh1 = mod[0:1, :]
    sc1 = mod[1:2, :]
    ms = jnp.mean(x * x, axis=-1, keepdims=True)
    y = x * lax.rsqrt(ms + RMS_EPS) * g1_ref[...]
    hb = (y * (1.0 + sc1) + sh1).astype(BF16)

    qkv = _mm(hb, wqkv_ref[...])
    e = e_ref[...]
    q = qkv[:, 0:FOX_WIDTH]
    k = qkv[:, FOX_WIDTH:2 * FOX_WIDTH]
    qms = _head_sums(q * q, e, 1) * (1.0 / HEAD_DIM)
    kms = _head_sums(k * k, e, 1) * (1.0 / HEAD_DIM)
    q_ref[0] = (q * lax.rsqrt(qms + RMS_EPS) * qg_ref[...] * (HEAD_DIM ** -0.5)).astype(BF16)
    k_ref[0] = (k * lax.rsqrt(kms + RMS_EPS) * kg_ref[...]).astype(BF16)
    v_ref[0] = qkv[:, 2 * FOX_WIDTH:3 * FOX_WIDTH].astype(BF16)

    rw = _mm(hb, wrw_ref[...])
    rolled = pltpu.roll(rw, 1, 0)
    first_row = lax.broadcasted_iota(jnp.int32, (8, 1), 0) == 0
    top = jnp.where(first_row, prev_ref[0:1, :], rolled[0:8])
    prev_ref[0:1, :] = rw[tm - 1:tm, :]
    rw_ref[0] = rw + (jnp.concatenate([top, rolled[8:tm]], axis=0) - rw) * mu_ref[...]

    lf_col = _log_sigmoid(_mm(hb, wf_ref[...]) + fbrow_ref[...])
    lf_row = _log_sigmoid(_mm(wft_ref[...], hb, dims=_NT) + fbcol_ref[...])
    c_col = _mm(trilo_ref[...], lf_col, 1, 3) + carry_row[...]
    c_row = _mm(lf_row, triup_ref[...], 3, 1) + carry_col[:, 0:1]
    ccol_ref[0] = c_col
    for jj in range(tm // tk):
        crow_ref[0, jj] = c_row[:, jj * tk:(jj + 1) * tk]
    carry_row[...] = c_col[tm - 1:tm, :]
    carry_col[...] = jnp.broadcast_to(c_row[:, tm - 1:tm], carry_col.shape)


def _in_proj(x, mod, g1, wqkv, wf, wft, wrw, fbrow, fbcol, qg, kg, e, mu, tm, tk):
    bsz, seq, d = x.shape
    nt = seq // tm
    row = lax.broadcasted_iota(jnp.int32, (tm, tm), 0)
    col = lax.broadcasted_iota(jnp.int32, (tm, tm), 1)
    trilo = (col <= row).astype(BF16)
    triup = (row <= col).astype(BF16)
    const = lambda shape: pl.BlockSpec(shape, lambda b, t: (0,) * len(shape))
    kern = functools.partial(_in_proj_kernel, tm=tm, tk=tk)
    return pl.pallas_call(
        kern,
        grid=(bsz, nt),
        in_specs=[
            pl.BlockSpec((1, tm, d), lambda b, t: (b, t, 0)),
            pl.BlockSpec((1, 6, d), lambda b, t: (b, 0, 0)),
            const((1, d)),
            const(wqkv.shape), const(wf.shape), const(wft.shape), const(wrw.shape),
            const(fbrow.shape), const(fbcol.shape), const(qg.shape), const(kg.shape),
            const(e.shape), const((tm, tm)), const((tm, tm)), const(mu.shape),
        ],
        out_specs=[
            pl.BlockSpec((1, tm, FOX_WIDTH), lambda b, t: (b, t, 0)),
            pl.BlockSpec((1, tm, FOX_WIDTH), lambda b, t: (b, t, 0)),
            pl.BlockSpec((1, tm, FOX_WIDTH), lambda b, t: (b, t, 0)),
            pl.BlockSpec((1, tm, LANES), lambda b, t: (b, t, 0)),
            pl.BlockSpec((1, tm // tk, 16, tk), lambda b, t: (b, t, 0, 0)),
            pl.BlockSpec((1, tm, RW_COLS), lambda b, t: (b, t, 0)),
        ],
        out_shape=[
            jax.ShapeDtypeStruct((bsz, seq, FOX_WIDTH), BF16),
            jax.ShapeDtypeStruct((bsz, seq, FOX_WIDTH), BF16),
            jax.ShapeDtypeStruct((bsz, seq, FOX_WIDTH), BF16),
            jax.ShapeDtypeStruct((bsz, seq, LANES), F32),
            jax.ShapeDtypeStruct((bsz, seq // tk, 16, tk), F32),
            jax.ShapeDtypeStruct((bsz, seq, RW_COLS), F32),
        ],
        scratch_shapes=[pltpu.VMEM((1, LANES), F32), pltpu.VMEM((16, LANES), F32),
                        pltpu.VMEM((8, RW_COLS), F32)],
        compiler_params=pltpu.CompilerParams(
            dimension_semantics=("arbitrary", "arbitrary"), vmem_limit_bytes=VMEM_LIMIT),
        name="in_proj",
    )(x, mod, g1, wqkv, wf, wft, wrw, fbrow, fbcol, qg, kg, e, trilo, triup, mu)


def _fox_bounded(bound, q_ref, k_ref, v_ref, ccol_ref, crow_ref, o_ref, acc_ref, qm_ref, cb_ref,
                 *, tq, tk):
    i = pl.program_id(1)
    nsub = tq // tk
    lane = lax.broadcasted_iota(jnp.int32, (tq, LANES), 1)
    low = lane < HEAD_DIM
    klow = lax.broadcasted_iota(jnp.int32, (tk, LANES), 1) < HEAD_DIM
    one_lo = jnp.where(klow, 1.0, 0.0).astype(BF16)
    one_hi = jnp.where(klow, 0.0, 1.0).astype(BF16)

    ccol = ccol_ref[0] - bound
    for hp in range(PAIRS):
        q2 = q_ref[0, :, LANES * hp:LANES * (hp + 1)]
        zero = jnp.zeros_like(q2)
        qm_ref[2 * hp] = jnp.where(low, q2, zero)
        qm_ref[2 * hp + 1] = jnp.where(low, zero, q2)
        for hh in range(2):
            h = 2 * hp + hh
            cb_ref[h] = jnp.broadcast_to(ccol[:, h:h + 1], (tq, LANES))
    acc_ref[...] = jnp.zeros_like(acc_ref)

    def step(j, sub):
        r0 = 0 if sub is None else sub * tk
        rows = tq - r0
        ks = pl.multiple_of(j * tk, tk)
        scores = []
        for hp in range(PAIRS):
            k2 = k_ref[0, pl.ds(ks, tk), LANES * hp:LANES * (hp + 1)]
            for hh in range(2):
                scores.append(_mm(qm_ref[2 * hp + hh, r0:tq, :], k2, dims=_NT))
        if sub is not None:
            causal = (lax.broadcasted_iota(jnp.int32, (rows, tk), 1)
                      <= lax.broadcasted_iota(jnp.int32, (rows, tk), 0))
        probs = []
        for h in range(FOX_HEADS):
            cb = cb_ref[h, r0:tq, :]
            bias = jnp.concatenate([cb] * (tk // LANES), axis=1) - crow_ref[0, j, h:h + 1, :]
            s = scores[h] + bias
            if sub is not None:
                s = jnp.where(causal, s, NEG_BIG)
            probs.append(jnp.exp(s).astype(BF16))
        for hp in range(PAIRS):
            v2 = v_ref[0, pl.ds(ks, tk), LANES * hp:LANES * (hp + 1)]
            zero = jnp.zeros_like(v2)
            vaug = jnp.concatenate([
                jnp.concatenate([jnp.where(klow, v2, zero), one_lo], axis=1),
                jnp.concatenate([jnp.where(klow, zero, v2), one_hi], axis=1)], axis=0)
            acc_ref[hp, r0:tq, :] += _mm(
                jnp.concatenate([probs[2 * hp], probs[2 * hp + 1]], axis=1), vaug)

    def body(j, carry):
        step(j, None)
        return carry

    lax.fori_loop(0, i * nsub, body, 0)
    for m in range(nsub):
        step(i * nsub + m, m)
    for hp in range(PAIRS):
        a = acc_ref[hp]
        o = a[:, 0:LANES] / a[:, LANES:2 * LANES]
        o_ref[0, :, LANES * hp:LANES * (hp + 1)] = o.astype(BF16)


def _fox_kernel(flag_ref, bound_ref, q_ref, k_ref, v_ref, ccol_ref, crow_ref, o_ref, acc_ref,
                qm_ref, cb_ref, *, tq, tk):
    @pl.when(flag_ref[0] == 1)
    def _():
        _fox_bounded(bound_ref[0], q_ref, k_ref, v_ref, ccol_ref, crow_ref, o_ref, acc_ref,
                     qm_ref, cb_ref, tq=tq, tk=tk)

    @pl.when(flag_ref[0] == 0)
    def _():
        _fox_running_max(q_ref, k_ref, v_ref, ccol_ref, crow_ref, o_ref, tq=tq, tk=tk)


def _fox_running_max(q_ref, k_ref, v_ref, ccol_ref, crow_ref, o_ref, *, tq, tk):
    i = pl.program_id(1)
    nsub = tq // tk
    lane = lax.broadcasted_iota(jnp.int32, (tq, LANES), 1)
    low = lane < HEAD_DIM
    row = lax.broadcasted_iota(jnp.int32, (tq, tk), 0)
    col = lax.broadcasted_iota(jnp.int32, (tq, tk), 1)
    ccol = ccol_ref[0]

    for hp in range(PAIRS):
        ls = slice(LANES * hp, LANES * (hp + 1))
        q2 = q_ref[0, :, ls]
        zero = jnp.zeros_like(q2)
        qm = (jnp.where(low, q2, zero), jnp.where(low, zero, q2))
        cc = tuple(ccol[:, 2 * hp + hh:2 * hp + hh + 1] for hh in range(2))

        def step(j, carry, sub, ls=ls, qm=qm, cc=cc, hp=hp):
            ks = pl.multiple_of(j * tk, tk)
            k2 = k_ref[0, pl.ds(ks, tk), ls]
            v2 = v_ref[0, pl.ds(ks, tk), ls]
            new = []
            for hh in range(2):
                m, l, acc = carry[hh]
                s = _mm(qm[hh], k2, dims=_NT)
                cr = crow_ref[0, j, 2 * hp + hh:2 * hp + hh + 1, :]
                s = s + (cc[hh] - cr)
                if sub is not None:
                    s = jnp.where(col + sub * tk <= row, s, NEG_BIG)
                m_new = jnp.maximum(m, jnp.max(s, axis=1, keepdims=True))
                alpha = jnp.exp(m - m_new)
                p = jnp.exp(s - m_new)
                l_new = alpha * l + jnp.sum(p, axis=1, keepdims=True)
                acc_new = alpha * acc + _mm(p.astype(BF16), v2)
                new.append((m_new, l_new, acc_new))
            return tuple(new)

        init_one = (jnp.full((tq, 1), NEG_BIG, F32), jnp.zeros((tq, 1), F32),
                    jnp.zeros((tq, LANES), F32))
        carry = lax.fori_loop(0, i * nsub, lambda j, c: step(j, c, None), (init_one, init_one))
        for sub in range(nsub):
            carry = step(i * nsub + sub, carry, sub)
        o0 = carry[0][2] / carry[0][1]
        o1 = carry[1][2] / carry[1][1]
        o_ref[0, :, ls] = jnp.where(low, o0, o1).astype(BF16)


def _fox(flag, bound, q, k, v, ccol, crow, tq, tk):
    bsz, seq, _ = q.shape
    kern = functools.partial(_fox_kernel, tq=tq, tk=tk)
    return pl.pallas_call(
        kern,
        grid=(bsz, seq // tq),
        in_specs=[
            pl.BlockSpec(memory_space=pltpu.SMEM),
            pl.BlockSpec(memory_space=pltpu.SMEM),
            pl.BlockSpec((1, tq, FOX_WIDTH), lambda b, i: (b, i, 0)),
            pl.BlockSpec((1, seq, FOX_WIDTH), lambda b, i: (b, 0, 0)),
            pl.BlockSpec((1, seq, FOX_WIDTH), lambda b, i: (b, 0, 0)),
            pl.BlockSpec((1, tq, LANES), lambda b, i: (b, i, 0)),
            pl.BlockSpec((1, seq // tk, 16, tk), lambda b, i: (b, 0, 0, 0)),
        ],
        out_specs=pl.BlockSpec((1, tq, FOX_WIDTH), lambda b, i: (b, i, 0)),
        out_shape=jax.ShapeDtypeStruct((bsz, seq, FOX_WIDTH), BF16),
        scratch_shapes=[pltpu.VMEM((PAIRS, tq, 2 * LANES), F32),
                        pltpu.VMEM((FOX_HEADS, tq, LANES), BF16),
                        pltpu.VMEM((FOX_HEADS, tq, LANES), F32)],
        compiler_params=pltpu.CompilerParams(
            dimension_semantics=("arbitrary", "arbitrary"), vmem_limit_bytes=VMEM_LIMIT),
        name="fox",
    )(flag, bound, q, k, v, ccol, crow)


def _unit_tri_inverses(mats, bs, passes, tick=lambda: None):
    c, w = mats[0].shape
    n = range(len(mats))
    lane_cache = {}

    def lane_ids(s):
        if s not in lane_cache:
            lane_cache[s] = lax.broadcasted_iota(jnp.int32, (s, w), 1)
        return lane_cache[s]

    def terms(x):
        return _split_bf16(x, passes)

    def dot_terms(a_t, b_t):
        out = None
        for i, ai in enumerate(a_t):
            for j, bj in enumerate(b_t):
                if i + j < max(len(a_t), len(b_t)):
                    t = lax.dot_general(ai, bj, _NN, preferred_element_type=F32)
                    out = t if out is None else out + t
        return out

    def block_rows(x, s, offset):
        lb = jnp.right_shift(lane_ids(s), s.bit_length() - 1)
        zero = jnp.zeros((s, w), BF16)
        keep = [lb == j for j in range(w // s)]
        out = []
        for t in terms(x):
            rows = []
            for j in range(w // s):
                if offset and j % 2 == 0:
                    rows.append(zero)
                else:
                    rows.append(jnp.where(keep[j - offset], t, zero))
            out.append(jnp.concatenate(rows, axis=0))
        return out

    def mm(a, b_terms):
        return dot_terms(terms(a), b_terms)

    s = bs
    sh = s.bit_length() - 1
    in_mat = jnp.bitwise_and(lane_ids(s), c - 1)
    q = []
    for a2 in mats:
        d = a2[0:s, :]
        for r in range(1, c // s):
            d = jnp.where(jnp.right_shift(in_mat, sh) == r, a2[r * s:(r + 1) * s, :], d)
        q.append(d)
    p = [mm(q[i], block_rows(q[i], s, 0)) for i in n]
    tick()
    for _ in range(s.bit_length() - 3):
        both = [mm(jnp.concatenate([q[i], p[i]], axis=0), block_rows(p[i], s, 0)) for i in n]
        q = [q[i] + p[i] + both[i][0:s] for i in n]
        p = [both[i][s:2 * s] for i in n]
        tick()
    q = [q[i] + p[i] + mm(q[i], block_rows(p[i], s, 0)) for i in n]
    tick()

    while s < c:
        sh = s.bit_length() - 1
        lane = lane_ids(s)
        first = jnp.bitwise_and(lane, s) == 0
        pair_id = jnp.right_shift(jnp.bitwise_and(lane, c - 1), sh + 1)
        l21 = []
        for a2 in mats:
            z = jnp.zeros((s, w), F32)
            for m in range(c // (2 * s)):
                rows = a2[(2 * m + 1) * s:(2 * m + 2) * s, :]
                z = jnp.where(pair_id == m, jnp.where(first, rows, 0.0), z)
            l21.append(z)
        x = [l21[i] + mm(l21[i], block_rows(q[i], s, 0)) for i in n]
        tick()
        t21 = [x[i] + mm(q[i], block_rows(x[i], s, 1)) for i in n]
        tick()
        q = [jnp.concatenate([jnp.where(first, q[i], 0.0), jnp.where(first, t21[i], q[i])], axis=0)
             for i in n]
        s *= 2
    return q


def _head_sums(x, e, pa):
    wd = e.shape[0]
    return jnp.concatenate(
        [_mm(x[:, j:j + wd], e, pa, 1) for j in range(0, x.shape[1], wd)], axis=1)


def _rwkv_kernel(rw_ref, w0_ref, w2_ref, a0_ref, a2_ref, g2_ref, kk_ref, ka_ref,
                 rk_ref, lg_ref, lb_ref, e_ref, tri_ref, o_ref, s_ref, *, prec, nb, group):
    cch = CHUNK
    rows = group * cch
    w = RWKV_WIDTH
    c = pl.program_id(1)
    pg, pd, ps = prec
    e = e_ref[...]

    @pl.when(c == 0)
    def _():
        s_ref[...] = jnp.zeros_like(s_ref)

    lane = lax.broadcasted_iota(jnp.int32, (cch, LANES), 1)
    low = lane < HEAD_DIM
    ri = lax.broadcasted_iota(jnp.int32, (cch, cch), 0)
    ci = lax.broadcasted_iota(jnp.int32, (cch, cch), 1)
    strict = ci < ri
    incl = ci <= ri
    blockdiag = (ri < HEAD_DIM) == (ci < HEAD_DIM)
    sl = [slice(LANES * hp, LANES * (hp + 1)) for hp in range(PAIRS)]
    rs = [slice(cch * i, cch * (i + 1)) for i in range(group)]
    combos = [(i, hp) for i in range(group) for hp in range(PAIRS)]
    n = range(len(combos))

    def halves(z):
        zero = jnp.zeros_like(z)
        return jnp.concatenate([jnp.where(low, z, zero), jnp.where(low, zero, z)], axis=0)

    def preparation(gi, out):
        base = gi * group

        def shifted(lo, hi):
            return rw_ref[base:base + group, :, lo:hi].reshape(rows, hi - lo)

        lora = shifted(OFF_WDN, RW_COLS)
        wdn = lora[:, 0:LANES]
        adn = lora[:, OFF_ADN - OFF_WDN:OFF_ADN - OFF_WDN + LANES]
        gdn = lora[:, OFF_GDN - OFF_WDN:OFF_GDN - OFF_WDN + 2 * LANES]
        wlog = _log_sigmoid(w0_ref[...] + _mm(jnp.tanh(wdn), w2_ref[...])) - 0.5
        logd = -jnp.exp(wlog)
        a = _sigmoid(a0_ref[...] + _mm(adn, a2_ref[...]))
        out["g"] = _mm(_sigmoid(gdn), g2_ref[...])
        yield
        cum = _mm(tri_ref[...], logd, 1, 2)
        lasts = [cum[cch * (i + 1) - 1:cch * (i + 1), :] for i in range(group)]
        clast = jnp.concatenate([jnp.broadcast_to(z, (cch, w)) for z in lasts], axis=0)
        out["pc"] = [jnp.exp(z) for z in lasts]
        k_n = shifted(w, 2 * w)
        kk = k_n * kk_ref[...]
        kk = kk * lax.rsqrt(jnp.maximum(_head_sums(kk * kk, e, 2), L2_EPS * L2_EPS))
        k_n = k_n * (1.0 + (a - 1.0) * ka_ref[...])
        yield
        avec = -kk
        bvec = kk * a
        einv = jnp.exp(-cum)
        out["at"] = (avec * jnp.exp(cum - logd)).astype(BF16)
        out["bt"] = (bvec * einv).astype(BF16)
        out["kt"] = (k_n * einv).astype(BF16)
        yield
        etail = jnp.exp(clast - cum)
        out["bh"] = (bvec * etail).astype(BF16)
        out["kh"] = (k_n * etail).astype(BF16)
        yield
        r_n = shifted(0, w)
        out["rt"] = (r_n * jnp.exp(cum)).astype(BF16)
        bonus = _head_sums(r_n * k_n * rk_ref[...], e, 1)
        yield
        v_n = shifted(2 * w, 3 * w)
        out["v"] = v_n.astype(BF16)
        out["bv"] = bonus * v_n
        yield

    def algebra(gi, ops, tick):
        base = gi * group

        def blk(name, j):
            i, hp = combos[j]
            return ops[name][rs[i], sl[hp]]

        aab, aak, arbk = [], [], []
        for j in n:
            rb = jnp.concatenate([blk("bt", j), blk("kt", j)], axis=0)
            blocks = []
            for hh in range(2):
                sel = low if hh == 0 else jnp.logical_not(low)
                zero = jnp.zeros((cch, LANES), BF16)
                la = jnp.concatenate([jnp.where(sel, blk("at", j), zero),
                                      jnp.where(sel, blk("rt", j), zero)], axis=0)
                blocks.append(_mm(la, rb, pg, pg, dims=_NT))
            aab.append(jnp.concatenate(
                [jnp.where(strict, gm[0:cch, 0:cch], 0.0) for gm in blocks], axis=1))
            aak.append(jnp.concatenate(
                [jnp.where(strict, gm[0:cch, cch:2 * cch], 0.0) for gm in blocks], axis=1))
            arbk.append(jnp.concatenate(
                [jnp.where(incl, gm[cch:2 * cch, 0:cch], 0.0) for gm in blocks]
                + [jnp.where(incl, gm[cch:2 * cch, cch:2 * cch], 0.0) for gm in blocks], axis=1))
        tick()
        qm = _unit_tri_inverses(aab, INV_BLOCK, pd, tick)

        sp = [s_ref[base + i, hp] for i, hp in combos]
        vst = [halves(blk("v", j)) for j in n]
        rhs = [_mm(blk("at", j), sp[j], ps, ps, dims=_NT) + _mm(aak[j], vst[j], ps, ps) for j in n]
        tick()
        u = [rhs[j] + _mm(qm[j], halves(rhs[j]), ps, ps) for j in n]
        tick()
        ys = [_mm(blk("rt", j), sp[j], ps, ps, dims=_NT)
              + _mm(arbk[j], jnp.concatenate([halves(u[j]).astype(BF16), vst[j]], axis=0), ps, ps)
              for j in n]
        tick()
        for j in n:
            i, hp = combos[j]
            uv = jnp.concatenate([u[j], blk("v", j).astype(F32)], axis=0)
            bk = jnp.concatenate([blk("bh", j), blk("kh", j)], axis=0)
            upd = _mm(uv.T, bk, ps, ps)
            s_ref[base + i, hp] = (sp[j] * ops["pc"][i][:, sl[hp]]
                                   + jnp.where(blockdiag, upd, 0.0))
        tick()
        return jnp.concatenate(
            [jnp.concatenate(ys[PAIRS * i:PAIRS * (i + 1)], axis=1) for i in range(group)], axis=0)

    def finish(gi, y, ops):
        base = gi * group
        inv_n = 1.0 / HEAD_DIM
        mean = _head_sums(y, e, 1) * inv_n
        d = y - mean
        yield
        var = _head_sums(d * d, e, 1) * inv_n
        yn = d * lax.rsqrt(var + GN_EPS) * lg_ref[...] + lb_ref[...]
        yield
        o_ref[base:base + group] = ((yn + ops["bv"]) * ops["g"]).astype(BF16).reshape(group, cch, w)
        yield

    def drain(gen):
        if gen is not None:
            for _ in gen:
                pass

    ngroups = nb // group
    outs = [dict() for _ in range(ngroups)]
    preps = [preparation(gi, outs[gi]) for gi in range(ngroups)]
    drain(preps[0])
    fin = None
    for gi in range(ngroups):
        nxt = preps[gi + 1] if gi + 1 < ngroups else None
        side = [g for g in (fin, nxt) if g is not None]

        def tick(side=side):
            for g in side:
                try:
                    next(g)
                    return
                except StopIteration:
                    continue

        y = algebra(gi, outs[gi], tick)
        drain(fin)
        drain(nxt)
        fin = finish(gi, y, outs[gi])
    drain(fin)


def _rwkv(rw, w0, w2, a0, a2, g2, k_k, k_a, r_k, lnx_g, lnx_b, e, prec):
    bsz, seq, _ = rw.shape
    cch = CHUNK
    nb = RWKV_SEQS if bsz % RWKV_SEQS == 0 else 1
    group = RWKV_GROUP if nb % RWKV_GROUP == 0 else 1
    row = lax.broadcasted_iota(jnp.int32, (group * cch, group * cch), 0)
    col = lax.broadcasted_iota(jnp.int32, (group * cch, group * cch), 1)
    tri = ((col <= row) & (row // cch == col // cch)).astype(BF16)
    const = lambda shape: pl.BlockSpec(shape, lambda b, t: (0,) * len(shape))
    args = (w0, w2, a0, a2, g2, k_k, k_a, r_k, lnx_g, lnx_b, e, tri)
    return pl.pallas_call(
        functools.partial(_rwkv_kernel, prec=prec, nb=nb, group=group),
        grid=(bsz // nb, seq // cch),
        in_specs=[pl.BlockSpec((nb, cch, RW_COLS), lambda b, t: (b, t, 0))]
        + [const(a.shape) for a in args],
        out_specs=pl.BlockSpec((nb, cch, RWKV_WIDTH), lambda b, t: (b, t, 0)),
        out_shape=jax.ShapeDtypeStruct((bsz, seq, RWKV_WIDTH), BF16),
        scratch_shapes=[pltpu.VMEM((nb, PAIRS, LANES, LANES), F32)],
        compiler_params=pltpu.CompilerParams(
            dimension_semantics=("arbitrary", "arbitrary"), vmem_limit_bytes=VMEM_LIMIT),
        name="rwkv",
    )(rw, *args)


def _out_ffn_kernel(of_ref, or_ref, x_ref, mod_ref, g2_ref, wt_ref, wb_ref, wg_ref, wu_ref, wd_ref,
                    o_ref, *, slabs):
    mod = mod_ref[0]
    gt1, sh2, sc2, gt2 = mod[2:3, :], mod[3:4, :], mod[4:5, :], mod[5:6, :]
    mix = _mm(of_ref[0], wt_ref[...]) + _mm(or_ref[0], wb_ref[...])
    x1 = x_ref[0] + gt1 * mix
    ms = jnp.mean(x1 * x1, axis=-1, keepdims=True)
    y = x1 * lax.rsqrt(ms + RMS_EPS) * g2_ref[...]
    h2 = (y * (1.0 + sc2) + sh2).astype(BF16)
    acc = None
    for lo, hi in slabs:
        gate = _mm(h2, wg_ref[:, lo:hi])
        up = _mm(h2, wu_ref[:, lo:hi])
        act = (gate * _sigmoid(gate) * up).astype(BF16)
        part = _mm(act, wd_ref[lo:hi, :])
        acc = part if acc is None else acc + part
    o_ref[0] = x1 + gt2 * acc


def _out_ffn(o_fox, o_rwkv, x, mod, g2, w_top, w_bot, wg, wu, wd, tm):
    bsz, seq, d = x.shape
    dff = wg.shape[1]
    tiles = dff // MXU_DIM if dff % MXU_DIM == 0 else 1
    cut = (tiles // 2) * (dff // tiles)
    slabs = ((0, cut), (cut, dff)) if cut else ((0, dff),)
    resident = lambda shape: pl.BlockSpec(shape, lambda b, t: (0,) * len(shape),
                                          pipeline_mode=pl.Buffered(1))
    return pl.pallas_call(
        functools.partial(_out_ffn_kernel, slabs=slabs),
        grid=(bsz, seq // tm),
        in_specs=[
            pl.BlockSpec((1, tm, FOX_WIDTH), lambda b, t: (b, t, 0)),
            pl.BlockSpec((1, tm, RWKV_WIDTH), lambda b, t: (b, t, 0)),
            pl.BlockSpec((1, tm, d), lambda b, t: (b, t, 0)),
            pl.BlockSpec((1, 6, d), lambda b, t: (b, 0, 0)),
            resident((1, d)), resident(w_top.shape), resident(w_bot.shape),
            resident(wg.shape), resident(wu.shape), resident(wd.shape),
        ],
        out_specs=pl.BlockSpec((1, tm, d), lambda b, t: (b, t, 0)),
        out_shape=jax.ShapeDtypeStruct((bsz, seq, d), F32),
        compiler_params=pltpu.CompilerParams(
            dimension_semantics=("arbitrary", "arbitrary"), vmem_limit_bytes=VMEM_LIMIT),
        name="out_ffn",
    )(o_fox, o_rwkv, x, mod, g2, w_top, w_bot, wg, wu, wd)


def _pad_cols(w, n):
    return jnp.pad(w, ((0, 0), (0, n - w.shape[1])))


def _pad_rows(w, n):
    return jnp.pad(w, ((0, n - w.shape[0]), (0, 0)))


def _layer(x, mod, norm1_g, norm2_g, w_in, fox_f_bias, fox_q_gain, fox_k_gain, rwkv_mu, rwkv_w0,
           rwkv_w2, rwkv_a0, rwkv_a2, rwkv_g2, rwkv_k_k, rwkv_k_a, rwkv_r_k, rwkv_lnx_g,
           rwkv_lnx_b, w_out, ffn_w_gate, ffn_w_up, ffn_w_down, *, tm, tq, tk, prec):
    bsz, seq, d = x.shape
    w = RWKV_WIDTH
    nfox = 3 * FOX_WIDTH + FOX_HEADS

    wqkv = w_in[:, 0:3 * FOX_WIDTH].astype(BF16)
    wf = _pad_cols(w_in[:, 3 * FOX_WIDTH:nfox], LANES).astype(BF16)
    wft = _pad_rows(w_in[:, 3 * FOX_WIDTH:nfox].T, 16).astype(BF16)
    wr = w_in[:, nfox:]
    o_w, o_a, o_g = 3 * w, 3 * w + DECAY_LORA, 3 * w + DECAY_LORA + A_LORA
    wrw = jnp.concatenate([
        wr[:, 0:3 * w],
        _pad_cols(wr[:, o_w:o_a], LANES),
        _pad_cols(wr[:, o_a:o_g], LANES),
        _pad_cols(wr[:, o_g:], 2 * LANES)], axis=1).astype(BF16)
    mu = rwkv_mu.reshape(1, -1)
    mu_p = jnp.concatenate([
        mu[:, 0:3 * w],
        _pad_cols(mu[:, o_w:o_a], LANES),
        _pad_cols(mu[:, o_a:o_g], LANES),
        _pad_cols(mu[:, o_g:], 2 * LANES)], axis=1)
    fbrow = _pad_cols(fox_f_bias.reshape(1, -1), LANES)
    fbcol = _pad_rows(fox_f_bias.reshape(-1, 1), 16)
    qg = jnp.tile(fox_q_gain, (1, 1)).reshape(1, FOX_WIDTH)
    kg = fox_k_gain.reshape(1, FOX_WIDTH)
    hi = lax.broadcasted_iota(jnp.int32, (MXU_DIM, MXU_DIM), 0) // HEAD_DIM
    hj = lax.broadcasted_iota(jnp.int32, (MXU_DIM, MXU_DIM), 1) // HEAD_DIM
    e = (hi == hj).astype(BF16)

    q, k, v, ccol, crow, rw = _in_proj(
        x, mod, norm1_g.reshape(1, d), wqkv, wf, wft, wrw, fbrow, fbcol, qg, kg, e, mu_p, tm, tk)
    bound = (1.05 * HEAD_DIM ** 0.5) * jnp.max(jnp.abs(fox_q_gain)) * jnp.max(jnp.abs(fox_k_gain))
    flag = (bound <= FOX_BOUND_MAX).astype(jnp.int32)
    o_fox = _fox(flag.reshape(1), bound.astype(F32).reshape(1), q, k, v, ccol, crow, tq, tk)
    o_rwkv = _rwkv(
        rw, rwkv_w0.reshape(1, w), _pad_rows(rwkv_w2, LANES).astype(BF16),
        rwkv_a0.reshape(1, w), _pad_rows(rwkv_a2, LANES).astype(BF16),
        _pad_rows(rwkv_g2, 2 * LANES).astype(BF16), rwkv_k_k.reshape(1, w), rwkv_k_a.reshape(1, w),
        rwkv_r_k.reshape(1, w), rwkv_lnx_g.reshape(1, w), rwkv_lnx_b.reshape(1, w), e, prec)
    wo = w_out.astype(BF16)
    return _out_ffn(o_fox, o_rwkv, x, mod, norm2_g.reshape(1, d), wo[0:FOX_WIDTH], wo[FOX_WIDTH:],
                    ffn_w_gate.astype(BF16), ffn_w_up.astype(BF16), ffn_w_down.astype(BF16),
                    tm)


def kernel(x, c, ada_w, ada_b, norm1_g, norm2_g, w_in, fox_f_bias, fox_q_gain, fox_k_gain, rwkv_mu,
           rwkv_w0, rwkv_w2, rwkv_a0, rwkv_a2, rwkv_g2, rwkv_k_k, rwkv_k_a, rwkv_r_k, rwkv_lnx_g,
           rwkv_lnx_b, w_out, ffn_w_gate, ffn_w_up, ffn_w_down):
    bsz, seq, d = x.shape
    depth = ada_w.shape[0]
    tm = min(ROW_TILE, seq)
    tk = min(KEY_TILE, seq)
    tq = min(QUERY_TILE, seq)
    assert seq % tm == 0 and seq % tq == 0 and tq % tk == 0 and seq % CHUNK == 0
    for l in range(depth):
        mod = _ada(c, ada_w[l], ada_b[l]).reshape(bsz, 6, d)
        x = _layer(x, mod, norm1_g[l], norm2_g[l], w_in[l], fox_f_bias[l], fox_q_gain[l],
                   fox_k_gain[l], rwkv_mu[l], rwkv_w0[l], rwkv_w2[l], rwkv_a0[l], rwkv_a2[l],
                   rwkv_g2[l], rwkv_k_k[l], rwkv_k_a[l], rwkv_r_k[l], rwkv_lnx_g[l],
                   rwkv_lnx_b[l], w_out[l], ffn_w_gate[l], ffn_w_up[l], ffn_w_down[l],
                   tm=tm, tq=tq, tk=tk, prec=RWKV_PASSES)
    return x
```

```python
import functools

import jax
import jax.numpy as jnp
from jax import lax
from jax.experimental import pallas as pl
from jax.experimental.pallas import tpu as pltpu

F32 = jnp.float32
BF16 = jnp.bfloat16

HEAD_DIM = 64
FOX_HEADS = 8
RWKV_HEADS = 8
FOX_WIDTH = FOX_HEADS * HEAD_DIM
RWKV_WIDTH = RWKV_HEADS * HEAD_DIM
DECAY_LORA = 64
A_LORA = 64
GATE_LORA = 160
RMS_EPS = 1e-6
GN_EPS = 64e-5
L2_EPS = 1e-12

LANES = 128
MXU_DIM = 256
PAIRS = FOX_HEADS // 2
RWKV_SEQS = 4
RWKV_GROUP = 2
INV_BLOCK = 64
RW_COLS = 2048
OFF_WDN, OFF_ADN, OFF_GDN = 1536, 1664, 1792
CHUNK = 128
NEG_BIG = -1e30
RWKV_PASSES = (1, 1, 1)
FOX_BOUND_MAX = 30.0
ROW_TILE = 512
QUERY_TILE = 1024
KEY_TILE = 256
VMEM_LIMIT = 56 * 1024 * 1024


def _split_bf16(x, n):
    if x.dtype == BF16:
        return [x]
    parts = []
    r = x
    for i in range(n):
        p = r.astype(BF16)
        parts.append(p)
        if i < n - 1:
            r = r - p.astype(F32)
    return parts


_NN = (((1,), (0,)), ((), ()))
_NT = (((1,), (1,)), ((), ()))
_TN = (((0,), (0,)), ((), ()))


def _mm(a, b, pa=1, pb=1, dims=_NN):
    a_parts = _split_bf16(a, pa)
    b_parts = _split_bf16(b, pb)
    order = max(len(a_parts), len(b_parts))
    out = None
    for i, ai in enumerate(a_parts):
        for j, bj in enumerate(b_parts):
            if i + j >= order:
                continue
            t = lax.dot_general(ai, bj, dims, preferred_element_type=F32)
            out = t if out is None else out + t
    return out


def _log_sigmoid(z):
    return jnp.minimum(z, 0.0) - jnp.log(1.0 + jnp.exp(-jnp.abs(z)))


def _sigmoid(z):
    return 1.0 / (1.0 + jnp.exp(-z))


def _ada_kernel(c_ref, w_ref, b_ref, o_ref):
    c = c_ref[...]
    cond = c * _sigmoid(c)
    o_ref[...] = _mm(cond, w_ref[...], 2, 2) + b_ref[...]


def _ada(c, ada_w, ada_b):
    bsz, d = c.shape
    n = ada_w.shape[1]
    tn = 512
    return pl.pallas_call(
        _ada_kernel,
        grid=(n // tn,),
        in_specs=[
            pl.BlockSpec((bsz, d), lambda j: (0, 0)),
            pl.BlockSpec((d, tn), lambda j: (0, j)),
            pl.BlockSpec((1, tn), lambda j: (0, j)),
        ],
        out_specs=pl.BlockSpec((bsz, tn), lambda j: (0, j)),
        out_shape=jax.ShapeDtypeStruct((bsz, n), F32),
        compiler_params=pltpu.CompilerParams(dimension_semantics=("arbitrary",)),
        name="ada",
    )(c, ada_w, ada_b.reshape(1, n))


def _in_proj_kernel(x_ref, mod_ref, g1_ref, wqkv_ref, wf_ref, wft_ref, wrw_ref,
                    fbrow_ref, fbcol_ref, qg_ref, kg_ref, e_ref, trilo_ref, triup_ref, mu_ref,
                    q_ref, k_ref, v_ref, ccol_ref, crow_ref, rw_ref,
                    carry_row, carry_col, prev_ref, *, tm, tk):
    t = pl.program_id(1)

    @pl.when(t == 0)
    def _():
        carry_row[...] = jnp.zeros_like(carry_row)
        carry_col[...] = jnp.zeros_like(carry_col)
        prev_ref[...] = jnp.zeros_like(prev_ref)

    x = x_ref[0]
    mod = mod_ref[0]
    sh1 = mod[0:1, :]
    sc1 = mod[1:2, :]
    ms = jnp.mean(x * x, axis=-1, keepdims=True)
    y = x * lax.rsqrt(ms + RMS_EPS) * g1_ref[...]
    hb = (y * (1.0 + sc1) + sh1).astype(BF16)

    qkv = _mm(hb, wqkv_ref[...])
    e = e_ref[...]
    q = qkv[:, 0:FOX_WIDTH]
    k = qkv[:, FOX_WIDTH:2 * FOX_WIDTH]
    qms = _head_sums(q * q, e, 1) * (1.0 / HEAD_DIM)
    kms = _head_sums(k * k, e, 1) * (1.0 / HEAD_DIM)
    q_ref[0] = (q * lax.rsqrt(qms + RMS_EPS) * qg_ref[...] * (HEAD_DIM ** -0.5)).astype(BF16)
    k_ref[0] = (k * lax.rsqrt(kms + RMS_EPS) * kg_ref[...]).astype(BF16)
    v_ref[0] = qkv[:, 2 * FOX_WIDTH:3 * FOX_WIDTH].astype(BF16)

    rw = _mm(hb, wrw_ref[...])
    rolled = pltpu.roll(rw, 1, 0)
    first_row = lax.broadcasted_iota(jnp.int32, (8, 1), 0) == 0
    top = jnp.where(first_row, prev_ref[0:1, :], rolled[0:8])
    prev_ref[0:1, :] = rw[tm - 1:tm, :]
    rw_ref[0] = rw + (jnp.concatenate([top, rolled[8:tm]], axis=0) - rw) * mu_ref[...]

    lf_col = _log_sigmoid(_mm(hb, wf_ref[...]) + fbrow_ref[...])
    lf_row = _log_sigmoid(_mm(wft_ref[...], hb, dims=_NT) + fbcol_ref[...])
    c_col = _mm(trilo_ref[...], lf_col, 1, 3) + carry_row[...]
    c_row = _mm(lf_row, triup_ref[...], 3, 1) + carry_col[:, 0:1]
    ccol_ref[0] = c_col
    for jj in range(tm // tk):
        crow_ref[0, jj] = c_row[:, jj * tk:(jj + 1) * tk]
    carry_row[...] = c_col[tm - 1:tm, :]
    carry_col[...] = jnp.broadcast_to(c_row[:, tm - 1:tm], carry_col.shape)


def _in_proj(x, mod, g1, wqkv, wf, wft, wrw, fbrow, fbcol, qg, kg, e, mu, tm, tk):
    bsz, seq, d = x.shape
    nt = seq // tm
    row = lax.broadcasted_iota(jnp.int32, (tm, tm), 0)
    col = lax.broadcasted_iota(jnp.int32, (tm, tm), 1)
    trilo = (col <= row).astype(BF16)
    triup = (row <= col).astype(BF16)
    const = lambda shape: pl.BlockSpec(shape, lambda b, t: (0,) * len(shape))
    kern = functools.partial(_in_proj_kernel, tm=tm, tk=tk)
    return pl.pallas_call(
        kern,
        grid=(bsz, nt),
        in_specs=[
            pl.BlockSpec((1, tm, d), lambda b, t: (b, t, 0)),
            pl.BlockSpec((1, 6, d), lambda b, t: (b, 0, 0)),
            const((1, d)),
            const(wqkv.shape), const(wf.shape), const(wft.shape), const(wrw.shape),
            const(fbrow.shape), const(fbcol.shape), const(qg.shape), const(kg.shape),
            const(e.shape), const((tm, tm)), const((tm, tm)), const(mu.shape),
        ],
        out_specs=[
            pl.BlockSpec((1, tm, FOX_WIDTH), lambda b, t: (b, t, 0)),
            pl.BlockSpec((1, tm, FOX_WIDTH), lambda b, t: (b, t, 0)),
            pl.BlockSpec((1, tm, FOX_WIDTH), lambda b, t: (b, t, 0)),
            pl.BlockSpec((1, tm, LANES), lambda b, t: (b, t, 0)),
            pl.BlockSpec((1, tm // tk, 16, tk), lambda b, t: (b, t, 0, 0)),
            pl.BlockSpec((1, tm, RW_COLS), lambda b, t: (b, t, 0)),
        ],
        out_shape=[
            jax.ShapeDtypeStruct((bsz, seq, FOX_WIDTH), BF16),
            jax.ShapeDtypeStruct((bsz, seq, FOX_WIDTH), BF16),
            jax.ShapeDtypeStruct((bsz, seq, FOX_WIDTH), BF16),
            jax.ShapeDtypeStruct((bsz, seq, LANES), F32),
            jax.ShapeDtypeStruct((bsz, seq // tk, 16, tk), F32),
            jax.ShapeDtypeStruct((bsz, seq, RW_COLS), F32),
        ],
        scratch_shapes=[pltpu.VMEM((1, LANES), F32), pltpu.VMEM((16, LANES), F32),
                        pltpu.VMEM((8, RW_COLS), F32)],
        compiler_params=pltpu.CompilerParams(
            dimension_semantics=("arbitrary", "arbitrary"), vmem_limit_bytes=VMEM_LIMIT),
        name="in_proj",
    )(x, mod, g1, wqkv, wf, wft, wrw, fbrow, fbcol, qg, kg, e, trilo, triup, mu)


def _fox_bounded(bound, q_ref, k_ref, v_ref, ccol_ref, crow_ref, o_ref, acc_ref, qm_ref, cb_ref,
                 *, tq, tk):
    i = pl.program_id(1)
    nsub = tq // tk
    lane = lax.broadcasted_iota(jnp.int32, (tq, LANES), 1)
    low = lane < HEAD_DIM
    klow = lax.broadcasted_iota(jnp.int32, (tk, LANES), 1) < HEAD_DIM
    one_lo = jnp.where(klow, 1.0, 0.0).astype(BF16)
    one_hi = jnp.where(klow, 0.0, 1.0).astype(BF16)

    ccol = ccol_ref[0] - bound
    for hp in range(PAIRS):
        q2 = q_ref[0, :, LANES * hp:LANES * (hp + 1)]
        zero = jnp.zeros_like(q2)
        qm_ref[2 * hp] = jnp.where(low, q2, zero)
        qm_ref[2 * hp + 1] = jnp.where(low, zero, q2)
        for hh in range(2):
            h = 2 * hp + hh
            cb_ref[h] = jnp.broadcast_to(ccol[:, h:h + 1], (tq, LANES))
    acc_ref[...] = jnp.zeros_like(acc_ref)

    def step(j, sub):
        r0 = 0 if sub is None else sub * tk
        rows = tq - r0
        ks = pl.multiple_of(j * tk, tk)
        scores = []
        for hp in range(PAIRS):
            k2 = k_ref[0, pl.ds(ks, tk), LANES * hp:LANES * (hp + 1)]
            for hh in range(2):
                scores.append(_mm(qm_ref[2 * hp + hh, r0:tq, :], k2, dims=_NT))
        if sub is not None:
            causal = (lax.broadcasted_iota(jnp.int32, (rows, tk), 1)
                      <= lax.broadcasted_iota(jnp.int32, (rows, tk), 0))
        probs = []
        for h in range(FOX_HEADS):
            cb = cb_ref[h, r0:tq, :]
            bias = jnp.concatenate([cb] * (tk // LANES), axis=1) - crow_ref[0, j, h:h + 1, :]
            s = scores[h] + bias
            if sub is not None:
                s = jnp.where(causal, s, NEG_BIG)
            probs.append(jnp.exp(s).astype(BF16))
        for hp in range(PAIRS):
            v2 = v_ref[0, pl.ds(ks, tk), LANES * hp:LANES * (hp + 1)]
            zero = jnp.zeros_like(v2)
            vaug = jnp.concatenate([
                jnp.concatenate([jnp.where(klow, v2, zero), one_lo], axis=1),
                jnp.concatenate([jnp.where(klow, zero, v2), one_hi], axis=1)], axis=0)
            acc_ref[hp, r0:tq, :] += _mm(
                jnp.concatenate([probs[2 * hp], probs[2 * hp + 1]], axis=1), vaug)

    def body(j, carry):
        step(j, None)
        return carry

    lax.fori_loop(0, i * nsub, body, 0)
    for m in range(nsub):
        step(i * nsub + m, m)
    for hp in range(PAIRS):
        a = acc_ref[hp]
        o = a[:, 0:LANES] / a[:, LANES:2 * LANES]
        o_ref[0, :, LANES * hp:LANES * (hp + 1)] = o.astype(BF16)


def _fox_kernel(flag_ref, bound_ref, q_ref, k_ref, v_ref, ccol_ref, crow_ref, o_ref, acc_ref,
                qm_ref, cb_ref, *, tq, tk):
    @pl.when(flag_ref[0] == 1)
    def _():
        _fox_bounded(bound_ref[0], q_ref, k_ref, v_ref, ccol_ref, crow_ref, o_ref, acc_ref,
                     qm_ref, cb_ref, tq=tq, tk=tk)

    @pl.when(flag_ref[0] == 0)
    def _():
        _fox_running_max(q_ref, k_ref, v_ref, ccol_ref, crow_ref, o_ref, tq=tq, tk=tk)


def _fox_running_max(q_ref, k_ref, v_ref, ccol_ref, crow_ref, o_ref, *, tq, tk):
    i = pl.program_id(1)
    nsub = tq // tk
    lane = lax.broadcasted_iota(jnp.int32, (tq, LANES), 1)
    low = lane < HEAD_DIM
    row = lax.broadcasted_iota(jnp.int32, (tq, tk), 0)
    col = lax.broadcasted_iota(jnp.int32, (tq, tk), 1)
    ccol = ccol_ref[0]

    for hp in range(PAIRS):
        ls = slice(LANES * hp, LANES * (hp + 1))
        q2 = q_ref[0, :, ls]
        zero = jnp.zeros_like(q2)
        qm = (jnp.where(low, q2, zero), jnp.where(low, zero, q2))
        cc = tuple(ccol[:, 2 * hp + hh:2 * hp + hh + 1] for hh in range(2))

        def step(j, carry, sub, ls=ls, qm=qm, cc=cc, hp=hp):
            ks = pl.multiple_of(j * tk, tk)
            k2 = k_ref[0, pl.ds(ks, tk), ls]
            v2 = v_ref[0, pl.ds(ks, tk), ls]
            new = []
            for hh in range(2):
                m, l, acc = carry[hh]
                s = _mm(qm[hh], k2, dims=_NT)
                cr = crow_ref[0, j, 2 * hp + hh:2 * hp + hh + 1, :]
                s = s + (cc[hh] - cr)
                if sub is not None:
                    s = jnp.where(col + sub * tk <= row, s, NEG_BIG)
                m_new = jnp.maximum(m, jnp.max(s, axis=1, keepdims=True))
                alpha = jnp.exp(m - m_new)
                p = jnp.exp(s - m_new)
                l_new = alpha * l + jnp.sum(p, axis=1, keepdims=True)
                acc_new = alpha * acc + _mm(p.astype(BF16), v2)
                new.append((m_new, l_new, acc_new))
            return tuple(new)

        init_one = (jnp.full((tq, 1), NEG_BIG, F32), jnp.zeros((tq, 1), F32),
                    jnp.zeros((tq, LANES), F32))
        carry = lax.fori_loop(0, i * nsub, lambda j, c: step(j, c, None), (init_one, init_one))
        for sub in range(nsub):
            carry = step(i * nsub + sub, carry, sub)
        o0 = carry[0][2] / carry[0][1]
        o1 = carry[1][2] / carry[1][1]
        o_ref[0, :, ls] = jnp.where(low, o0, o1).astype(BF16)


def _fox(flag, bound, q, k, v, ccol, crow, tq, tk):
    bsz, seq, _ = q.shape
    kern = functools.partial(_fox_kernel, tq=tq, tk=tk)
    return pl.pallas_call(
        kern,
        grid=(bsz, seq // tq),
        in_specs=[
            pl.BlockSpec(memory_space=pltpu.SMEM),
            pl.BlockSpec(memory_space=pltpu.SMEM),
            pl.BlockSpec((1, tq, FOX_WIDTH), lambda b, i: (b, i, 0)),
            pl.BlockSpec((1, seq, FOX_WIDTH), lambda b, i: (b, 0, 0)),
            pl.BlockSpec((1, seq, FOX_WIDTH), lambda b, i: (b, 0, 0)),
            pl.BlockSpec((1, tq, LANES), lambda b, i: (b, i, 0)),
            pl.BlockSpec((1, seq // tk, 16, tk), lambda b, i: (b, 0, 0, 0)),
        ],
        out_specs=pl.BlockSpec((1, tq, FOX_WIDTH), lambda b, i: (b, i, 0)),
        out_shape=jax.ShapeDtypeStruct((bsz, seq, FOX_WIDTH), BF16),
        scratch_shapes=[pltpu.VMEM((PAIRS, tq, 2 * LANES), F32),
                        pltpu.VMEM((FOX_HEADS, tq, LANES), BF16),
                        pltpu.VMEM((FOX_HEADS, tq, LANES), F32)],
        compiler_params=pltpu.CompilerParams(
            dimension_semantics=("arbitrary", "arbitrary"), vmem_limit_bytes=VMEM_LIMIT),
        name="fox",
    )(flag, bound, q, k, v, ccol, crow)


def _unit_tri_inverses(mats, bs, passes, tick=lambda: None):
    c, w = mats[0].shape
    n = range(len(mats))
    lane_cache = {}

    def lane_ids(s):
        if s not in lane_cache:
            lane_cache[s] = lax.broadcasted_iota(jnp.int32, (s, w), 1)
        return lane_cache[s]

    def terms(x):
        return _split_bf16(x, passes)

    def dot_terms(a_t, b_t):
        out = None
        for i, ai in enumerate(a_t):
            for j, bj in enumerate(b_t):
                if i + j < max(len(a_t), len(b_t)):
                    t = lax.dot_general(ai, bj, _NN, preferred_element_type=F32)
                    out = t if out is None else out + t
        return out

    def block_rows(x, s, offset):
        lb = jnp.right_shift(lane_ids(s), s.bit_length() - 1)
        zero = jnp.zeros((s, w), BF16)
        keep = [lb == j for j in range(w // s)]
        out = []
        for t in terms(x):
            rows = []
            for j in range(w // s):
                if offset and j % 2 == 0:
                    rows.append(zero)
                else:
                    rows.append(jnp.where(keep[j - offset], t, zero))
            out.append(jnp.concatenate(rows, axis=0))
        return out

    def mm(a, b_terms):
        return dot_terms(terms(a), b_terms)

    s = bs
    sh = s.bit_length() - 1
    in_mat = jnp.bitwise_and(lane_ids(s), c - 1)
    q = []
    for a2 in mats:
        d = a2[0:s, :]
        for r in range(1, c // s):
            d = jnp.where(jnp.right_shift(in_mat, sh) == r, a2[r * s:(r + 1) * s, :], d)
        q.append(d)
    p = [mm(q[i], block_rows(q[i], s, 0)) for i in n]
    tick()
    for _ in range(s.bit_length() - 3):
        both = [mm(jnp.concatenate([q[i], p[i]], axis=0), block_rows(p[i], s, 0)) for i in n]
        q = [q[i] + p[i] + both[i][0:s] for i in n]
        p = [both[i][s:2 * s] for i in n]
        tick()
    q = [q[i] + p[i] + mm(q[i], block_rows(p[i], s, 0)) for i in n]
    tick()

    while s < c:
        sh = s.bit_length() - 1
        lane = lane_ids(s)
        first = jnp.bitwise_and(lane, s) == 0
        pair_id = jnp.right_shift(jnp.bitwise_and(lane, c - 1), sh + 1)
        l21 = []
        for a2 in mats:
            z = jnp.zeros((s, w), F32)
            for m in range(c // (2 * s)):
                rows = a2[(2 * m + 1) * s:(2 * m + 2) * s, :]
                z = jnp.where(pair_id == m, jnp.where(first, rows, 0.0), z)
            l21.append(z)
        x = [l21[i] + mm(l21[i], block_rows(q[i], s, 0)) for i in n]
        tick()
        t21 = [x[i] + mm(q[i], block_rows(x[i], s, 1)) for i in n]
        tick()
        q = [jnp.concatenate([jnp.where(first, q[i], 0.0), jnp.where(first, t21[i], q[i])], axis=0)
             for i in n]
        s *= 2
    return q


def _head_sums(x, e, pa):
    wd = e.shape[0]
    return jnp.concatenate(
        [_mm(x[:, j:j + wd], e, pa, 1) for j in range(0, x.shape[1], wd)], axis=1)


def _rwkv_kernel(rw_ref, w0_ref, w2_ref, a0_ref, a2_ref, g2_ref, kk_ref, ka_ref,
                 rk_ref, lg_ref, lb_ref, e_ref, tri_ref, o_ref, s_ref, *, prec, nb, group):
    cch = CHUNK
    rows = group * cch
    w = RWKV_WIDTH
    c = pl.program_id(1)
    pg, pd, ps = prec
    e = e_ref[...]

    @pl.when(c == 0)
    def _():
        s_ref[...] = jnp.zeros_like(s_ref)

    lane = lax.broadcasted_iota(jnp.int32, (cch, LANES), 1)
    low = lane < HEAD_DIM
    ri = lax.broadcasted_iota(jnp.int32, (cch, cch), 0)
    ci = lax.broadcasted_iota(jnp.int32, (cch, cch), 1)
    strict = ci < ri
    incl = ci <= ri
    blockdiag = (ri < HEAD_DIM) == (ci < HEAD_DIM)
    sl = [slice(LANES * hp, LANES * (hp + 1)) for hp in range(PAIRS)]
    rs = [slice(cch * i, cch * (i + 1)) for i in range(group)]
    combos = [(i, hp) for i in range(group) for hp in range(PAIRS)]
    n = range(len(combos))

    def halves(z):
        zero = jnp.zeros_like(z)
        return jnp.concatenate([jnp.where(low, z, zero), jnp.where(low, zero, z)], axis=0)

    def preparation(gi, out):
        base = gi * group

        def shifted(lo, hi):
            return rw_ref[base:base + group, :, lo:hi].reshape(rows, hi - lo)

        lora = shifted(OFF_WDN, RW_COLS)
        wdn = lora[:, 0:LANES]
        adn = lora[:, OFF_ADN - OFF_WDN:OFF_ADN - OFF_WDN + LANES]
        gdn = lora[:, OFF_GDN - OFF_WDN:OFF_GDN - OFF_WDN + 2 * LANES]
        wlog = _log_sigmoid(w0_ref[...] + _mm(jnp.tanh(wdn), w2_ref[...])) - 0.5
        logd = -jnp.exp(wlog)
        a = _sigmoid(a0_ref[...] + _mm(adn, a2_ref[...]))
        out["g"] = _mm(_sigmoid(gdn), g2_ref[...])
        yield
        cum = _mm(tri_ref[...], logd, 1, 2)
        lasts = [cum[cch * (i + 1) - 1:cch * (i + 1), :] for i in range(group)]
        clast = jnp.concatenate([jnp.broadcast_to(z, (cch, w)) for z in lasts], axis=0)
        out["pc"] = [jnp.exp(z) for z in lasts]
        k_n = shifted(w, 2 * w)
        kk = k_n * kk_ref[...]
        kk = kk * lax.rsqrt(jnp.maximum(_head_sums(kk * kk, e, 2), L2_EPS * L2_EPS))
        k_n = k_n * (1.0 + (a - 1.0) * ka_ref[...])
        yield
        avec = -kk
        bvec = kk * a
        einv = jnp.exp(-cum)
        out["at"] = (avec * jnp.exp(cum - logd)).astype(BF16)
        out["bt"] = (bvec * einv).astype(BF16)
        out["kt"] = (k_n * einv).astype(BF16)
        yield
        etail = jnp.exp(clast - cum)
        out["bh"] = (bvec * etail).astype(BF16)
        out["kh"] = (k_n * etail).astype(BF16)
        yield
        r_n = shifted(0, w)
        out["rt"] = (r_n * jnp.exp(cum)).astype(BF16)
        bonus = _head_sums(r_n * k_n * rk_ref[...], e, 1)
        yield
        v_n = shifted(2 * w, 3 * w)
        out["v"] = v_n.astype(BF16)
        out["bv"] = bonus * v_n
        yield

    def algebra(gi, ops, tick):
        base = gi * group

        def blk(name, j):
            i, hp = combos[j]
            return ops[name][rs[i], sl[hp]]

        aab, aak, arbk = [], [], []
        for j in n:
            rb = jnp.concatenate([blk("bt", j), blk("kt", j)], axis=0)
            blocks = []
            for hh in range(2):
                sel = low if hh == 0 else jnp.logical_not(low)
                zero = jnp.zeros((cch, LANES), BF16)
                la = jnp.concatenate([jnp.where(sel, blk("at", j), zero),
                                      jnp.where(sel, blk("rt", j), zero)], axis=0)
                blocks.append(_mm(la, rb, pg, pg, dims=_NT))
            aab.append(jnp.concatenate(
                [jnp.where(strict, gm[0:cch, 0:cch], 0.0) for gm in blocks], axis=1))
            aak.append(jnp.concatenate(
                [jnp.where(strict, gm[0:cch, cch:2 * cch], 0.0) for gm in blocks], axis=1))
            arbk.append(jnp.concatenate(
                [jnp.where(incl, gm[cch:2 * cch, 0:cch], 0.0) for gm in blocks]
                + [jnp.where(incl, gm[cch:2 * cch, cch:2 * cch], 0.0) for gm in blocks], axis=1))
        tick()
        qm = _unit_tri_inverses(aab, INV_BLOCK, pd, tick)

        sp = [s_ref[base + i, hp] for i, hp in combos]
        vst = [halves(blk("v", j)) for j in n]
        rhs = [_mm(blk("at", j), sp[j], ps, ps, dims=_NT) + _mm(aak[j], vst[j], ps, ps) for j in n]
        tick()
        u = [rhs[j] + _mm(qm[j], halves(rhs[j]), ps, ps) for j in n]
        tick()
        ys = [_mm(blk("rt", j), sp[j], ps, ps, dims=_NT)
              + _mm(arbk[j], jnp.concatenate([halves(u[j]).astype(BF16), vst[j]], axis=0), ps, ps)
              for j in n]
        tick()
        for j in n:
            i, hp = combos[j]
            uv = jnp.concatenate([u[j], blk("v", j).astype(F32)], axis=0)
            bk = jnp.concatenate([blk("bh", j), blk("kh", j)], axis=0)
            upd = _mm(uv.T, bk, ps, ps)
            s_ref[base + i, hp] = (sp[j] * ops["pc"][i][:, sl[hp]]
                                   + jnp.where(blockdiag, upd, 0.0))
        tick()
        return jnp.concatenate(
            [jnp.concatenate(ys[PAIRS * i:PAIRS * (i + 1)], axis=1) for i in range(group)], axis=0)

    def finish(gi, y, ops):
        base = gi * group
        inv_n = 1.0 / HEAD_DIM
        mean = _head_sums(y, e, 1) * inv_n
        d = y - mean
        yield
        var = _head_sums(d * d, e, 1) * inv_n
        yn = d * lax.rsqrt(var + GN_EPS) * lg_ref[...] + lb_ref[...]
        yield
        o_ref[base:base + group] = ((yn + ops["bv"]) * ops["g"]).astype(BF16).reshape(group, cch, w)
        yield

    def drain(gen):
        if gen is not None:
            for _ in gen:
                pass

    ngroups = nb // group
    outs = [dict() for _ in range(ngroups)]
    preps = [preparation(gi, outs[gi]) for gi in range(ngroups)]
    drain(preps[0])
    fin = None
    for gi in range(ngroups):
        nxt = preps[gi + 1] if gi + 1 < ngroups else None
        side = [g for g in (fin, nxt) if g is not None]

        def tick(side=side):
            for g in side:
                try:
                    next(g)
                    return
                except StopIteration:
                    continue

        y = algebra(gi, outs[gi], tick)
        drain(fin)
        drain(nxt)
        fin = finish(gi, y, outs[gi])
    drain(fin)


def _rwkv(rw, w0, w2, a0, a2, g2, k_k, k_a, r_k, lnx_g, lnx_b, e, prec):
    bsz, seq, _ = rw.shape
    cch = CHUNK
    nb = RWKV_SEQS if bsz % RWKV_SEQS == 0 else 1
    group = RWKV_GROUP if nb % RWKV_GROUP == 0 else 1
    row = lax.broadcasted_iota(jnp.int32, (group * cch, group * cch), 0)
    col = lax.broadcasted_iota(jnp.int32, (group * cch, group * cch), 1)
    tri = ((col <= row) & (row // cch == col // cch)).astype(BF16)
    const = lambda shape: pl.BlockSpec(shape, lambda b, t: (0,) * len(shape))
    args = (w0, w2, a0, a2, g2, k_k, k_a, r_k, lnx_g, lnx_b, e, tri)
    return pl.pallas_call(
        functools.partial(_rwkv_kernel, prec=prec, nb=nb, group=group),
        grid=(bsz // nb, seq // cch),
        in_specs=[pl.BlockSpec((nb, cch, RW_COLS), lambda b, t: (b, t, 0))]
        + [const(a.shape) for a in args],
        out_specs=pl.BlockSpec((nb, cch, RWKV_WIDTH), lambda b, t: (b, t, 0)),
        out_shape=jax.ShapeDtypeStruct((bsz, seq, RWKV_WIDTH), BF16),
        scratch_shapes=[pltpu.VMEM((nb, PAIRS, LANES, LANES), F32)],
        compiler_params=pltpu.CompilerParams(
            dimension_semantics=("arbitrary", "arbitrary"), vmem_limit_bytes=VMEM_LIMIT),
        name="rwkv",
    )(rw, *args)


def _out_ffn_kernel(of_ref, or_ref, x_ref, mod_ref, g2_ref, wt_ref, wb_ref, wg_ref, wu_ref, wd_ref,
                    o_ref, *, slabs):
    mod = mod_ref[0]
    gt1, sh2, sc2, gt2 = mod[2:3, :], mod[3:4, :], mod[4:5, :], mod[5:6, :]
    mix = _mm(of_ref[0], wt_ref[...]) + _mm(or_ref[0], wb_ref[...])
    x1 = x_ref[0] + gt1 * mix
    ms = jnp.mean(x1 * x1, axis=-1, keepdims=True)
    y = x1 * lax.rsqrt(ms + RMS_EPS) * g2_ref[...]
    h2 = (y * (1.0 + sc2) + sh2).astype(BF16)
    acc = None
    for lo, hi in slabs:
        gate = _mm(h2, wg_ref[:, lo:hi])
        up = _mm(h2, wu_ref[:, lo:hi])
        act = (gate * _sigmoid(gate) * up).astype(BF16)
        part = _mm(act, wd_ref[lo:hi, :])
        acc = part if acc is None else acc + part
    o_ref[0] = x1 + gt2 * acc


def _out_ffn(o_fox, o_rwkv, x, mod, g2, w_top, w_bot, wg, wu, wd, tm):
    bsz, seq, d = x.shape
    dff = wg.shape[1]
    tiles = dff // MXU_DIM if dff % MXU_DIM == 0 else 1
    cut = (tiles // 2) * (dff // tiles)
    slabs = ((0, cut), (cut, dff)) if cut else ((0, dff),)
    resident = lambda shape: pl.BlockSpec(shape, lambda b, t: (0,) * len(shape),
                                          pipeline_mode=pl.Buffered(1))
    return pl.pallas_call(
        functools.partial(_out_ffn_kernel, slabs=slabs),
        grid=(bsz, seq // tm),
        in_specs=[
            pl.BlockSpec((1, tm, FOX_WIDTH), lambda b, t: (b, t, 0)),
            pl.BlockSpec((1, tm, RWKV_WIDTH), lambda b, t: (b, t, 0)),
            pl.BlockSpec((1, tm, d), lambda b, t: (b, t, 0)),
            pl.BlockSpec((1, 6, d), lambda b, t: (b, 0, 0)),
            resident((1, d)), resident(w_top.shape), resident(w_bot.shape),
            resident(wg.shape), resident(wu.shape), resident(wd.shape),
        ],
        out_specs=pl.BlockSpec((1, tm, d), lambda b, t: (b, t, 0)),
        out_shape=jax.ShapeDtypeStruct((bsz, seq, d), F32),
        compiler_params=pltpu.CompilerParams(
            dimension_semantics=("arbitrary", "arbitrary"), vmem_limit_bytes=VMEM_LIMIT),
        name="out_ffn",
    )(o_fox, o_rwkv, x, mod, g2, w_top, w_bot, wg, wu, wd)


def _pad_cols(w, n):
    return jnp.pad(w, ((0, 0), (0, n - w.shape[1])))


def _pad_rows(w, n):
    return jnp.pad(w, ((0, n - w.shape[0]), (0, 0)))


def _layer(x, mod, norm1_g, norm2_g, w_in, fox_f_bias, fox_q_gain, fox_k_gain, rwkv_mu, rwkv_w0,
           rwkv_w2, rwkv_a0, rwkv_a2, rwkv_g2, rwkv_k_k, rwkv_k_a, rwkv_r_k, rwkv_lnx_g,
           rwkv_lnx_b, w_out, ffn_w_gate, ffn_w_up, ffn_w_down, *, tm, tq, tk, prec):
    bsz, seq, d = x.shape
    w = RWKV_WIDTH
    nfox = 3 * FOX_WIDTH + FOX_HEADS

    wqkv = w_in[:, 0:3 * FOX_WIDTH].astype(BF16)
    wf = _pad_cols(w_in[:, 3 * FOX_WIDTH:nfox], LANES).astype(BF16)
    wft = _pad_rows(w_in[:, 3 * FOX_WIDTH:nfox].T, 16).astype(BF16)
    wr = w_in[:, nfox:]
    o_w, o_a, o_g = 3 * w, 3 * w + DECAY_LORA, 3 * w + DECAY_LORA + A_LORA
    wrw = jnp.concatenate([
        wr[:, 0:3 * w],
        _pad_cols(wr[:, o_w:o_a], LANES),
        _pad_cols(wr[:, o_a:o_g], LANES),
        _pad_cols(wr[:, o_g:], 2 * LANES)], axis=1).astype(BF16)
    mu = rwkv_mu.reshape(1, -1)
    mu_p = jnp.concatenate([
        mu[:, 0:3 * w],
        _pad_cols(mu[:, o_w:o_a], LANES),
        _pad_cols(mu[:, o_a:o_g], LANES),
        _pad_cols(mu[:, o_g:], 2 * LANES)], axis=1)
    fbrow = _pad_cols(fox_f_bias.reshape(1, -1), LANES)
    fbcol = _pad_rows(fox_f_bias.reshape(-1, 1), 16)
    qg = jnp.tile(fox_q_gain, (1, 1)).reshape(1, FOX_WIDTH)
    kg = fox_k_gain.reshape(1, FOX_WIDTH)
    hi = lax.broadcasted_iota(jnp.int32, (MXU_DIM, MXU_DIM), 0) // HEAD_DIM
    hj = lax.broadcasted_iota(jnp.int32, (MXU_DIM, MXU_DIM), 1) // HEAD_DIM
    e = (hi == hj).astype(BF16)

    q, k, v, ccol, crow, rw = _in_proj(
        x, mod, norm1_g.reshape(1, d), wqkv, wf, wft, wrw, fbrow, fbcol, qg, kg, e, mu_p, tm, tk)
    bound = (1.05 * HEAD_DIM ** 0.5) * jnp.max(jnp.abs(fox_q_gain)) * jnp.max(jnp.abs(fox_k_gain))
    flag = (bound <= FOX_BOUND_MAX).astype(jnp.int32)
    o_fox = _fox(flag.reshape(1), bound.astype(F32).reshape(1), q, k, v, ccol, crow, tq, tk)
    o_rwkv = _rwkv(
        rw, rwkv_w0.reshape(1, w), _pad_rows(rwkv_w2, LANES).astype(BF16),
        rwkv_a0.reshape(1, w), _pad_rows(rwkv_a2, LANES).astype(BF16),
        _pad_rows(rwkv_g2, 2 * LANES).astype(BF16), rwkv_k_k.reshape(1, w), rwkv_k_a.reshape(1, w),
        rwkv_r_k.reshape(1, w), rwkv_lnx_g.reshape(1, w), rwkv_lnx_b.reshape(1, w), e, prec)
    wo = w_out.astype(BF16)
    return _out_ffn(o_fox, o_rwkv, x, mod, norm2_g.reshape(1, d), wo[0:FOX_WIDTH], wo[FOX_WIDTH:],
                    ffn_w_gate.astype(BF16), ffn_w_up.astype(BF16), ffn_w_down.astype(BF16),
                    tm)


def kernel(x, c, ada_w, ada_b, norm1_g, norm2_g, w_in, fox_f_bias, fox_q_gain, fox_k_gain, rwkv_mu,
           rwkv_w0, rwkv_w2, rwkv_a0, rwkv_a2, rwkv_g2, rwkv_k_k, rwkv_k_a, rwkv_r_k, rwkv_lnx_g,
           rwkv_lnx_b, w_out, ffn_w_gate, ffn_w_up, ffn_w_down):
    bsz, seq, d = x.shape
    depth = ada_w.shape[0]
    tm = min(ROW_TILE, seq)
    tk = min(KEY_TILE, seq)
    tq = min(QUERY_TILE, seq)
    assert seq % tm == 0 and seq % tq == 0 and tq % tk == 0 and seq % CHUNK == 0
    for l in range(depth):
        mod = _ada(c, ada_w[l], ada_b[l]).reshape(bsz, 6, d)
        x = _layer(x, mod, norm1_g[l], norm2_g[l], w_in[l], fox_f_bias[l], fox_q_gain[l],
                   fox_k_gain[l], rwkv_mu[l], rwkv_w0[l], rwkv_w2[l], rwkv_a0[l], rwkv_a2[l],
                   rwkv_g2[l], rwkv_k_k[l], rwkv_k_a[l], rwkv_r_k[l], rwkv_lnx_g[l],
                   rwkv_lnx_b[l], w_out[l], ffn_w_gate[l], ffn_w_up[l], ffn_w_down[l],
                   tm=tm, tq=tq, tk=tk, prec=RWKV_PASSES)
    return x
```

```python
import functools

import jax
import jax.numpy as jnp
from jax import lax
from jax.experimental import pallas as pl
from jax.experimental.pallas import tpu as pltpu

F32 = jnp.float32
BF16 = jnp.bfloat16

HEAD_DIM = 64
FOX_HEADS = 8
RWKV_HEADS = 8
FOX_WIDTH = FOX_HEADS * HEAD_DIM
RWKV_WIDTH = RWKV_HEADS * HEAD_DIM
DECAY_LORA = 64
A_LORA = 64
GATE_LORA = 160
RMS_EPS = 1e-6
GN_EPS = 64e-5
L2_EPS = 1e-12

LANES = 128
MXU_DIM = 256
PAIRS = FOX_HEADS // 2
RWKV_SEQS = 4
RWKV_GROUP = 2
INV_BLOCK = 64
RW_COLS = 1920
OFF_LORA, OFF_GDN = 1536, 1664
CHUNK = 128
NEG_BIG = -1e30
RWKV_PASSES = (1, 1, 1)
FOX_BOUND_MAX = 30.0
ROW_TILE = 512
QUERY_TILE = 512
KEY_TILE = 256
VMEM_LIMIT = 56 * 1024 * 1024


def _split_bf16(x, n):
    if x.dtype == BF16:
        return [x]
    parts = []
    r = x
    for i in range(n):
        p = r.astype(BF16)
        parts.append(p)
        if i < n - 1:
            r = r - p.astype(F32)
    return parts


_NN = (((1,), (0,)), ((), ()))
_NT = (((1,), (1,)), ((), ()))
_TN = (((0,), (0,)), ((), ()))


def _mm(a, b, pa=1, pb=1, dims=_NN):
    a_parts = _split_bf16(a, pa)
    b_parts = _split_bf16(b, pb)
    order = max(len(a_parts), len(b_parts))
    out = None
    for i, ai in enumerate(a_parts):
        for j, bj in enumerate(b_parts):
            if i + j >= order:
                continue
            t = lax.dot_general(ai, bj, dims, preferred_element_type=F32)
            out = t if out is None else out + t
    return out


def _log_sigmoid(z):
    return jnp.minimum(z, 0.0) - jnp.log(1.0 + jnp.exp(-jnp.abs(z)))


def _sigmoid(z):
    return 1.0 / (1.0 + jnp.exp(-z))


def _ada_kernel(c_ref, w_ref, b_ref, o_ref):
    c = c_ref[...]
    cond = c * _sigmoid(c)
    o_ref[...] = _mm(cond, w_ref[...], 2, 2) + b_ref[...]


def _ada(c, ada_w, ada_b):
    bsz, d = c.shape
    n = ada_w.shape[1]
    tn = 512
    return pl.pallas_call(
        _ada_kernel,
        grid=(n // tn,),
        in_specs=[
            pl.BlockSpec((bsz, d), lambda j: (0, 0)),
            pl.BlockSpec((d, tn), lambda j: (0, j)),
            pl.BlockSpec((1, tn), lambda j: (0, j)),
        ],
        out_specs=pl.BlockSpec((bsz, tn), lambda j: (0, j)),
        out_shape=jax.ShapeDtypeStruct((bsz, n), F32),
        compiler_params=pltpu.CompilerParams(dimension_semantics=("arbitrary",)),
        name="ada",
    )(c, ada_w, ada_b.reshape(1, n))


def _in_proj_kernel(x_ref, mod_ref, g1_ref, wqkv_ref, wf_ref, wrw_ref,
                    fbrow_ref, qg_ref, kg_ref, e_ref, trilo_ref, mu_ref,
                    q_ref, k_ref, v_ref, ccol_ref, crow_ref, rw_ref,
                    carry_row, prev_ref, *, tm, tk):
    t = pl.program_id(1)

    @pl.when(t == 0)
    def _():
        carry_row[...] = jnp.zeros_like(carry_row)
        prev_ref[...] = jnp.zeros_like(prev_ref)

    x = x_ref[0]
    mod = mod_ref[0]
    sh1 = mod[0:1, :]
    sc1 = mod[1:2, :]
    ms = jnp.mean(x * x, axis=-1, keepdims=True)
    y = x * lax.rsqrt(ms + RMS_EPS) * g1_ref[...]
    hb = (y * (1.0 + sc1) + sh1).astype(BF16)

    qkv = _mm(hb, wqkv_ref[...])
    e = e_ref[...]
    q = qkv[:, 0:FOX_WIDTH]
    k = qkv[:, FOX_WIDTH:2 * FOX_WIDTH]
    qms = _head_sums(q * q, e, 1) * (1.0 / HEAD_DIM)
    kms = _head_sums(k * k, e, 1) * (1.0 / HEAD_DIM)
    q_ref[0] = (q * lax.rsqrt(qms + RMS_EPS) * qg_ref[...] * (HEAD_DIM ** -0.5)).astype(BF16)
    k_ref[0] = (k * lax.rsqrt(kms + RMS_EPS) * kg_ref[...]).astype(BF16)
    v_ref[0] = qkv[:, 2 * FOX_WIDTH:3 * FOX_WIDTH].astype(BF16)

    rw = _mm(hb, wrw_ref[...])
    rolled = pltpu.roll(rw, 1, 0)
    first_row = lax.broadcasted_iota(jnp.int32, (8, 1), 0) == 0
    top = jnp.where(first_row, prev_ref[0:1, :], rolled[0:8])
    prev_ref[0:1, :] = rw[tm - 1:tm, :]
    rw_ref[0] = rw + (jnp.concatenate([top, rolled[8:tm]], axis=0) - rw) * mu_ref[...]

    lf = _log_sigmoid(_mm(hb, wf_ref[...]) + fbrow_ref[...])
    blocks = []
    carry = carry_row[...]
    for j in range(tm // LANES):
        cj = _mm(trilo_ref[...], lf[j * LANES:(j + 1) * LANES], 1, 3) + carry
        blocks.append(cj)
        carry = cj[LANES - 1:LANES, :]
    carry_row[...] = carry
    ccol_ref[0] = jnp.concatenate(blocks, axis=0)
    for j, cj in enumerate(blocks):
        jj, off = divmod(j * LANES, tk)
        crow_ref[0, jj, :, off:off + LANES] = cj.T[0:16, :]


def _in_proj(x, mod, g1, wqkv, wf, wrw, fbrow, qg, kg, e, mu, tm, tk):
    bsz, seq, d = x.shape
    nt = seq // tm
    row = lax.broadcasted_iota(jnp.int32, (LANES, LANES), 0)
    col = lax.broadcasted_iota(jnp.int32, (LANES, LANES), 1)
    trilo = (col <= row).astype(BF16)
    const = lambda shape: pl.BlockSpec(shape, lambda b, t: (0,) * len(shape))
    kern = functools.partial(_in_proj_kernel, tm=tm, tk=tk)
    return pl.pallas_call(
        kern,
        grid=(bsz, nt),
        in_specs=[
            pl.BlockSpec((1, tm, d), lambda b, t: (b, t, 0)),
            pl.BlockSpec((1, 6, d), lambda b, t: (b, 0, 0)),
            const((1, d)),
            const(wqkv.shape), const(wf.shape), const(wrw.shape),
            const(fbrow.shape), const(qg.shape), const(kg.shape),
            const(e.shape), const(trilo.shape), const(mu.shape),
        ],
        out_specs=[
            pl.BlockSpec((1, tm, FOX_WIDTH), lambda b, t: (b, t, 0)),
            pl.BlockSpec((1, tm, FOX_WIDTH), lambda b, t: (b, t, 0)),
            pl.BlockSpec((1, tm, FOX_WIDTH), lambda b, t: (b, t, 0)),
            pl.BlockSpec((1, tm, LANES), lambda b, t: (b, t, 0)),
            pl.BlockSpec((1, tm // tk, 16, tk), lambda b, t: (b, t, 0, 0)),
            pl.BlockSpec((1, tm, RW_COLS), lambda b, t: (b, t, 0)),
        ],
        out_shape=[
            jax.ShapeDtypeStruct((bsz, seq, FOX_WIDTH), BF16),
            jax.ShapeDtypeStruct((bsz, seq, FOX_WIDTH), BF16),
            jax.ShapeDtypeStruct((bsz, seq, FOX_WIDTH), BF16),
            jax.ShapeDtypeStruct((bsz, seq, LANES), F32),
            jax.ShapeDtypeStruct((bsz, seq // tk, 16, tk), F32),
            jax.ShapeDtypeStruct((bsz, seq, RW_COLS), F32),
        ],
        scratch_shapes=[pltpu.VMEM((1, LANES), F32), pltpu.VMEM((8, RW_COLS), F32)],
        compiler_params=pltpu.CompilerParams(
            dimension_semantics=("arbitrary", "arbitrary"), vmem_limit_bytes=VMEM_LIMIT),
        name="in_proj",
    )(x, mod, g1, wqkv, wf, wrw, fbrow, qg, kg, e, trilo, mu)


def _fox_bounded(bound, q_ref, k_ref, v_ref, ccol_ref, crow_ref, o_ref, acc_ref, qm_ref, cb_ref,
                 *, tq, tk):
    i = pl.program_id(1)
    nsub = tq // tk
    lane = lax.broadcasted_iota(jnp.int32, (tq, LANES), 1)
    low = lane < HEAD_DIM
    klow = lax.broadcasted_iota(jnp.int32, (tk, LANES), 1) < HEAD_DIM
    one_lo = jnp.where(klow, 1.0, 0.0).astype(BF16)
    one_hi = jnp.where(klow, 0.0, 1.0).astype(BF16)

    ccol = ccol_ref[0] - bound
    for hp in range(PAIRS):
        q2 = q_ref[0, :, LANES * hp:LANES * (hp + 1)]
        zero = jnp.zeros_like(q2)
        qm_ref[2 * hp] = jnp.where(low, q2, zero)
        qm_ref[2 * hp + 1] = jnp.where(low, zero, q2)
        for hh in range(2):
            h = 2 * hp + hh
            cb_ref[h] = jnp.broadcast_to(ccol[:, h:h + 1], (tq, LANES))
    acc_ref[...] = jnp.zeros_like(acc_ref)

    def step(j, sub):
        r0 = 0 if sub is None else sub * tk
        rows = tq - r0
        ks = pl.multiple_of(j * tk, tk)
        scores = []
        for hp in range(PAIRS):
            k2 = k_ref[0, pl.ds(ks, tk), LANES * hp:LANES * (hp + 1)]
            for hh in range(2):
                scores.append(_mm(qm_ref[2 * hp + hh, r0:tq, :], k2, dims=_NT))
        if sub is not None:
            causal = (lax.broadcasted_iota(jnp.int32, (rows, tk), 1)
                      <= lax.broadcasted_iota(jnp.int32, (rows, tk), 0))
        probs = []
        for h in range(FOX_HEADS):
            cb = cb_ref[h, r0:tq, :]
            bias = jnp.concatenate([cb] * (tk // LANES), axis=1) - crow_ref[0, j, h:h + 1, :]
            s = scores[h] + bias
            if sub is not None:
                s = jnp.where(causal, s, NEG_BIG)
            probs.append(jnp.exp(s).astype(BF16))
        for hp in range(PAIRS):
            v2 = v_ref[0, pl.ds(ks, tk), LANES * hp:LANES * (hp + 1)]
            zero = jnp.zeros_like(v2)
            vaug = jnp.concatenate([
                jnp.concatenate([jnp.where(klow, v2, zero), one_lo], axis=1),
                jnp.concatenate([jnp.where(klow, zero, v2), one_hi], axis=1)], axis=0)
            acc_ref[hp, r0:tq, :] += _mm(
                jnp.concatenate([probs[2 * hp], probs[2 * hp + 1]], axis=1), vaug)

    def body(j, carry):
        step(j, None)
        return carry

    lax.fori_loop(0, i * nsub, body, 0)
    for m in range(nsub):
        step(i * nsub + m, m)
    for hp in range(PAIRS):
        a = acc_ref[hp]
        o = a[:, 0:LANES] / a[:, LANES:2 * LANES]
        o_ref[0, :, LANES * hp:LANES * (hp + 1)] = o.astype(BF16)


def _fox_kernel(flag_ref, bound_ref, q_ref, k_ref, v_ref, ccol_ref, crow_ref, o_ref, acc_ref,
                qm_ref, cb_ref, *, tq, tk):
    @pl.when(flag_ref[0] == 1)
    def _():
        _fox_bounded(bound_ref[0], q_ref, k_ref, v_ref, ccol_ref, crow_ref, o_ref, acc_ref,
                     qm_ref, cb_ref, tq=tq, tk=tk)

    @pl.when(flag_ref[0] == 0)
    def _():
        _fox_running_max(q_ref, k_ref, v_ref, ccol_ref, crow_ref, o_ref, tq=tq, tk=tk)


def _fox_running_max(q_ref, k_ref, v_ref, ccol_ref, crow_ref, o_ref, *, tq, tk):
    i = pl.program_id(1)
    nsub = tq // tk
    lane = lax.broadcasted_iota(jnp.int32, (tq, LANES), 1)
    low = lane < HEAD_DIM
    row = lax.broadcasted_iota(jnp.int32, (tq, tk), 0)
    col = lax.broadcasted_iota(jnp.int32, (tq, tk), 1)
    ccol = ccol_ref[0]

    for hp in range(PAIRS):
        ls = slice(LANES * hp, LANES * (hp + 1))
        q2 = q_ref[0, :, ls]
        zero = jnp.zeros_like(q2)
        qm = (jnp.where(low, q2, zero), jnp.where(low, zero, q2))
        cc = tuple(ccol[:, 2 * hp + hh:2 * hp + hh + 1] for hh in range(2))

        def step(j, carry, sub, ls=ls, qm=qm, cc=cc, hp=hp):
            ks = pl.multiple_of(j * tk, tk)
            k2 = k_ref[0, pl.ds(ks, tk), ls]
            v2 = v_ref[0, pl.ds(ks, tk), ls]
            new = []
            for hh in range(2):
                m, l, acc = carry[hh]
                s = _mm(qm[hh], k2, dims=_NT)
                cr = crow_ref[0, j, 2 * hp + hh:2 * hp + hh + 1, :]
                s = s + (cc[hh] - cr)
                if sub is not None:
                    s = jnp.where(col + sub * tk <= row, s, NEG_BIG)
                m_new = jnp.maximum(m, jnp.max(s, axis=1, keepdims=True))
                alpha = jnp.exp(m - m_new)
                p = jnp.exp(s - m_new)
                l_new = alpha * l + jnp.sum(p, axis=1, keepdims=True)
                acc_new = alpha * acc + _mm(p.astype(BF16), v2)
                new.append((m_new, l_new, acc_new))
            return tuple(new)

        init_one = (jnp.full((tq, 1), NEG_BIG, F32), jnp.zeros((tq, 1), F32),
                    jnp.zeros((tq, LANES), F32))
        carry = lax.fori_loop(0, i * nsub, lambda j, c: step(j, c, None), (init_one, init_one))
        for sub in range(nsub):
            carry = step(i * nsub + sub, carry, sub)
        o0 = carry[0][2] / carry[0][1]
        o1 = carry[1][2] / carry[1][1]
        o_ref[0, :, ls] = jnp.where(low, o0, o1).astype(BF16)


def _fox(flag, bound, q, k, v, ccol, crow, tq, tk):
    bsz, seq, _ = q.shape
    kern = functools.partial(_fox_kernel, tq=tq, tk=tk)
    return pl.pallas_call(
        kern,
        grid=(bsz, seq // tq),
        in_specs=[
            pl.BlockSpec(memory_space=pltpu.SMEM),
            pl.BlockSpec(memory_space=pltpu.SMEM),
            pl.BlockSpec((1, tq, FOX_WIDTH), lambda b, i: (b, i, 0)),
            pl.BlockSpec((1, seq, FOX_WIDTH), lambda b, i: (b, 0, 0)),
            pl.BlockSpec((1, seq, FOX_WIDTH), lambda b, i: (b, 0, 0)),
            pl.BlockSpec((1, tq, LANES), lambda b, i: (b, i, 0)),
            pl.BlockSpec((1, seq // tk, 16, tk), lambda b, i: (b, 0, 0, 0)),
        ],
        out_specs=pl.BlockSpec((1, tq, FOX_WIDTH), lambda b, i: (b, i, 0)),
        out_shape=jax.ShapeDtypeStruct((bsz, seq, FOX_WIDTH), BF16),
        scratch_shapes=[pltpu.VMEM((PAIRS, tq, 2 * LANES), F32),
                        pltpu.VMEM((FOX_HEADS, tq, LANES), BF16),
                        pltpu.VMEM((FOX_HEADS, tq, LANES), F32)],
        compiler_params=pltpu.CompilerParams(
            dimension_semantics=("arbitrary", "arbitrary"), vmem_limit_bytes=VMEM_LIMIT),
        name="fox",
    )(flag, bound, q, k, v, ccol, crow)


def _unit_tri_inverses(mats, bs, passes, tick=lambda: None):
    c, w = mats[0].shape
    n = range(len(mats))
    lane_cache = {}

    def lane_ids(s):
        if s not in lane_cache:
            lane_cache[s] = lax.broadcasted_iota(jnp.int32, (s, w), 1)
        return lane_cache[s]

    def terms(x):
        return _split_bf16(x, passes)

    def dot_terms(a_t, b_t):
        out = None
        for i, ai in enumerate(a_t):
            for j, bj in enumerate(b_t):
                if i + j < max(len(a_t), len(b_t)):
                    t = lax.dot_general(ai, bj, _NN, preferred_element_type=F32)
                    out = t if out is None else out + t
        return out

    def block_rows(x, s, offset):
        lb = jnp.right_shift(lane_ids(s), s.bit_length() - 1)
        zero = jnp.zeros((s, w), BF16)
        keep = [lb == j for j in range(w // s)]
        out = []
        for t in terms(x):
            rows = []
            for j in range(w // s):
                if offset and j % 2 == 0:
                    rows.append(zero)
                else:
                    rows.append(jnp.where(keep[j - offset], t, zero))
            out.append(jnp.concatenate(rows, axis=0))
        return out

    def mm(a, b_terms):
        return dot_terms(terms(a), b_terms)

    s = bs
    sh = s.bit_length() - 1
    in_mat = jnp.bitwise_and(lane_ids(s), c - 1)
    q = []
    for a2 in mats:
        d = a2[0:s, :]
        for r in range(1, c // s):
            d = jnp.where(jnp.right_shift(in_mat, sh) == r, a2[r * s:(r + 1) * s, :], d)
        q.append(d)
    p = [mm(q[i], block_rows(q[i], s, 0)) for i in n]
    tick()
    for _ in range(s.bit_length() - 3):
        both = [mm(jnp.concatenate([q[i], p[i]], axis=0), block_rows(p[i], s, 0)) for i in n]
        q = [q[i] + p[i] + both[i][0:s] for i in n]
        p = [both[i][s:2 * s] for i in n]
        tick()
    q = [q[i] + p[i] + mm(q[i], block_rows(p[i], s, 0)) for i in n]
    tick()

    while s < c:
        sh = s.bit_length() - 1
        lane = lane_ids(s)
        first = jnp.bitwise_and(lane, s) == 0
        pair_id = jnp.right_shift(jnp.bitwise_and(lane, c - 1), sh + 1)
        l21 = []
        for a2 in mats:
            z = jnp.zeros((s, w), F32)
            for m in range(c // (2 * s)):
                rows = a2[(2 * m + 1) * s:(2 * m + 2) * s, :]
                z = jnp.where(pair_id == m, jnp.where(first, rows, 0.0), z)
            l21.append(z)
        x = [l21[i] + mm(l21[i], block_rows(q[i], s, 0)) for i in n]
        tick()
        t21 = [x[i] + mm(q[i], block_rows(x[i], s, 1)) for i in n]
        tick()
        q = [jnp.concatenate([jnp.where(first, q[i], 0.0), jnp.where(first, t21[i], q[i])], axis=0)
             for i in n]
        s *= 2
    return q


def _head_sums(x, e, pa):
    wd = e.shape[0]
    return jnp.concatenate(
        [_mm(x[:, j:j + wd], e, pa, 1) for j in range(0, x.shape[1], wd)], axis=1)


def _rwkv_kernel(rw_ref, w0_ref, w2_ref, a0_ref, a2_ref, g2_ref, kk_ref, ka_ref,
                 rk_ref, lg_ref, lb_ref, e_ref, tri_ref, o_ref, s_ref, *, prec, nb, group):
    cch = CHUNK
    rows = group * cch
    w = RWKV_WIDTH
    c = pl.program_id(1)
    pg, pd, ps = prec
    e = e_ref[...]

    @pl.when(c == 0)
    def _():
        s_ref[...] = jnp.zeros_like(s_ref)

    lane = lax.broadcasted_iota(jnp.int32, (cch, LANES), 1)
    low = lane < HEAD_DIM
    ri = lax.broadcasted_iota(jnp.int32, (cch, cch), 0)
    ci = lax.broadcasted_iota(jnp.int32, (cch, cch), 1)
    strict = ci < ri
    incl = ci <= ri
    blockdiag = (ri < HEAD_DIM) == (ci < HEAD_DIM)
    sl = [slice(LANES * hp, LANES * (hp + 1)) for hp in range(PAIRS)]
    rs = [slice(cch * i, cch * (i + 1)) for i in range(group)]
    combos = [(i, hp) for i in range(group) for hp in range(PAIRS)]
    n = range(len(combos))

    def halves(z):
        zero = jnp.zeros_like(z)
        return jnp.concatenate([jnp.where(low, z, zero), jnp.where(low, zero, z)], axis=0)

    def preparation(gi, out):
        base = gi * group

        def shifted(lo, hi):
            return rw_ref[base:base + group, :, lo:hi].reshape(rows, hi - lo)

        wa = shifted(OFF_LORA, OFF_GDN)
        gdn = shifted(OFF_GDN, RW_COLS)
        wlog = _log_sigmoid(w0_ref[...] + _mm(jnp.tanh(wa), w2_ref[...])) - 0.5
        logd = -jnp.exp(wlog)
        a = _sigmoid(a0_ref[...] + _mm(wa, a2_ref[...]))
        out["g"] = _mm(_sigmoid(gdn), g2_ref[...])
        yield
        cum = _mm(tri_ref[...], logd, 1, 2)
        lasts = [cum[cch * (i + 1) - 1:cch * (i + 1), :] for i in range(group)]
        clast = jnp.concatenate([jnp.broadcast_to(z, (cch, w)) for z in lasts], axis=0)
        out["pc"] = [jnp.exp(z) for z in lasts]
        k_n = shifted(w, 2 * w)
        kk = k_n * kk_ref[...]
        kk = kk * lax.rsqrt(jnp.maximum(_head_sums(kk * kk, e, 2), L2_EPS * L2_EPS))
        k_n = k_n * (1.0 + (a - 1.0) * ka_ref[...])
        yield
        avec = -kk
        bvec = kk * a
        einv = jnp.exp(-cum)
        out["at"] = (avec * jnp.exp(cum - logd)).astype(BF16)
        out["bt"] = (bvec * einv).astype(BF16)
        out["kt"] = (k_n * einv).astype(BF16)
        yield
        etail = jnp.exp(clast - cum)
        out["bh"] = (bvec * etail).astype(BF16)
        out["kh"] = (k_n * etail).astype(BF16)
        yield
        r_n = shifted(0, w)
        out["rt"] = (r_n * jnp.exp(cum)).astype(BF16)
        bonus = _head_sums(r_n * k_n * rk_ref[...], e, 1)
        yield
        v_n = shifted(2 * w, 3 * w)
        out["v"] = v_n.astype(BF16)
        out["bv"] = bonus * v_n
        yield

    def algebra(gi, ops, tick):
        base = gi * group

        def blk(name, j):
            i, hp = combos[j]
            return ops[name][rs[i], sl[hp]]

        aab, aak, arbk = [], [], []
        for j in n:
            rb = jnp.concatenate([blk("bt", j), blk("kt", j)], axis=0)
            blocks = []
            for hh in range(2):
                sel = low if hh == 0 else jnp.logical_not(low)
                zero = jnp.zeros((cch, LANES), BF16)
                la = jnp.concatenate([jnp.where(sel, blk("at", j), zero),
                                      jnp.where(sel, blk("rt", j), zero)], axis=0)
                blocks.append(_mm(la, rb, pg, pg, dims=_NT))
            aab.append(jnp.concatenate(
                [jnp.where(strict, gm[0:cch, 0:cch], 0.0) for gm in blocks], axis=1))
            aak.append(jnp.concatenate(
                [jnp.where(strict, gm[0:cch, cch:2 * cch], 0.0) for gm in blocks], axis=1))
            arbk.append(jnp.concatenate(
                [jnp.where(incl, gm[cch:2 * cch, 0:cch], 0.0) for gm in blocks]
                + [jnp.where(incl, gm[cch:2 * cch, cch:2 * cch], 0.0) for gm in blocks], axis=1))
        tick()
        qm = _unit_tri_inverses(aab, INV_BLOCK, pd, tick)

        sp = [s_ref[base + i, hp] for i, hp in combos]
        vst = [halves(blk("v", j)) for j in n]
        rhs = [_mm(blk("at", j), sp[j], ps, ps, dims=_NT) + _mm(aak[j], vst[j], ps, ps) for j in n]
        tick()
        u = [rhs[j] + _mm(qm[j], halves(rhs[j]), ps, ps) for j in n]
        tick()
        ys = [_mm(blk("rt", j), sp[j], ps, ps, dims=_NT)
              + _mm(arbk[j], jnp.concatenate([halves(u[j]).astype(BF16), vst[j]], axis=0), ps, ps)
              for j in n]
        tick()
        for j in n:
            i, hp = combos[j]
            uv = jnp.concatenate([u[j], blk("v", j).astype(F32)], axis=0)
            bk = jnp.concatenate([blk("bh", j), blk("kh", j)], axis=0)
            upd = _mm(uv, bk, ps, ps, dims=_TN)
            s_ref[base + i, hp] = (sp[j] * ops["pc"][i][:, sl[hp]]
                                   + jnp.where(blockdiag, upd, 0.0))
        tick()
        return jnp.concatenate(
            [jnp.concatenate(ys[PAIRS * i:PAIRS * (i + 1)], axis=1) for i in range(group)], axis=0)

    def finish(gi, y, ops):
        base = gi * group
        inv_n = 1.0 / HEAD_DIM
        mean = _head_sums(y, e, 1) * inv_n
        d = y - mean
        yield
        var = _head_sums(d * d, e, 1) * inv_n
        yn = d * lax.rsqrt(var + GN_EPS) * lg_ref[...] + lb_ref[...]
        yield
        o_ref[base:base + group] = ((yn + ops["bv"]) * ops["g"]).astype(BF16).reshape(group, cch, w)
        yield

    def drain(gen):
        if gen is not None:
            for _ in gen:
                pass

    ngroups = nb // group
    outs = [dict() for _ in range(ngroups)]
    preps = [preparation(gi, outs[gi]) for gi in range(ngroups)]
    drain(preps[0])
    fin = None
    for gi in range(ngroups):
        nxt = preps[gi + 1] if gi + 1 < ngroups else None
        side = [g for g in (fin, nxt) if g is not None]

        def tick(side=side):
            for g in side:
                try:
                    next(g)
                    return
                except StopIteration:
                    continue

        y = algebra(gi, outs[gi], tick)
        drain(fin)
        drain(nxt)
        fin = finish(gi, y, outs[gi])
    drain(fin)


def _rwkv(rw, w0, w2, a0, a2, g2, k_k, k_a, r_k, lnx_g, lnx_b, e, prec):
    bsz, seq, _ = rw.shape
    cch = CHUNK
    nb = RWKV_SEQS if bsz % RWKV_SEQS == 0 else 1
    group = RWKV_GROUP if nb % RWKV_GROUP == 0 else 1
    row = lax.broadcasted_iota(jnp.int32, (group * cch, group * cch), 0)
    col = lax.broadcasted_iota(jnp.int32, (group * cch, group * cch), 1)
    tri = ((col <= row) & (row // cch == col // cch)).astype(BF16)
    const = lambda shape: pl.BlockSpec(shape, lambda b, t: (0,) * len(shape))
    args = (w0, w2, a0, a2, g2, k_k, k_a, r_k, lnx_g, lnx_b, e, tri)
    return pl.pallas_call(
        functools.partial(_rwkv_kernel, prec=prec, nb=nb, group=group),
        grid=(bsz // nb, seq // cch),
        in_specs=[pl.BlockSpec((nb, cch, RW_COLS), lambda b, t: (b, t, 0))]
        + [const(a.shape) for a in args],
        out_specs=pl.BlockSpec((nb, cch, RWKV_WIDTH), lambda b, t: (b, t, 0)),
        out_shape=jax.ShapeDtypeStruct((bsz, seq, RWKV_WIDTH), BF16),
        scratch_shapes=[pltpu.VMEM((nb, PAIRS, LANES, LANES), F32)],
        compiler_params=pltpu.CompilerParams(
            dimension_semantics=("arbitrary", "arbitrary"), vmem_limit_bytes=VMEM_LIMIT),
        name="rwkv",
    )(rw, *args)


def _out_ffn_kernel(of_ref, or_ref, x_ref, mod_ref, g2_ref, wt_ref, wb_ref, wg_ref, wu_ref, wd_ref,
                    o_ref, *, slabs):
    mod = mod_ref[0]
    gt1, sh2, sc2, gt2 = mod[2:3, :], mod[3:4, :], mod[4:5, :], mod[5:6, :]
    mix = _mm(of_ref[0], wt_ref[...]) + _mm(or_ref[0], wb_ref[...])
    x1 = x_ref[0] + gt1 * mix
    ms = jnp.mean(x1 * x1, axis=-1, keepdims=True)
    y = x1 * lax.rsqrt(ms + RMS_EPS) * g2_ref[...]
    h2 = (y * (1.0 + sc2) + sh2).astype(BF16)
    acc = None
    for lo, hi in slabs:
        gate = _mm(h2, wg_ref[:, lo:hi])
        up = _mm(h2, wu_ref[:, lo:hi])
        act = (gate * _sigmoid(gate) * up).astype(BF16)
        part = _mm(act, wd_ref[lo:hi, :])
        acc = part if acc is None else acc + part
    o_ref[0] = x1 + gt2 * acc


def _out_ffn(o_fox, o_rwkv, x, mod, g2, w_top, w_bot, wg, wu, wd, tm):
    bsz, seq, d = x.shape
    dff = wg.shape[1]
    tiles = dff // MXU_DIM if dff % MXU_DIM == 0 else 1
    cut = (tiles // 2) * (dff // tiles)
    slabs = ((0, cut), (cut, dff)) if cut else ((0, dff),)
    resident = lambda shape: pl.BlockSpec(shape, lambda b, t: (0,) * len(shape),
                                          pipeline_mode=pl.Buffered(1))
    return pl.pallas_call(
        functools.partial(_out_ffn_kernel, slabs=slabs),
        grid=(bsz, seq // tm),
        in_specs=[
            pl.BlockSpec((1, tm, FOX_WIDTH), lambda b, t: (b, t, 0)),
            pl.BlockSpec((1, tm, RWKV_WIDTH), lambda b, t: (b, t, 0)),
            pl.BlockSpec((1, tm, d), lambda b, t: (b, t, 0)),
            pl.BlockSpec((1, 6, d), lambda b, t: (b, 0, 0)),
            resident((1, d)), resident(w_top.shape), resident(w_bot.shape),
            resident(wg.shape), resident(wu.shape), resident(wd.shape),
        ],
        out_specs=pl.BlockSpec((1, tm, d), lambda b, t: (b, t, 0)),
        out_shape=jax.ShapeDtypeStruct((bsz, seq, d), F32),
        compiler_params=pltpu.CompilerParams(
            dimension_semantics=("arbitrary", "arbitrary"), vmem_limit_bytes=VMEM_LIMIT),
        name="out_ffn",
    )(o_fox, o_rwkv, x, mod, g2, w_top, w_bot, wg, wu, wd)


def _pad_cols(w, n):
    return jnp.pad(w, ((0, 0), (0, n - w.shape[1])))


def _pad_rows(w, n):
    return jnp.pad(w, ((0, n - w.shape[0]), (0, 0)))


def _layer(x, mod, norm1_g, norm2_g, w_in, fox_f_bias, fox_q_gain, fox_k_gain, rwkv_mu, rwkv_w0,
           rwkv_w2, rwkv_a0, rwkv_a2, rwkv_g2, rwkv_k_k, rwkv_k_a, rwkv_r_k, rwkv_lnx_g,
           rwkv_lnx_b, w_out, ffn_w_gate, ffn_w_up, ffn_w_down, *, tm, tq, tk, prec):
    bsz, seq, d = x.shape
    w = RWKV_WIDTH
    nfox = 3 * FOX_WIDTH + FOX_HEADS

    wqkv = w_in[:, 0:3 * FOX_WIDTH].astype(BF16)
    wf = _pad_cols(w_in[:, 3 * FOX_WIDTH:nfox], LANES).astype(BF16)
    wr = w_in[:, nfox:]
    assert 3 * w == OFF_LORA and DECAY_LORA + A_LORA == OFF_GDN - OFF_LORA
    wrw = _pad_cols(wr, RW_COLS).astype(BF16)
    mu_p = _pad_cols(rwkv_mu.reshape(1, -1), RW_COLS)
    fbrow = _pad_cols(fox_f_bias.reshape(1, -1), LANES)
    qg = jnp.tile(fox_q_gain, (1, 1)).reshape(1, FOX_WIDTH)
    kg = fox_k_gain.reshape(1, FOX_WIDTH)
    hi = lax.broadcasted_iota(jnp.int32, (MXU_DIM, MXU_DIM), 0) // HEAD_DIM
    hj = lax.broadcasted_iota(jnp.int32, (MXU_DIM, MXU_DIM), 1) // HEAD_DIM
    e = (hi == hj).astype(BF16)

    q, k, v, ccol, crow, rw = _in_proj(
        x, mod, norm1_g.reshape(1, d), wqkv, wf, wrw, fbrow, qg, kg, e, mu_p, tm, tk)
    bound = (1.05 * HEAD_DIM ** 0.5) * jnp.max(jnp.abs(fox_q_gain)) * jnp.max(jnp.abs(fox_k_gain))
    flag = (bound <= FOX_BOUND_MAX).astype(jnp.int32)
    o_fox = _fox(flag.reshape(1), bound.astype(F32).reshape(1), q, k, v, ccol, crow, tq, tk)
    o_rwkv = _rwkv(
        rw, rwkv_w0.reshape(1, w), _pad_rows(rwkv_w2, LANES).astype(BF16),
        rwkv_a0.reshape(1, w), jnp.pad(rwkv_a2, ((DECAY_LORA, 0), (0, 0))).astype(BF16),
        _pad_rows(rwkv_g2, RW_COLS - OFF_GDN).astype(BF16), rwkv_k_k.reshape(1, w), rwkv_k_a.reshape(1, w),
        rwkv_r_k.reshape(1, w), rwkv_lnx_g.reshape(1, w), rwkv_lnx_b.reshape(1, w), e, prec)
    wo = w_out.astype(BF16)
    return _out_ffn(o_fox, o_rwkv, x, mod, norm2_g.reshape(1, d), wo[0:FOX_WIDTH], wo[FOX_WIDTH:],
                    ffn_w_gate.astype(BF16), ffn_w_up.astype(BF16), ffn_w_down.astype(BF16),
                    tm)


def kernel(x, c, ada_w, ada_b, norm1_g, norm2_g, w_in, fox_f_bias, fox_q_gain, fox_k_gain, rwkv_mu,
           rwkv_w0, rwkv_w2, rwkv_a0, rwkv_a2, rwkv_g2, rwkv_k_k, rwkv_k_a, rwkv_r_k, rwkv_lnx_g,
           rwkv_lnx_b, w_out, ffn_w_gate, ffn_w_up, ffn_w_down):
    bsz, seq, d = x.shape
    depth = ada_w.shape[0]
    tm = min(ROW_TILE, seq)
    tk = min(KEY_TILE, seq)
    tq = min(QUERY_TILE, seq)
    assert seq % tm == 0 and seq % tq == 0 and tq % tk == 0 and seq % CHUNK == 0
    for l in range(depth):
        mod = _ada(c, ada_w[l], ada_b[l]).reshape(bsz, 6, d)
        x = _layer(x, mod, norm1_g[l], norm2_g[l], w_in[l], fox_f_bias[l], fox_q_gain[l],
                   fox_k_gain[l], rwkv_mu[l], rwkv_w0[l], rwkv_w2[l], rwkv_a0[l], rwkv_a2[l],
                   rwkv_g2[l], rwkv_k_k[l], rwkv_k_a[l], rwkv_r_k[l], rwkv_lnx_g[l],
                   rwkv_lnx_b[l], w_out[l], ffn_w_gate[l], ffn_w_up[l], ffn_w_down[l],
                   tm=tm, tq=tq, tk=tk, prec=RWKV_PASSES)
    return x
```

```python
import functools

import jax
import jax.numpy as jnp
from jax import lax
from jax.experimental import pallas as pl
from jax.experimental.pallas import tpu as pltpu

F32 = jnp.float32
BF16 = jnp.bfloat16

HEAD_DIM = 64
FOX_HEADS = 8
RWKV_HEADS = 8
FOX_WIDTH = FOX_HEADS * HEAD_DIM
RWKV_WIDTH = RWKV_HEADS * HEAD_DIM
DECAY_LORA = 64
A_LORA = 64
GATE_LORA = 160
RMS_EPS = 1e-6
GN_EPS = 64e-5
L2_EPS = 1e-12

LANES = 128
MXU_DIM = 256
PAIRS = FOX_HEADS // 2
RWKV_SEQS = 4
RWKV_GROUP = 4
INV_BLOCK = 64
RW_COLS = 1920
OFF_LORA, OFF_GDN = 1536, 1664
CHUNK = 128
NEG_BIG = -1e30
RWKV_PASSES = (1, 1, 1)
FOX_BOUND_MAX = 30.0
ROW_TILE = 512
QUERY_TILE = 512
KEY_TILE = 256
VMEM_LIMIT = 56 * 1024 * 1024


def _split_bf16(x, n):
    if x.dtype == BF16:
        return [x]
    parts = []
    r = x
    for i in range(n):
        p = r.astype(BF16)
        parts.append(p)
        if i < n - 1:
            r = r - p.astype(F32)
    return parts


_NN = (((1,), (0,)), ((), ()))
_NT = (((1,), (1,)), ((), ()))
_TN = (((0,), (0,)), ((), ()))


def _mm(a, b, pa=1, pb=1, dims=_NN):
    a_parts = _split_bf16(a, pa)
    b_parts = _split_bf16(b, pb)
    order = max(len(a_parts), len(b_parts))
    out = None
    for i, ai in enumerate(a_parts):
        for j, bj in enumerate(b_parts):
            if i + j >= order:
                continue
            t = lax.dot_general(ai, bj, dims, preferred_element_type=F32)
            out = t if out is None else out + t
    return out


def _log_sigmoid(z):
    return jnp.minimum(z, 0.0) - jnp.log(1.0 + jnp.exp(-jnp.abs(z)))


def _sigmoid(z):
    return 1.0 / (1.0 + jnp.exp(-z))


def _ada_kernel(c_ref, w_ref, b_ref, o_ref):
    c = c_ref[...]
    cond = c * _sigmoid(c)
    o_ref[...] = _mm(cond, w_ref[...], 2, 2) + b_ref[...]


def _ada(c, ada_w, ada_b):
    bsz, d = c.shape
    n = ada_w.shape[1]
    tn = 512
    return pl.pallas_call(
        _ada_kernel,
        grid=(n // tn,),
        in_specs=[
            pl.BlockSpec((bsz, d), lambda j: (0, 0)),
            pl.BlockSpec((d, tn), lambda j: (0, j)),
            pl.BlockSpec((1, tn), lambda j: (0, j)),
        ],
        out_specs=pl.BlockSpec((bsz, tn), lambda j: (0, j)),
        out_shape=jax.ShapeDtypeStruct((bsz, n), F32),
        compiler_params=pltpu.CompilerParams(dimension_semantics=("arbitrary",)),
        name="ada",
    )(c, ada_w, ada_b.reshape(1, n))


def _in_proj_kernel(x_ref, mod_ref, g1_ref, wqkv_ref, wf_ref, wrw_ref,
                    fbrow_ref, qg_ref, kg_ref, e_ref, trilo_ref, mu_ref,
                    q_ref, k_ref, v_ref, ccol_ref, crow_ref, rw_ref,
                    carry_row, prev_ref, *, tm, tk):
    t = pl.program_id(1)

    @pl.when(t == 0)
    def _():
        carry_row[...] = jnp.zeros_like(carry_row)
        prev_ref[...] = jnp.zeros_like(prev_ref)

    x = x_ref[0]
    mod = mod_ref[0]
    sh1 = mod[0:1, :]
    sc1 = mod[1:2, :]
    ms = jnp.mean(x * x, axis=-1, keepdims=True)
    y = x * lax.rsqrt(ms + RMS_EPS) * g1_ref[...]
    hb = (y * (1.0 + sc1) + sh1).astype(BF16)

    qkv = _mm(hb, wqkv_ref[...])
    e = e_ref[...]
    q = qkv[:, 0:FOX_WIDTH]
    k = qkv[:, FOX_WIDTH:2 * FOX_WIDTH]
    qms = _head_sums(q * q, e, 1) * (1.0 / HEAD_DIM)
    kms = _head_sums(k * k, e, 1) * (1.0 / HEAD_DIM)
    q_ref[0] = (q * lax.rsqrt(qms + RMS_EPS) * qg_ref[...] * (HEAD_DIM ** -0.5)).astype(BF16)
    k_ref[0] = (k * lax.rsqrt(kms + RMS_EPS) * kg_ref[...]).astype(BF16)
    v_ref[0] = qkv[:, 2 * FOX_WIDTH:3 * FOX_WIDTH].astype(BF16)

    rw = _mm(hb, wrw_ref[...])
    rolled = pltpu.roll(rw, 1, 0)
    first_row = lax.broadcasted_iota(jnp.int32, (8, 1), 0) == 0
    top = jnp.where(first_row, prev_ref[0:1, :], rolled[0:8])
    prev_ref[0:1, :] = rw[tm - 1:tm, :]
    rw_ref[0] = rw + (jnp.concatenate([top, rolled[8:tm]], axis=0) - rw) * mu_ref[...]

    lf = _log_sigmoid(_mm(hb, wf_ref[...]) + fbrow_ref[...])
    blocks = []
    carry = carry_row[...]
    for j in range(tm // LANES):
        cj = _mm(trilo_ref[...], lf[j * LANES:(j + 1) * LANES], 1, 3) + carry
        blocks.append(cj)
        carry = cj[LANES - 1:LANES, :]
    carry_row[...] = carry
    ccol_ref[0] = jnp.concatenate(blocks, axis=0)
    for j, cj in enumerate(blocks):
        jj, off = divmod(j * LANES, tk)
        crow_ref[0, jj, :, off:off + LANES] = cj.T[0:16, :]


def _in_proj(x, mod, g1, wqkv, wf, wrw, fbrow, qg, kg, e, mu, tm, tk):
    bsz, seq, d = x.shape
    nt = seq // tm
    row = lax.broadcasted_iota(jnp.int32, (LANES, LANES), 0)
    col = lax.broadcasted_iota(jnp.int32, (LANES, LANES), 1)
    trilo = (col <= row).astype(BF16)
    const = lambda shape: pl.BlockSpec(shape, lambda b, t: (0,) * len(shape))
    kern = functools.partial(_in_proj_kernel, tm=tm, tk=tk)
    return pl.pallas_call(
        kern,
        grid=(bsz, nt),
        in_specs=[
            pl.BlockSpec((1, tm, d), lambda b, t: (b, t, 0)),
            pl.BlockSpec((1, 6, d), lambda b, t: (b, 0, 0)),
            const((1, d)),
            const(wqkv.shape), const(wf.shape), const(wrw.shape),
            const(fbrow.shape), const(qg.shape), const(kg.shape),
            const(e.shape), const(trilo.shape), const(mu.shape),
        ],
        out_specs=[
            pl.BlockSpec((1, tm, FOX_WIDTH), lambda b, t: (b, t, 0)),
            pl.BlockSpec((1, tm, FOX_WIDTH), lambda b, t: (b, t, 0)),
            pl.BlockSpec((1, tm, FOX_WIDTH), lambda b, t: (b, t, 0)),
            pl.BlockSpec((1, tm, LANES), lambda b, t: (b, t, 0)),
            pl.BlockSpec((1, tm // tk, 16, tk), lambda b, t: (b, t, 0, 0)),
            pl.BlockSpec((1, tm, RW_COLS), lambda b, t: (b, t, 0)),
        ],
        out_shape=[
            jax.ShapeDtypeStruct((bsz, seq, FOX_WIDTH), BF16),
            jax.ShapeDtypeStruct((bsz, seq, FOX_WIDTH), BF16),
            jax.ShapeDtypeStruct((bsz, seq, FOX_WIDTH), BF16),
            jax.ShapeDtypeStruct((bsz, seq, LANES), F32),
            jax.ShapeDtypeStruct((bsz, seq // tk, 16, tk), F32),
            jax.ShapeDtypeStruct((bsz, seq, RW_COLS), F32),
        ],
        scratch_shapes=[pltpu.VMEM((1, LANES), F32), pltpu.VMEM((8, RW_COLS), F32)],
        compiler_params=pltpu.CompilerParams(
            dimension_semantics=("arbitrary", "arbitrary"), vmem_limit_bytes=VMEM_LIMIT),
        name="in_proj",
    )(x, mod, g1, wqkv, wf, wrw, fbrow, qg, kg, e, trilo, mu)


def _fox_bounded(bound, q_ref, k_ref, v_ref, ccol_ref, crow_ref, o_ref, acc_ref, qm_ref, cb_ref,
                 *, tq, tk):
    i = pl.program_id(1)
    nsub = tq // tk
    lane = lax.broadcasted_iota(jnp.int32, (tq, LANES), 1)
    low = lane < HEAD_DIM
    klow = lax.broadcasted_iota(jnp.int32, (tk, LANES), 1) < HEAD_DIM
    one_lo = jnp.where(klow, 1.0, 0.0).astype(BF16)
    one_hi = jnp.where(klow, 0.0, 1.0).astype(BF16)

    ccol = ccol_ref[0] - bound
    for hp in range(PAIRS):
        q2 = q_ref[0, :, LANES * hp:LANES * (hp + 1)]
        zero = jnp.zeros_like(q2)
        qm_ref[2 * hp] = jnp.where(low, q2, zero)
        qm_ref[2 * hp + 1] = jnp.where(low, zero, q2)
        for hh in range(2):
            h = 2 * hp + hh
            cb_ref[h] = jnp.broadcast_to(ccol[:, h:h + 1], (tq, LANES))
    acc_ref[...] = jnp.zeros_like(acc_ref)

    def step(j, sub):
        r0 = 0 if sub is None else sub * tk
        rows = tq - r0
        ks = pl.multiple_of(j * tk, tk)
        scores = []
        for hp in range(PAIRS):
            k2 = k_ref[0, pl.ds(ks, tk), LANES * hp:LANES * (hp + 1)]
            for hh in range(2):
                scores.append(_mm(qm_ref[2 * hp + hh, r0:tq, :], k2, dims=_NT))
        if sub is not None:
            causal = (lax.broadcasted_iota(jnp.int32, (rows, tk), 1)
                      <= lax.broadcasted_iota(jnp.int32, (rows, tk), 0))
        probs = []
        for h in range(FOX_HEADS):
            cb = cb_ref[h, r0:tq, :]
            bias = jnp.concatenate([cb] * (tk // LANES), axis=1) - crow_ref[0, j, h:h + 1, :]
            s = scores[h] + bias
            if sub is not None:
                s = jnp.where(causal, s, NEG_BIG)
            probs.append(jnp.exp(s).astype(BF16))
        for hp in range(PAIRS):
            v2 = v_ref[0, pl.ds(ks, tk), LANES * hp:LANES * (hp + 1)]
            zero = jnp.zeros_like(v2)
            vaug = jnp.concatenate([
                jnp.concatenate([jnp.where(klow, v2, zero), one_lo], axis=1),
                jnp.concatenate([jnp.where(klow, zero, v2), one_hi], axis=1)], axis=0)
            acc_ref[hp, r0:tq, :] += _mm(
                jnp.concatenate([probs[2 * hp], probs[2 * hp + 1]], axis=1), vaug)

    def body(jo, carry):
        for m in range(nsub):
            step(jo * nsub + m, None)
        return carry

    lax.fori_loop(0, i, body, 0)
    for m in range(nsub):
        step(i * nsub + m, m)
    for hp in range(PAIRS):
        a = acc_ref[hp]
        o = a[:, 0:LANES] / a[:, LANES:2 * LANES]
        o_ref[0, :, LANES * hp:LANES * (hp + 1)] = o.astype(BF16)


def _fox_kernel(flag_ref, bound_ref, q_ref, k_ref, v_ref, ccol_ref, crow_ref, o_ref, acc_ref,
                qm_ref, cb_ref, *, tq, tk):
    @pl.when(flag_ref[0] == 1)
    def _():
        _fox_bounded(bound_ref[0], q_ref, k_ref, v_ref, ccol_ref, crow_ref, o_ref, acc_ref,
                     qm_ref, cb_ref, tq=tq, tk=tk)

    @pl.when(flag_ref[0] == 0)
    def _():
        _fox_running_max(q_ref, k_ref, v_ref, ccol_ref, crow_ref, o_ref, tq=tq, tk=tk)


def _fox_running_max(q_ref, k_ref, v_ref, ccol_ref, crow_ref, o_ref, *, tq, tk):
    i = pl.program_id(1)
    nsub = tq // tk
    lane = lax.broadcasted_iota(jnp.int32, (tq, LANES), 1)
    low = lane < HEAD_DIM
    row = lax.broadcasted_iota(jnp.int32, (tq, tk), 0)
    col = lax.broadcasted_iota(jnp.int32, (tq, tk), 1)
    ccol = ccol_ref[0]

    for hp in range(PAIRS):
        ls = slice(LANES * hp, LANES * (hp + 1))
        q2 = q_ref[0, :, ls]
        zero = jnp.zeros_like(q2)
        qm = (jnp.where(low, q2, zero), jnp.where(low, zero, q2))
        cc = tuple(ccol[:, 2 * hp + hh:2 * hp + hh + 1] for hh in range(2))

        def step(j, carry, sub, ls=ls, qm=qm, cc=cc, hp=hp):
            ks = pl.multiple_of(j * tk, tk)
            k2 = k_ref[0, pl.ds(ks, tk), ls]
            v2 = v_ref[0, pl.ds(ks, tk), ls]
            new = []
            for hh in range(2):
                m, l, acc = carry[hh]
                s = _mm(qm[hh], k2, dims=_NT)
                cr = crow_ref[0, j, 2 * hp + hh:2 * hp + hh + 1, :]
                s = s + (cc[hh] - cr)
                if sub is not None:
                    s = jnp.where(col + sub * tk <= row, s, NEG_BIG)
                m_new = jnp.maximum(m, jnp.max(s, axis=1, keepdims=True))
                alpha = jnp.exp(m - m_new)
                p = jnp.exp(s - m_new)
                l_new = alpha * l + jnp.sum(p, axis=1, keepdims=True)
                acc_new = alpha * acc + _mm(p.astype(BF16), v2)
                new.append((m_new, l_new, acc_new))
            return tuple(new)

        init_one = (jnp.full((tq, 1), NEG_BIG, F32), jnp.zeros((tq, 1), F32),
                    jnp.zeros((tq, LANES), F32))
        carry = lax.fori_loop(0, i * nsub, lambda j, c: step(j, c, None), (init_one, init_one))
        for sub in range(nsub):
            carry = step(i * nsub + sub, carry, sub)
        o0 = carry[0][2] / carry[0][1]
        o1 = carry[1][2] / carry[1][1]
        o_ref[0, :, ls] = jnp.where(low, o0, o1).astype(BF16)


def _fox(flag, bound, q, k, v, ccol, crow, tq, tk):
    bsz, seq, _ = q.shape
    kern = functools.partial(_fox_kernel, tq=tq, tk=tk)
    return pl.pallas_call(
        kern,
        grid=(bsz, seq // tq),
        in_specs=[
            pl.BlockSpec(memory_space=pltpu.SMEM),
            pl.BlockSpec(memory_space=pltpu.SMEM),
            pl.BlockSpec((1, tq, FOX_WIDTH), lambda b, i: (b, i, 0)),
            pl.BlockSpec((1, seq, FOX_WIDTH), lambda b, i: (b, 0, 0)),
            pl.BlockSpec((1, seq, FOX_WIDTH), lambda b, i: (b, 0, 0)),
            pl.BlockSpec((1, tq, LANES), lambda b, i: (b, i, 0)),
            pl.BlockSpec((1, seq // tk, 16, tk), lambda b, i: (b, 0, 0, 0)),
        ],
        out_specs=pl.BlockSpec((1, tq, FOX_WIDTH), lambda b, i: (b, i, 0)),
        out_shape=jax.ShapeDtypeStruct((bsz, seq, FOX_WIDTH), BF16),
        scratch_shapes=[pltpu.VMEM((PAIRS, tq, 2 * LANES), F32),
                        pltpu.VMEM((FOX_HEADS, tq, LANES), BF16),
                        pltpu.VMEM((FOX_HEADS, tq, LANES), F32)],
        compiler_params=pltpu.CompilerParams(
            dimension_semantics=("arbitrary", "arbitrary"), vmem_limit_bytes=VMEM_LIMIT),
        name="fox",
    )(flag, bound, q, k, v, ccol, crow)


def _unit_tri_inverses(mats, bs, passes, tick=lambda: None):
    c, w = mats[0].shape
    n = range(len(mats))
    lane_cache = {}

    def lane_ids(s):
        if s not in lane_cache:
            lane_cache[s] = lax.broadcasted_iota(jnp.int32, (s, w), 1)
        return lane_cache[s]

    def terms(x):
        return _split_bf16(x, passes)

    def dot_terms(a_t, b_t):
        out = None
        for i, ai in enumerate(a_t):
            for j, bj in enumerate(b_t):
                if i + j < max(len(a_t), len(b_t)):
                    t = lax.dot_general(ai, bj, _NN, preferred_element_type=F32)
                    out = t if out is None else out + t
        return out

    def block_rows(x, s, offset):
        lb = jnp.right_shift(lane_ids(s), s.bit_length() - 1)
        zero = jnp.zeros((s, w), BF16)
        keep = [lb == j for j in range(w // s)]
        out = []
        for t in terms(x):
            rows = []
            for j in range(w // s):
                if offset and j % 2 == 0:
                    rows.append(zero)
                else:
                    rows.append(jnp.where(keep[j - offset], t, zero))
            out.append(jnp.concatenate(rows, axis=0))
        return out

    def mm(a, b_terms):
        return dot_terms(terms(a), b_terms)

    s = bs
    sh = s.bit_length() - 1
    in_mat = jnp.bitwise_and(lane_ids(s), c - 1)
    q = []
    for a2 in mats:
        d = a2[0:s, :]
        for r in range(1, c // s):
            d = jnp.where(jnp.right_shift(in_mat, sh) == r, a2[r * s:(r + 1) * s, :], d)
        q.append(d)
    p = [mm(q[i], block_rows(q[i], s, 0)) for i in n]
    tick()
    for _ in range(s.bit_length() - 3):
        both = [mm(jnp.concatenate([q[i], p[i]], axis=0), block_rows(p[i], s, 0)) for i in n]
        q = [q[i] + p[i] + both[i][0:s] for i in n]
        p = [both[i][s:2 * s] for i in n]
        tick()
    q = [q[i] + p[i] + mm(q[i], block_rows(p[i], s, 0)) for i in n]
    tick()

    while s < c:
        sh = s.bit_length() - 1
        lane = lane_ids(s)
        first = jnp.bitwise_and(lane, s) == 0
        pair_id = jnp.right_shift(jnp.bitwise_and(lane, c - 1), sh + 1)
        l21 = []
        for a2 in mats:
            z = jnp.zeros((s, w), F32)
            for m in range(c // (2 * s)):
                rows = a2[(2 * m + 1) * s:(2 * m + 2) * s, :]
                z = jnp.where(pair_id == m, jnp.where(first, rows, 0.0), z)
            l21.append(z)
        x = [l21[i] + mm(l21[i], block_rows(q[i], s, 0)) for i in n]
        tick()
        t21 = [x[i] + mm(q[i], block_rows(x[i], s, 1)) for i in n]
        tick()
        q = [jnp.concatenate([jnp.where(first, q[i], 0.0), jnp.where(first, t21[i], q[i])], axis=0)
             for i in n]
        s *= 2
    return q


def _head_sums(x, e, pa):
    wd = e.shape[0]
    return jnp.concatenate(
        [_mm(x[:, j:j + wd], e, pa, 1) for j in range(0, x.shape[1], wd)], axis=1)


def _rwkv_kernel(rw_ref, w0_ref, w2_ref, a0_ref, a2_ref, g2_ref, kk_ref, ka_ref,
                 rk_ref, lg_ref, lb_ref, e_ref, tri_ref, o_ref, s_ref, *, prec, nb, group):
    cch = CHUNK
    rows = group * cch
    w = RWKV_WIDTH
    c = pl.program_id(1)
    pg, pd, ps = prec
    e = e_ref[...]

    @pl.when(c == 0)
    def _():
        s_ref[...] = jnp.zeros_like(s_ref)

    lane = lax.broadcasted_iota(jnp.int32, (cch, LANES), 1)
    low = lane < HEAD_DIM
    ri = lax.broadcasted_iota(jnp.int32, (cch, cch), 0)
    ci = lax.broadcasted_iota(jnp.int32, (cch, cch), 1)
    strict = ci < ri
    incl = ci <= ri
    blockdiag = (ri < HEAD_DIM) == (ci < HEAD_DIM)
    sl = [slice(LANES * hp, LANES * (hp + 1)) for hp in range(PAIRS)]
    rs = [slice(cch * i, cch * (i + 1)) for i in range(group)]
    combos = [(i, hp) for i in range(group) for hp in range(PAIRS)]
    n = range(len(combos))

    def halves(z):
        zero = jnp.zeros_like(z)
        return jnp.concatenate([jnp.where(low, z, zero), jnp.where(low, zero, z)], axis=0)

    def preparation(gi, out):
        base = gi * group

        def shifted(lo, hi):
            return rw_ref[base:base + group, :, lo:hi].reshape(rows, hi - lo)

        wa = shifted(OFF_LORA, OFF_GDN)
        gdn = shifted(OFF_GDN, RW_COLS)
        wlog = _log_sigmoid(w0_ref[...] + _mm(jnp.tanh(wa), w2_ref[...])) - 0.5
        logd = -jnp.exp(wlog)
        a = _sigmoid(a0_ref[...] + _mm(wa, a2_ref[...]))
        out["g"] = _mm(_sigmoid(gdn), g2_ref[...])
        yield
        tri = tri_ref[...]
        cum = jnp.concatenate([_mm(tri, logd[rs[i]], 1, 2) for i in range(group)], axis=0)
        lasts = [cum[cch * (i + 1) - 1:cch * (i + 1), :] for i in range(group)]
        clast = jnp.concatenate([jnp.broadcast_to(z, (cch, w)) for z in lasts], axis=0)
        out["pc"] = [jnp.exp(z) for z in lasts]
        k_n = shifted(w, 2 * w)
        kk = k_n * kk_ref[...]
        kk = kk * lax.rsqrt(jnp.maximum(_head_sums(kk * kk, e, 2), L2_EPS * L2_EPS))
        k_n = k_n * (1.0 + (a - 1.0) * ka_ref[...])
        yield
        avec = -kk
        bvec = kk * a
        einv = jnp.exp(-cum)
        out["at"] = (avec * jnp.exp(cum - logd)).astype(BF16)
        out["bt"] = (bvec * einv).astype(BF16)
        out["kt"] = (k_n * einv).astype(BF16)
        yield
        etail = jnp.exp(clast - cum)
        out["bh"] = (bvec * etail).astype(BF16)
        out["kh"] = (k_n * etail).astype(BF16)
        yield
        r_n = shifted(0, w)
        out["rt"] = (r_n * jnp.exp(cum)).astype(BF16)
        bonus = _head_sums(r_n * k_n * rk_ref[...], e, 1)
        yield
        v_n = shifted(2 * w, 3 * w)
        out["v"] = v_n.astype(BF16)
        out["bv"] = bonus * v_n
        yield

    def algebra(gi, ops, tick):
        base = gi * group

        def blk(name, j):
            i, hp = combos[j]
            return ops[name][rs[i], sl[hp]]

        aab, aak, arbk = [], [], []
        for j in n:
            rb = jnp.concatenate([blk("bt", j), blk("kt", j)], axis=0)
            blocks = []
            for hh in range(2):
                sel = low if hh == 0 else jnp.logical_not(low)
                zero = jnp.zeros((cch, LANES), BF16)
                la = jnp.concatenate([jnp.where(sel, blk("at", j), zero),
                                      jnp.where(sel, blk("rt", j), zero)], axis=0)
                blocks.append(_mm(la, rb, pg, pg, dims=_NT))
            aab.append(jnp.concatenate(
                [jnp.where(strict, gm[0:cch, 0:cch], 0.0) for gm in blocks], axis=1))
            aak.append(jnp.concatenate(
                [jnp.where(strict, gm[0:cch, cch:2 * cch], 0.0) for gm in blocks], axis=1))
            arbk.append(jnp.concatenate(
                [jnp.where(incl, gm[cch:2 * cch, 0:cch], 0.0) for gm in blocks]
                + [jnp.where(incl, gm[cch:2 * cch, cch:2 * cch], 0.0) for gm in blocks], axis=1))
        tick()
        qm = _unit_tri_inverses(aab, INV_BLOCK, pd, tick)

        sp = [s_ref[base + i, hp] for i, hp in combos]
        vst = [halves(blk("v", j)) for j in n]
        rhs = [_mm(blk("at", j), sp[j], ps, ps, dims=_NT) + _mm(aak[j], vst[j], ps, ps) for j in n]
        tick()
        u = [rhs[j] + _mm(qm[j], halves(rhs[j]), ps, ps) for j in n]
        tick()
        ys = [_mm(blk("rt", j), sp[j], ps, ps, dims=_NT)
              + _mm(arbk[j], jnp.concatenate([halves(u[j]).astype(BF16), vst[j]], axis=0), ps, ps)
              for j in n]
        tick()
        for j in n:
            i, hp = combos[j]
            uv = jnp.concatenate([u[j], blk("v", j).astype(F32)], axis=0)
            bk = jnp.concatenate([blk("bh", j), blk("kh", j)], axis=0)
            upd = _mm(uv, bk, ps, ps, dims=_TN)
            s_ref[base + i, hp] = (sp[j] * ops["pc"][i][:, sl[hp]]
                                   + jnp.where(blockdiag, upd, 0.0))
        tick()
        return jnp.concatenate(
            [jnp.concatenate(ys[PAIRS * i:PAIRS * (i + 1)], axis=1) for i in range(group)], axis=0)

    def finish(gi, y, ops):
        base = gi * group
        inv_n = 1.0 / HEAD_DIM
        mean = _head_sums(y, e, 1) * inv_n
        d = y - mean
        yield
        var = _head_sums(d * d, e, 1) * inv_n
        yn = d * lax.rsqrt(var + GN_EPS) * lg_ref[...] + lb_ref[...]
        yield
        o_ref[base:base + group] = ((yn + ops["bv"]) * ops["g"]).astype(BF16).reshape(group, cch, w)
        yield

    def drain(gen):
        if gen is not None:
            for _ in gen:
                pass

    ngroups = nb // group
    outs = [dict() for _ in range(ngroups)]
    preps = [preparation(gi, outs[gi]) for gi in range(ngroups)]
    drain(preps[0])
    fin = None
    for gi in range(ngroups):
        nxt = preps[gi + 1] if gi + 1 < ngroups else None
        side = [g for g in (fin, nxt) if g is not None]

        def tick(side=side):
            for g in side:
                try:
                    next(g)
                    return
                except StopIteration:
                    continue

        y = algebra(gi, outs[gi], tick)
        drain(fin)
        drain(nxt)
        fin = finish(gi, y, outs[gi])
    drain(fin)


def _rwkv(rw, w0, w2, a0, a2, g2, k_k, k_a, r_k, lnx_g, lnx_b, e, prec):
    bsz, seq, _ = rw.shape
    cch = CHUNK
    nb = RWKV_SEQS if bsz % RWKV_SEQS == 0 else 1
    group = RWKV_GROUP if nb % RWKV_GROUP == 0 else 1
    row = lax.broadcasted_iota(jnp.int32, (cch, cch), 0)
    col = lax.broadcasted_iota(jnp.int32, (cch, cch), 1)
    tri = (col <= row).astype(BF16)
    const = lambda shape: pl.BlockSpec(shape, lambda b, t: (0,) * len(shape))
    args = (w0, w2, a0, a2, g2, k_k, k_a, r_k, lnx_g, lnx_b, e, tri)
    return pl.pallas_call(
        functools.partial(_rwkv_kernel, prec=prec, nb=nb, group=group),
        grid=(bsz // nb, seq // cch),
        in_specs=[pl.BlockSpec((nb, cch, RW_COLS), lambda b, t: (b, t, 0))]
        + [const(a.shape) for a in args],
        out_specs=pl.BlockSpec((nb, cch, RWKV_WIDTH), lambda b, t: (b, t, 0)),
        out_shape=jax.ShapeDtypeStruct((bsz, seq, RWKV_WIDTH), BF16),
        scratch_shapes=[pltpu.VMEM((nb, PAIRS, LANES, LANES), F32)],
        compiler_params=pltpu.CompilerParams(
            dimension_semantics=("arbitrary", "arbitrary"), vmem_limit_bytes=VMEM_LIMIT),
        name="rwkv",
    )(rw, *args)


def _out_ffn_kernel(of_ref, or_ref, x_ref, mod_ref, g2_ref, wt_ref, wb_ref, wg_ref, wu_ref, wd_ref,
                    o_ref, *, slabs):
    mod = mod_ref[0]
    gt1, sh2, sc2, gt2 = mod[2:3, :], mod[3:4, :], mod[4:5, :], mod[5:6, :]
    mix = _mm(of_ref[0], wt_ref[...]) + _mm(or_ref[0], wb_ref[...])
    x1 = x_ref[0] + gt1 * mix
    ms = jnp.mean(x1 * x1, axis=-1, keepdims=True)
    y = x1 * lax.rsqrt(ms + RMS_EPS) * g2_ref[...]
    h2 = (y * (1.0 + sc2) + sh2).astype(BF16)
    acc = None
    for lo, hi in slabs:
        gate = _mm(h2, wg_ref[:, lo:hi])
        up = _mm(h2, wu_ref[:, lo:hi])
        act = (gate * _sigmoid(gate) * up).astype(BF16)
        part = _mm(act, wd_ref[lo:hi, :])
        acc = part if acc is None else acc + part
    o_ref[0] = x1 + gt2 * acc


def _out_ffn(o_fox, o_rwkv, x, mod, g2, w_top, w_bot, wg, wu, wd, tm):
    bsz, seq, d = x.shape
    dff = wg.shape[1]
    tiles = dff // MXU_DIM if dff % MXU_DIM == 0 else 1
    cut = (tiles // 2) * (dff // tiles)
    slabs = ((0, cut), (cut, dff)) if cut else ((0, dff),)
    resident = lambda shape: pl.BlockSpec(shape, lambda b, t: (0,) * len(shape),
                                          pipeline_mode=pl.Buffered(1))
    return pl.pallas_call(
        functools.partial(_out_ffn_kernel, slabs=slabs),
        grid=(bsz, seq // tm),
        in_specs=[
            pl.BlockSpec((1, tm, FOX_WIDTH), lambda b, t: (b, t, 0)),
            pl.BlockSpec((1, tm, RWKV_WIDTH), lambda b, t: (b, t, 0)),
            pl.BlockSpec((1, tm, d), lambda b, t: (b, t, 0)),
            pl.BlockSpec((1, 6, d), lambda b, t: (b, 0, 0)),
            resident((1, d)), resident(w_top.shape), resident(w_bot.shape),
            resident(wg.shape), resident(wu.shape), resident(wd.shape),
        ],
        out_specs=pl.BlockSpec((1, tm, d), lambda b, t: (b, t, 0)),
        out_shape=jax.ShapeDtypeStruct((bsz, seq, d), F32),
        compiler_params=pltpu.CompilerParams(
            dimension_semantics=("arbitrary", "arbitrary"), vmem_limit_bytes=VMEM_LIMIT),
        name="out_ffn",
    )(o_fox, o_rwkv, x, mod, g2, w_top, w_bot, wg, wu, wd)


def _pad_cols(w, n):
    return jnp.pad(w, ((0, 0), (0, n - w.shape[1])))


def _pad_rows(w, n):
    return jnp.pad(w, ((0, n - w.shape[0]), (0, 0)))


def _layer(x, mod, norm1_g, norm2_g, w_in, fox_f_bias, fox_q_gain, fox_k_gain, rwkv_mu, rwkv_w0,
           rwkv_w2, rwkv_a0, rwkv_a2, rwkv_g2, rwkv_k_k, rwkv_k_a, rwkv_r_k, rwkv_lnx_g,
           rwkv_lnx_b, w_out, ffn_w_gate, ffn_w_up, ffn_w_down, *, tm, tq, tk, prec):
    bsz, seq, d = x.shape
    w = RWKV_WIDTH
    nfox = 3 * FOX_WIDTH + FOX_HEADS

    wqkv = w_in[:, 0:3 * FOX_WIDTH].astype(BF16)
    wf = _pad_cols(w_in[:, 3 * FOX_WIDTH:nfox], LANES).astype(BF16)
    wr = w_in[:, nfox:]
    assert 3 * w == OFF_LORA and DECAY_LORA + A_LORA == OFF_GDN - OFF_LORA
    wrw = _pad_cols(wr, RW_COLS).astype(BF16)
    mu_p = _pad_cols(rwkv_mu.reshape(1, -1), RW_COLS)
    fbrow = _pad_cols(fox_f_bias.reshape(1, -1), LANES)
    qg = jnp.tile(fox_q_gain, (1, 1)).reshape(1, FOX_WIDTH)
    kg = fox_k_gain.reshape(1, FOX_WIDTH)
    hi = lax.broadcasted_iota(jnp.int32, (MXU_DIM, MXU_DIM), 0) // HEAD_DIM
    hj = lax.broadcasted_iota(jnp.int32, (MXU_DIM, MXU_DIM), 1) // HEAD_DIM
    e = (hi == hj).astype(BF16)

    q, k, v, ccol, crow, rw = _in_proj(
        x, mod, norm1_g.reshape(1, d), wqkv, wf, wrw, fbrow, qg, kg, e, mu_p, tm, tk)
    bound = (1.05 * HEAD_DIM ** 0.5) * jnp.max(jnp.abs(fox_q_gain)) * jnp.max(jnp.abs(fox_k_gain))
    flag = (bound <= FOX_BOUND_MAX).astype(jnp.int32)
    o_fox = _fox(flag.reshape(1), bound.astype(F32).reshape(1), q, k, v, ccol, crow, tq, tk)
    o_rwkv = _rwkv(
        rw, rwkv_w0.reshape(1, w), _pad_rows(rwkv_w2, LANES).astype(BF16),
        rwkv_a0.reshape(1, w), jnp.pad(rwkv_a2, ((DECAY_LORA, 0), (0, 0))).astype(BF16),
        _pad_rows(rwkv_g2, RW_COLS - OFF_GDN).astype(BF16), rwkv_k_k.reshape(1, w), rwkv_k_a.reshape(1, w),
        rwkv_r_k.reshape(1, w), rwkv_lnx_g.reshape(1, w), rwkv_lnx_b.reshape(1, w), e, prec)
    wo = w_out.astype(BF16)
    return _out_ffn(o_fox, o_rwkv, x, mod, norm2_g.reshape(1, d), wo[0:FOX_WIDTH], wo[FOX_WIDTH:],
                    ffn_w_gate.astype(BF16), ffn_w_up.astype(BF16), ffn_w_down.astype(BF16),
                    tm)


def kernel(x, c, ada_w, ada_b, norm1_g, norm2_g, w_in, fox_f_bias, fox_q_gain, fox_k_gain, rwkv_mu,
           rwkv_w0, rwkv_w2, rwkv_a0, rwkv_a2, rwkv_g2, rwkv_k_k, rwkv_k_a, rwkv_r_k, rwkv_lnx_g,
           rwkv_lnx_b, w_out, ffn_w_gate, ffn_w_up, ffn_w_down):
    bsz, seq, d = x.shape
    depth = ada_w.shape[0]
    tm = min(ROW_TILE, seq)
    tk = min(KEY_TILE, seq)
    tq = min(QUERY_TILE, seq)
    assert seq % tm == 0 and seq % tq == 0 and tq % tk == 0 and seq % CHUNK == 0
    for l in range(depth):
        mod = _ada(c, ada_w[l], ada_b[l]).reshape(bsz, 6, d)
        x = _layer(x, mod, norm1_g[l], norm2_g[l], w_in[l], fox_f_bias[l], fox_q_gain[l],
                   fox_k_gain[l], rwkv_mu[l], rwkv_w0[l], rwkv_w2[l], rwkv_a0[l], rwkv_a2[l],
                   rwkv_g2[l], rwkv_k_k[l], rwkv_k_a[l], rwkv_r_k[l], rwkv_lnx_g[l],
                   rwkv_lnx_b[l], w_out[l], ffn_w_gate[l], ffn_w_up[l], ffn_w_down[l],
                   tm=tm, tq=tq, tk=tk, prec=RWKV_PASSES)
    return x
```

```python
import functools

import jax
import jax.numpy as jnp
from jax import lax
from jax.experimental import pallas as pl
from jax.experimental.pallas import tpu as pltpu

F32 = jnp.float32
BF16 = jnp.bfloat16

HEAD_DIM = 64
FOX_HEADS = 8
RWKV_HEADS = 8
FOX_WIDTH = FOX_HEADS * HEAD_DIM
RWKV_WIDTH = RWKV_HEADS * HEAD_DIM
DECAY_LORA = 64
A_LORA = 64
GATE_LORA = 160
RMS_EPS = 1e-6
GN_EPS = 64e-5
L2_EPS = 1e-12

LANES = 128
MXU_DIM = 256
PAIRS = FOX_HEADS // 2
RWKV_SEQS = 4
INV_BLOCK = 64
RW_COLS = 1920
OFF_LORA, OFF_GDN = 1536, 1664
CHUNK = 128
NEG_BIG = -1e30
RWKV_PASSES = (1, 1, 1)
FOX_BOUND_MAX = 30.0
ROW_TILE = 512
QUERY_TILE = 512
KEY_TILE = 256
VMEM_LIMIT = 56 * 1024 * 1024


def _split_bf16(x, n):
    if x.dtype == BF16:
        return [x]
    parts = []
    r = x
    for i in range(n):
        p = r.astype(BF16)
        parts.append(p)
        if i < n - 1:
            r = r - p.astype(F32)
    return parts


_NN = (((1,), (0,)), ((), ()))
_NT = (((1,), (1,)), ((), ()))
_TN = (((0,), (0,)), ((), ()))


def _mm(a, b, pa=1, pb=1, dims=_NN):
    a_parts = _split_bf16(a, pa)
    b_parts = _split_bf16(b, pb)
    order = max(len(a_parts), len(b_parts))
    out = None
    for i, ai in enumerate(a_parts):
        for j, bj in enumerate(b_parts):
            if i + j >= order:
                continue
            t = lax.dot_general(ai, bj, dims, preferred_element_type=F32)
            out = t if out is None else out + t
    return out


def _log_sigmoid(z):
    return jnp.minimum(z, 0.0) - jnp.log(1.0 + jnp.exp(-jnp.abs(z)))


def _sigmoid(z):
    return 1.0 / (1.0 + jnp.exp(-z))


def _ada_kernel(c_ref, w_ref, b_ref, o_ref):
    c = c_ref[...]
    cond = c * _sigmoid(c)
    o_ref[...] = _mm(cond, w_ref[0], 2, 2) + b_ref[0]


def _ada(c, ada_w, ada_b, layer):
    bsz, d = c.shape
    depth, _, n = ada_w.shape
    tn = 512
    return pl.pallas_call(
        _ada_kernel,
        grid=(n // tn,),
        in_specs=[
            pl.BlockSpec((bsz, d), lambda j: (0, 0)),
            pl.BlockSpec((1, d, tn), lambda j: (layer, 0, j)),
            pl.BlockSpec((1, 1, tn), lambda j: (layer, 0, j)),
        ],
        out_specs=pl.BlockSpec((bsz, tn), lambda j: (0, j)),
        out_shape=jax.ShapeDtypeStruct((bsz, n), F32),
        compiler_params=pltpu.CompilerParams(dimension_semantics=("arbitrary",)),
        name="ada",
    )(c, ada_w, ada_b.reshape(depth, 1, n))


def _in_proj_kernel(x_ref, mod_ref, g1_ref, wqkv_ref, wf_ref, wrw_ref,
                    fbrow_ref, qg_ref, kg_ref, e_ref, trilo_ref, mu_ref,
                    q_ref, k_ref, v_ref, ccol_ref, crow_ref, rw_ref,
                    carry_row, prev_ref, *, tm, tk):
    t = pl.program_id(1)

    @pl.when(t == 0)
    def _():
        carry_row[...] = jnp.zeros_like(carry_row)
        prev_ref[...] = jnp.zeros_like(prev_ref)

    x = x_ref[0]
    mod = mod_ref[0]
    sh1 = mod[0:1, :]
    sc1 = mod[1:2, :]
    ms = jnp.mean(x * x, axis=-1, keepdims=True)
    y = x * lax.rsqrt(ms + RMS_EPS) * g1_ref[...]
    hb = (y * (1.0 + sc1) + sh1).astype(BF16)

    qkv = _mm(hb, wqkv_ref[...])
    e = e_ref[...]
    q = qkv[:, 0:FOX_WIDTH]
    k = qkv[:, FOX_WIDTH:2 * FOX_WIDTH]
    qms = _head_sums(q * q, e, 1) * (1.0 / HEAD_DIM)
    kms = _head_sums(k * k, e, 1) * (1.0 / HEAD_DIM)
    q_ref[0] = (q * lax.rsqrt(qms + RMS_EPS) * qg_ref[...] * (HEAD_DIM ** -0.5)).astype(BF16)
    k_ref[0] = (k * lax.rsqrt(kms + RMS_EPS) * kg_ref[...]).astype(BF16)
    v_ref[0] = qkv[:, 2 * FOX_WIDTH:3 * FOX_WIDTH].astype(BF16)

    rw = _mm(hb, wrw_ref[...])
    rolled = pltpu.roll(rw, 1, 0)
    first_row = lax.broadcasted_iota(jnp.int32, (8, 1), 0) == 0
    top = jnp.where(first_row, prev_ref[0:1, :], rolled[0:8])
    prev_ref[0:1, :] = rw[tm - 1:tm, :]
    rw_ref[0] = rw + (jnp.concatenate([top, rolled[8:tm]], axis=0) - rw) * mu_ref[...]

    lf = _log_sigmoid(_mm(hb, wf_ref[...]) + fbrow_ref[...])
    blocks = []
    carry = carry_row[...]
    for j in range(tm // LANES):
        cj = _mm(trilo_ref[...], lf[j * LANES:(j + 1) * LANES], 1, 3) + carry
        blocks.append(cj)
        carry = cj[LANES - 1:LANES, :]
    carry_row[...] = carry
    ccol_ref[0] = jnp.concatenate(blocks, axis=0)
    for j, cj in enumerate(blocks):
        jj, off = divmod(j * LANES, tk)
        crow_ref[0, jj, :, off:off + LANES] = cj.T[0:16, :]


def _in_proj(x, mod, g1, wqkv, wf, wrw, fbrow, qg, kg, e, mu, tm, tk):
    bsz, seq, d = x.shape
    nt = seq // tm
    row = lax.broadcasted_iota(jnp.int32, (LANES, LANES), 0)
    col = lax.broadcasted_iota(jnp.int32, (LANES, LANES), 1)
    trilo = (col <= row).astype(BF16)
    const = lambda shape: pl.BlockSpec(shape, lambda b, t: (0,) * len(shape))
    kern = functools.partial(_in_proj_kernel, tm=tm, tk=tk)
    return pl.pallas_call(
        kern,
        grid=(bsz, nt),
        in_specs=[
            pl.BlockSpec((1, tm, d), lambda b, t: (b, t, 0)),
            pl.BlockSpec((1, 6, d), lambda b, t: (b, 0, 0)),
            const((1, d)),
            const(wqkv.shape), const(wf.shape), const(wrw.shape),
            const(fbrow.shape), const(qg.shape), const(kg.shape),
            const(e.shape), const(trilo.shape), const(mu.shape),
        ],
        out_specs=[
            pl.BlockSpec((1, tm, FOX_WIDTH), lambda b, t: (b, t, 0)),
            pl.BlockSpec((1, tm, FOX_WIDTH), lambda b, t: (b, t, 0)),
            pl.BlockSpec((1, tm, FOX_WIDTH), lambda b, t: (b, t, 0)),
            pl.BlockSpec((1, tm, LANES), lambda b, t: (b, t, 0)),
            pl.BlockSpec((1, tm // tk, 16, tk), lambda b, t: (b, t, 0, 0)),
            pl.BlockSpec((1, tm, RW_COLS), lambda b, t: (b, t, 0)),
        ],
        out_shape=[
            jax.ShapeDtypeStruct((bsz, seq, FOX_WIDTH), BF16),
            jax.ShapeDtypeStruct((bsz, seq, FOX_WIDTH), BF16),
            jax.ShapeDtypeStruct((bsz, seq, FOX_WIDTH), BF16),
            jax.ShapeDtypeStruct((bsz, seq, LANES), F32),
            jax.ShapeDtypeStruct((bsz, seq // tk, 16, tk), F32),
            jax.ShapeDtypeStruct((bsz, seq, RW_COLS), F32),
        ],
        scratch_shapes=[pltpu.VMEM((1, LANES), F32), pltpu.VMEM((8, RW_COLS), F32)],
        compiler_params=pltpu.CompilerParams(
            dimension_semantics=("arbitrary", "arbitrary"), vmem_limit_bytes=VMEM_LIMIT),
        name="in_proj",
    )(x, mod, g1, wqkv, wf, wrw, fbrow, qg, kg, e, trilo, mu)


def _fox_bounded(bound, q_ref, k_ref, v_ref, ccol_ref, crow_ref, o_ref, acc_ref, qm_ref, cb_ref,
                 *, tq, tk):
    i = pl.program_id(1)
    nsub = tq // tk
    lane = lax.broadcasted_iota(jnp.int32, (tq, LANES), 1)
    low = lane < HEAD_DIM
    klow = lax.broadcasted_iota(jnp.int32, (tk, LANES), 1) < HEAD_DIM
    one_lo = jnp.where(klow, 1.0, 0.0).astype(BF16)
    one_hi = jnp.where(klow, 0.0, 1.0).astype(BF16)

    ccol = ccol_ref[0] - bound
    for hp in range(PAIRS):
        q2 = q_ref[0, :, LANES * hp:LANES * (hp + 1)]
        zero = jnp.zeros_like(q2)
        qm_ref[2 * hp] = jnp.where(low, q2, zero)
        qm_ref[2 * hp + 1] = jnp.where(low, zero, q2)
        for hh in range(2):
            h = 2 * hp + hh
            cb_ref[h] = jnp.broadcast_to(ccol[:, h:h + 1], (tq, LANES))
    acc_ref[...] = jnp.zeros_like(acc_ref)

    def step(j, sub):
        r0 = 0 if sub is None else sub * tk
        rows = tq - r0
        ks = pl.multiple_of(j * tk, tk)
        scores = []
        for hp in range(PAIRS):
            k2 = k_ref[0, pl.ds(ks, tk), LANES * hp:LANES * (hp + 1)]
            for hh in range(2):
                scores.append(_mm(qm_ref[2 * hp + hh, r0:tq, :], k2, dims=_NT))
        if sub is not None:
            causal = (lax.broadcasted_iota(jnp.int32, (rows, tk), 1)
                      <= lax.broadcasted_iota(jnp.int32, (rows, tk), 0))
        probs = []
        for h in range(FOX_HEADS):
            cb = cb_ref[h, r0:tq, :]
            bias = jnp.concatenate([cb] * (tk // LANES), axis=1) - crow_ref[0, j, h:h + 1, :]
            s = scores[h] + bias
            if sub is not None:
                s = jnp.where(causal, s, NEG_BIG)
            probs.append(jnp.exp(s).astype(BF16))
        for hp in range(PAIRS):
            v2 = v_ref[0, pl.ds(ks, tk), LANES * hp:LANES * (hp + 1)]
            zero = jnp.zeros_like(v2)
            vaug = jnp.concatenate([
                jnp.concatenate([jnp.where(klow, v2, zero), one_lo], axis=1),
                jnp.concatenate([jnp.where(klow, zero, v2), one_hi], axis=1)], axis=0)
            acc_ref[hp, r0:tq, :] += _mm(
                jnp.concatenate([probs[2 * hp], probs[2 * hp + 1]], axis=1), vaug)

    def body(jo, carry):
        for m in range(nsub):
            step(jo * nsub + m, None)
        return carry

    lax.fori_loop(0, i, body, 0)
    for m in range(nsub):
        step(i * nsub + m, m)
    for hp in range(PAIRS):
        a = acc_ref[hp]
        o = a[:, 0:LANES] / a[:, LANES:2 * LANES]
        o_ref[0, :, LANES * hp:LANES * (hp + 1)] = o.astype(BF16)


def _fox_kernel(flag_ref, bound_ref, q_ref, k_ref, v_ref, ccol_ref, crow_ref, o_ref, acc_ref,
                qm_ref, cb_ref, *, tq, tk):
    @pl.when(flag_ref[0] == 1)
    def _():
        _fox_bounded(bound_ref[0], q_ref, k_ref, v_ref, ccol_ref, crow_ref, o_ref, acc_ref,
                     qm_ref, cb_ref, tq=tq, tk=tk)

    @pl.when(flag_ref[0] == 0)
    def _():
        _fox_running_max(q_ref, k_ref, v_ref, ccol_ref, crow_ref, o_ref, tq=tq, tk=tk)


def _fox_running_max(q_ref, k_ref, v_ref, ccol_ref, crow_ref, o_ref, *, tq, tk):
    i = pl.program_id(1)
    nsub = tq // tk
    lane = lax.broadcasted_iota(jnp.int32, (tq, LANES), 1)
    low = lane < HEAD_DIM
    row = lax.broadcasted_iota(jnp.int32, (tq, tk), 0)
    col = lax.broadcasted_iota(jnp.int32, (tq, tk), 1)
    ccol = ccol_ref[0]

    for hp in range(PAIRS):
        ls = slice(LANES * hp, LANES * (hp + 1))
        q2 = q_ref[0, :, ls]
        zero = jnp.zeros_like(q2)
        qm = (jnp.where(low, q2, zero), jnp.where(low, zero, q2))
        cc = tuple(ccol[:, 2 * hp + hh:2 * hp + hh + 1] for hh in range(2))

        def step(j, carry, sub, ls=ls, qm=qm, cc=cc, hp=hp):
            ks = pl.multiple_of(j * tk, tk)
            k2 = k_ref[0, pl.ds(ks, tk), ls]
            v2 = v_ref[0, pl.ds(ks, tk), ls]
            new = []
            for hh in range(2):
                m, l, acc = carry[hh]
                s = _mm(qm[hh], k2, dims=_NT)
                cr = crow_ref[0, j, 2 * hp + hh:2 * hp + hh + 1, :]
                s = s + (cc[hh] - cr)
                if sub is not None:
                    s = jnp.where(col + sub * tk <= row, s, NEG_BIG)
                m_new = jnp.maximum(m, jnp.max(s, axis=1, keepdims=True))
                alpha = jnp.exp(m - m_new)
                p = jnp.exp(s - m_new)
                l_new = alpha * l + jnp.sum(p, axis=1, keepdims=True)
                acc_new = alpha * acc + _mm(p.astype(BF16), v2)
                new.append((m_new, l_new, acc_new))
            return tuple(new)

        init_one = (jnp.full((tq, 1), NEG_BIG, F32), jnp.zeros((tq, 1), F32),
                    jnp.zeros((tq, LANES), F32))
        carry = lax.fori_loop(0, i * nsub, lambda j, c: step(j, c, None), (init_one, init_one))
        for sub in range(nsub):
            carry = step(i * nsub + sub, carry, sub)
        o0 = carry[0][2] / carry[0][1]
        o1 = carry[1][2] / carry[1][1]
        o_ref[0, :, ls] = jnp.where(low, o0, o1).astype(BF16)


def _fox(flag, bound, q, k, v, ccol, crow, tq, tk):
    bsz, seq, _ = q.shape
    kern = functools.partial(_fox_kernel, tq=tq, tk=tk)
    return pl.pallas_call(
        kern,
        grid=(bsz, seq // tq),
        in_specs=[
            pl.BlockSpec(memory_space=pltpu.SMEM),
            pl.BlockSpec(memory_space=pltpu.SMEM),
            pl.BlockSpec((1, tq, FOX_WIDTH), lambda b, i: (b, i, 0)),
            pl.BlockSpec((1, seq, FOX_WIDTH), lambda b, i: (b, 0, 0)),
            pl.BlockSpec((1, seq, FOX_WIDTH), lambda b, i: (b, 0, 0)),
            pl.BlockSpec((1, tq, LANES), lambda b, i: (b, i, 0)),
            pl.BlockSpec((1, seq // tk, 16, tk), lambda b, i: (b, 0, 0, 0)),
        ],
        out_specs=pl.BlockSpec((1, tq, FOX_WIDTH), lambda b, i: (b, i, 0)),
        out_shape=jax.ShapeDtypeStruct((bsz, seq, FOX_WIDTH), BF16),
        scratch_shapes=[pltpu.VMEM((PAIRS, tq, 2 * LANES), F32),
                        pltpu.VMEM((FOX_HEADS, tq, LANES), BF16),
                        pltpu.VMEM((FOX_HEADS, tq, LANES), F32)],
        compiler_params=pltpu.CompilerParams(
            dimension_semantics=("arbitrary", "arbitrary"), vmem_limit_bytes=VMEM_LIMIT),
        name="fox",
    )(flag, bound, q, k, v, ccol, crow)


def _unit_tri_inverses(mats, bs, passes):
    c, w = mats[0].shape
    n = range(len(mats))
    lane_cache = {}

    def lane_ids(s):
        if s not in lane_cache:
            lane_cache[s] = lax.broadcasted_iota(jnp.int32, (s, w), 1)
        return lane_cache[s]

    def terms(x):
        return _split_bf16(x, passes)

    def dot_terms(a_t, b_t):
        out = None
        for i, ai in enumerate(a_t):
            for j, bj in enumerate(b_t):
                if i + j < max(len(a_t), len(b_t)):
                    t = lax.dot_general(ai, bj, _NN, preferred_element_type=F32)
                    out = t if out is None else out + t
        return out

    def block_rows(x, s, offset):
        lb = jnp.right_shift(lane_ids(s), s.bit_length() - 1)
        zero = jnp.zeros((s, w), BF16)
        keep = [lb == j for j in range(w // s)]
        out = []
        for t in terms(x):
            rows = []
            for j in range(w // s):
                if offset and j % 2 == 0:
                    rows.append(zero)
                else:
                    rows.append(jnp.where(keep[j - offset], t, zero))
            out.append(jnp.concatenate(rows, axis=0))
        return out

    def mm(a, b_terms):
        return dot_terms(terms(a), b_terms)

    s = bs
    sh = s.bit_length() - 1
    in_mat = jnp.bitwise_and(lane_ids(s), c - 1)
    q = []
    for a2 in mats:
        d = a2[0:s, :]
        for r in range(1, c // s):
            d = jnp.where(jnp.right_shift(in_mat, sh) == r, a2[r * s:(r + 1) * s, :], d)
        q.append(d)
    p = [mm(q[i], block_rows(q[i], s, 0)) for i in n]
    for _ in range(s.bit_length() - 3):
        both = [mm(jnp.concatenate([q[i], p[i]], axis=0), block_rows(p[i], s, 0)) for i in n]
        q = [q[i] + p[i] + both[i][0:s] for i in n]
        p = [both[i][s:2 * s] for i in n]
    q = [q[i] + p[i] + mm(q[i], block_rows(p[i], s, 0)) for i in n]

    while s < c:
        sh = s.bit_length() - 1
        lane = lane_ids(s)
        first = jnp.bitwise_and(lane, s) == 0
        pair_id = jnp.right_shift(jnp.bitwise_and(lane, c - 1), sh + 1)
        l21 = []
        for a2 in mats:
            z = jnp.zeros((s, w), F32)
            for m in range(c // (2 * s)):
                rows = a2[(2 * m + 1) * s:(2 * m + 2) * s, :]
                z = jnp.where(pair_id == m, jnp.where(first, rows, 0.0), z)
            l21.append(z)
        x = [l21[i] + mm(l21[i], block_rows(q[i], s, 0)) for i in n]
        t21 = [x[i] + mm(q[i], block_rows(x[i], s, 1)) for i in n]
        q = [jnp.concatenate([jnp.where(first, q[i], 0.0), jnp.where(first, t21[i], q[i])], axis=0)
             for i in n]
        s *= 2
    return q


def _head_sums(x, e, pa):
    wd = e.shape[0]
    return jnp.concatenate(
        [_mm(x[:, j:j + wd], e, pa, 1) for j in range(0, x.shape[1], wd)], axis=1)


def _rwkv_kernel(rw_ref, w0_ref, w2_ref, a0_ref, a2_ref, g2_ref, kk_ref, ka_ref,
                 rk_ref, lg_ref, lb_ref, e_ref, tri_ref, o_ref, s_ref, *, prec, nb):
    cch = CHUNK
    rows = nb * cch
    w = RWKV_WIDTH
    c = pl.program_id(1)
    pg, pd, ps = prec
    e = e_ref[...]

    @pl.when(c == 0)
    def _():
        s_ref[...] = jnp.zeros_like(s_ref)

    lane = lax.broadcasted_iota(jnp.int32, (cch, LANES), 1)
    low = lane < HEAD_DIM
    ri = lax.broadcasted_iota(jnp.int32, (cch, cch), 0)
    ci = lax.broadcasted_iota(jnp.int32, (cch, cch), 1)
    strict = ci < ri
    incl = ci <= ri
    blockdiag = (ri < HEAD_DIM) == (ci < HEAD_DIM)
    sl = [slice(LANES * hp, LANES * (hp + 1)) for hp in range(PAIRS)]
    rs = [slice(cch * i, cch * (i + 1)) for i in range(nb)]
    combos = [(i, hp) for i in range(nb) for hp in range(PAIRS)]
    n = range(len(combos))

    def halves(z):
        zero = jnp.zeros_like(z)
        return jnp.concatenate([jnp.where(low, z, zero), jnp.where(low, zero, z)], axis=0)

    def shifted(lo, hi):
        return rw_ref[:, :, lo:hi].reshape(rows, hi - lo)

    wa = shifted(OFF_LORA, OFF_GDN)
    gdn = shifted(OFF_GDN, RW_COLS)
    wlog = _log_sigmoid(w0_ref[...] + _mm(jnp.tanh(wa), w2_ref[...])) - 0.5
    logd = -jnp.exp(wlog)
    a = _sigmoid(a0_ref[...] + _mm(wa, a2_ref[...]))
    g = _mm(_sigmoid(gdn), g2_ref[...])
    tri = tri_ref[...]
    cum = jnp.concatenate([_mm(tri, logd[rs[i]], 1, 2) for i in range(nb)], axis=0)
    lasts = [cum[cch * (i + 1) - 1:cch * (i + 1), :] for i in range(nb)]
    clast = jnp.concatenate([jnp.broadcast_to(z, (cch, w)) for z in lasts], axis=0)
    pc = [jnp.exp(z) for z in lasts]
    k = shifted(w, 2 * w)
    kk = k * kk_ref[...]
    kk = kk * lax.rsqrt(jnp.maximum(_head_sums(kk * kk, e, 2), L2_EPS * L2_EPS))
    k = k * (1.0 + (a - 1.0) * ka_ref[...])
    avec = -kk
    bvec = kk * a
    einv = jnp.exp(-cum)
    at = (avec * jnp.exp(cum - logd)).astype(BF16)
    bt = (bvec * einv).astype(BF16)
    kt = (k * einv).astype(BF16)
    etail = jnp.exp(clast - cum)
    bh = (bvec * etail).astype(BF16)
    kh = (k * etail).astype(BF16)
    r = shifted(0, w)
    rt = (r * jnp.exp(cum)).astype(BF16)
    bonus = _head_sums(r * k * rk_ref[...], e, 1)
    v = shifted(2 * w, 3 * w)
    vb = v.astype(BF16)
    bv = bonus * v

    def blk(z, j):
        i, hp = combos[j]
        return z[rs[i], sl[hp]]

    aab, aak, arbk = [], [], []
    for j in n:
        rb = jnp.concatenate([blk(bt, j), blk(kt, j)], axis=0)
        blocks = []
        for hh in range(2):
            sel = low if hh == 0 else jnp.logical_not(low)
            zero = jnp.zeros((cch, LANES), BF16)
            la = jnp.concatenate([jnp.where(sel, blk(at, j), zero),
                                  jnp.where(sel, blk(rt, j), zero)], axis=0)
            blocks.append(_mm(la, rb, pg, pg, dims=_NT))
        aab.append(jnp.concatenate(
            [jnp.where(strict, gm[0:cch, 0:cch], 0.0) for gm in blocks], axis=1))
        aak.append(jnp.concatenate(
            [jnp.where(strict, gm[0:cch, cch:2 * cch], 0.0) for gm in blocks], axis=1))
        arbk.append(jnp.concatenate(
            [jnp.where(incl, gm[cch:2 * cch, 0:cch], 0.0) for gm in blocks]
            + [jnp.where(incl, gm[cch:2 * cch, cch:2 * cch], 0.0) for gm in blocks], axis=1))
    qm = _unit_tri_inverses(aab, INV_BLOCK, pd)

    sp = [s_ref[i, hp] for i, hp in combos]
    vst = [halves(blk(vb, j)) for j in n]
    rhs = [_mm(blk(at, j), sp[j], ps, ps, dims=_NT) + _mm(aak[j], vst[j], ps, ps) for j in n]
    u = [rhs[j] + _mm(qm[j], halves(rhs[j]), ps, ps) for j in n]
    ys = [_mm(blk(rt, j), sp[j], ps, ps, dims=_NT)
          + _mm(arbk[j], jnp.concatenate([halves(u[j]).astype(BF16), vst[j]], axis=0), ps, ps)
          for j in n]
    for j in n:
        i, hp = combos[j]
        uv = jnp.concatenate([u[j], blk(vb, j).astype(F32)], axis=0)
        bk = jnp.concatenate([blk(bh, j), blk(kh, j)], axis=0)
        upd = _mm(uv, bk, ps, ps, dims=_TN)
        s_ref[i, hp] = sp[j] * pc[i][:, sl[hp]] + jnp.where(blockdiag, upd, 0.0)

    y = jnp.concatenate(
        [jnp.concatenate(ys[PAIRS * i:PAIRS * (i + 1)], axis=1) for i in range(nb)], axis=0)
    inv_n = 1.0 / HEAD_DIM
    mean = _head_sums(y, e, 1) * inv_n
    d = y - mean
    var = _head_sums(d * d, e, 1) * inv_n
    yn = d * lax.rsqrt(var + GN_EPS) * lg_ref[...] + lb_ref[...]
    o_ref[...] = ((yn + bv) * g).astype(BF16).reshape(nb, cch, w)


def _rwkv(rw, w0, w2, a0, a2, g2, k_k, k_a, r_k, lnx_g, lnx_b, e, prec):
    bsz, seq, _ = rw.shape
    cch = CHUNK
    nb = RWKV_SEQS if bsz % RWKV_SEQS == 0 else 1
    row = lax.broadcasted_iota(jnp.int32, (cch, cch), 0)
    col = lax.broadcasted_iota(jnp.int32, (cch, cch), 1)
    tri = (col <= row).astype(BF16)
    const = lambda shape: pl.BlockSpec(shape, lambda b, t: (0,) * len(shape))
    args = (w0, w2, a0, a2, g2, k_k, k_a, r_k, lnx_g, lnx_b, e, tri)
    return pl.pallas_call(
        functools.partial(_rwkv_kernel, prec=prec, nb=nb),
        grid=(bsz // nb, seq // cch),
        in_specs=[pl.BlockSpec((nb, cch, RW_COLS), lambda b, t: (b, t, 0))]
        + [const(a.shape) for a in args],
        out_specs=pl.BlockSpec((nb, cch, RWKV_WIDTH), lambda b, t: (b, t, 0)),
        out_shape=jax.ShapeDtypeStruct((bsz, seq, RWKV_WIDTH), BF16),
        scratch_shapes=[pltpu.VMEM((nb, PAIRS, LANES, LANES), F32)],
        compiler_params=pltpu.CompilerParams(
            dimension_semantics=("arbitrary", "arbitrary"), vmem_limit_bytes=VMEM_LIMIT),
        name="rwkv",
    )(rw, *args)


def _out_ffn_kernel(of_ref, or_ref, x_ref, mod_ref, g2_ref, wt_ref, wb_ref, wg_ref, wu_ref, wd_ref,
                    o_ref, *, slabs):
    mod = mod_ref[0]
    gt1, sh2, sc2, gt2 = mod[2:3, :], mod[3:4, :], mod[4:5, :], mod[5:6, :]
    mix = _mm(of_ref[0], wt_ref[...]) + _mm(or_ref[0], wb_ref[...])
    x1 = x_ref[0] + gt1 * mix
    ms = jnp.mean(x1 * x1, axis=-1, keepdims=True)
    y = x1 * lax.rsqrt(ms + RMS_EPS) * g2_ref[...]
    h2 = (y * (1.0 + sc2) + sh2).astype(BF16)
    acc = None
    for lo, hi in slabs:
        gate = _mm(h2, wg_ref[:, lo:hi])
        up = _mm(h2, wu_ref[:, lo:hi])
        act = (gate * _sigmoid(gate) * up).astype(BF16)
        part = _mm(act, wd_ref[lo:hi, :])
        acc = part if acc is None else acc + part
    o_ref[0] = x1 + gt2 * acc


def _out_ffn(o_fox, o_rwkv, x, mod, g2, w_top, w_bot, wg, wu, wd, tm):
    bsz, seq, d = x.shape
    dff = wg.shape[1]
    tiles = dff // MXU_DIM if dff % MXU_DIM == 0 else 1
    cut = (tiles // 2) * (dff // tiles)
    slabs = ((0, cut), (cut, dff)) if cut else ((0, dff),)
    resident = lambda shape: pl.BlockSpec(shape, lambda b, t: (0,) * len(shape),
                                          pipeline_mode=pl.Buffered(1))
    return pl.pallas_call(
        functools.partial(_out_ffn_kernel, slabs=slabs),
        grid=(bsz, seq // tm),
        in_specs=[
            pl.BlockSpec((1, tm, FOX_WIDTH), lambda b, t: (b, t, 0)),
            pl.BlockSpec((1, tm, RWKV_WIDTH), lambda b, t: (b, t, 0)),
            pl.BlockSpec((1, tm, d), lambda b, t: (b, t, 0)),
            pl.BlockSpec((1, 6, d), lambda b, t: (b, 0, 0)),
            resident((1, d)), resident(w_top.shape), resident(w_bot.shape),
            resident(wg.shape), resident(wu.shape), resident(wd.shape),
        ],
        out_specs=pl.BlockSpec((1, tm, d), lambda b, t: (b, t, 0)),
        out_shape=jax.ShapeDtypeStruct((bsz, seq, d), F32),
        compiler_params=pltpu.CompilerParams(
            dimension_semantics=("arbitrary", "arbitrary"), vmem_limit_bytes=VMEM_LIMIT),
        name="out_ffn",
    )(o_fox, o_rwkv, x, mod, g2, w_top, w_bot, wg, wu, wd)


def _pad_cols(w, n):
    return jnp.pad(w, ((0, 0), (0, n - w.shape[1])))


def _pad_rows(w, n):
    return jnp.pad(w, ((0, n - w.shape[0]), (0, 0)))


def _layer(x, mod, norm1_g, norm2_g, w_in, fox_f_bias, fox_q_gain, fox_k_gain, rwkv_mu, rwkv_w0,
           rwkv_w2, rwkv_a0, rwkv_a2, rwkv_g2, rwkv_k_k, rwkv_k_a, rwkv_r_k, rwkv_lnx_g,
           rwkv_lnx_b, w_out, ffn_w_gate, ffn_w_up, ffn_w_down, *, tm, tq, tk, prec):
    bsz, seq, d = x.shape
    w = RWKV_WIDTH
    nfox = 3 * FOX_WIDTH + FOX_HEADS

    wqkv = w_in[:, 0:3 * FOX_WIDTH].astype(BF16)
    wf = _pad_cols(w_in[:, 3 * FOX_WIDTH:nfox], LANES).astype(BF16)
    wr = w_in[:, nfox:]
    assert 3 * w == OFF_LORA and DECAY_LORA + A_LORA == OFF_GDN - OFF_LORA
    wrw = _pad_cols(wr, RW_COLS).astype(BF16)
    mu_p = _pad_cols(rwkv_mu.reshape(1, -1), RW_COLS)
    fbrow = _pad_cols(fox_f_bias.reshape(1, -1), LANES)
    qg = jnp.tile(fox_q_gain, (1, 1)).reshape(1, FOX_WIDTH)
    kg = fox_k_gain.reshape(1, FOX_WIDTH)
    hi = lax.broadcasted_iota(jnp.int32, (MXU_DIM, MXU_DIM), 0) // HEAD_DIM
    hj = lax.broadcasted_iota(jnp.int32, (MXU_DIM, MXU_DIM), 1) // HEAD_DIM
    e = (hi == hj).astype(BF16)

    q, k, v, ccol, crow, rw = _in_proj(
        x, mod, norm1_g.reshape(1, d), wqkv, wf, wrw, fbrow, qg, kg, e, mu_p, tm, tk)
    bound = (1.05 * HEAD_DIM ** 0.5) * jnp.max(jnp.abs(fox_q_gain)) * jnp.max(jnp.abs(fox_k_gain))
    flag = (bound <= FOX_BOUND_MAX).astype(jnp.int32)
    o_fox = _fox(flag.reshape(1), bound.astype(F32).reshape(1), q, k, v, ccol, crow, tq, tk)
    o_rwkv = _rwkv(
        rw, rwkv_w0.reshape(1, w), _pad_rows(rwkv_w2, LANES).astype(BF16),
        rwkv_a0.reshape(1, w), jnp.pad(rwkv_a2, ((DECAY_LORA, 0), (0, 0))).astype(BF16),
        _pad_rows(rwkv_g2, RW_COLS - OFF_GDN).astype(BF16), rwkv_k_k.reshape(1, w), rwkv_k_a.reshape(1, w),
        rwkv_r_k.reshape(1, w), rwkv_lnx_g.reshape(1, w), rwkv_lnx_b.reshape(1, w), e, prec)
    wo = w_out.astype(BF16)
    return _out_ffn(o_fox, o_rwkv, x, mod, norm2_g.reshape(1, d), wo[0:FOX_WIDTH], wo[FOX_WIDTH:],
                    ffn_w_gate.astype(BF16), ffn_w_up.astype(BF16), ffn_w_down.astype(BF16),
                    tm)


def kernel(x, c, ada_w, ada_b, norm1_g, norm2_g, w_in, fox_f_bias, fox_q_gain, fox_k_gain, rwkv_mu,
           rwkv_w0, rwkv_w2, rwkv_a0, rwkv_a2, rwkv_g2, rwkv_k_k, rwkv_k_a, rwkv_r_k, rwkv_lnx_g,
           rwkv_lnx_b, w_out, ffn_w_gate, ffn_w_up, ffn_w_down):
    bsz, seq, d = x.shape
    depth = ada_w.shape[0]
    tm = min(ROW_TILE, seq)
    tk = min(KEY_TILE, seq)
    tq = min(QUERY_TILE, seq)
    assert seq % tm == 0 and seq % tq == 0 and tq % tk == 0 and seq % CHUNK == 0
    for l in range(depth):
        mod = _ada(c, ada_w, ada_b, l).reshape(bsz, 6, d)
        x = _layer(x, mod, norm1_g[l], norm2_g[l], w_in[l], fox_f_bias[l], fox_q_gain[l],
                   fox_k_gain[l], rwkv_mu[l], rwkv_w0[l], rwkv_w2[l], rwkv_a0[l], rwkv_a2[l],
                   rwkv_g2[l], rwkv_k_k[l], rwkv_k_a[l], rwkv_r_k[l], rwkv_lnx_g[l],
                   rwkv_lnx_b[l], w_out[l], ffn_w_gate[l], ffn_w_up[l], ffn_w_down[l],
                   tm=tm, tq=tq, tk=tk, prec=RWKV_PASSES)
    return x
```

```python
import functools

import jax
import jax.numpy as jnp
from jax import lax
from jax.experimental import pallas as pl
from jax.experimental.pallas import tpu as pltpu

F32 = jnp.float32
BF16 = jnp.bfloat16

HEAD_DIM = 64
FOX_HEADS = 8
RWKV_HEADS = 8
FOX_WIDTH = FOX_HEADS * HEAD_DIM
RWKV_WIDTH = RWKV_HEADS * HEAD_DIM
DECAY_LORA = 64
A_LORA = 64
GATE_LORA = 160
RMS_EPS = 1e-6
GN_EPS = 64e-5
L2_EPS = 1e-12

LANES = 128
MXU_DIM = 256
PAIRS = FOX_HEADS // 2
RWKV_SEQS = 8
INV_BLOCK = 64
RW_COLS = 1920
OFF_LORA, OFF_GDN = 1536, 1664
CHUNK = 128
NEG_BIG = -1e30
RWKV_PASSES = (1, 1, 1)
FOX_BOUND_MAX = 30.0
ROW_TILE = 512
QUERY_TILE = 512
KEY_TILE = 256
VMEM_LIMIT = 56 * 1024 * 1024


def _split_bf16(x, n):
    if x.dtype == BF16:
        return [x]
    parts = []
    r = x
    for i in range(n):
        p = r.astype(BF16)
        parts.append(p)
        if i < n - 1:
            r = r - p.astype(F32)
    return parts


_NN = (((1,), (0,)), ((), ()))
_NT = (((1,), (1,)), ((), ()))
_TN = (((0,), (0,)), ((), ()))


def _mm(a, b, pa=1, pb=1, dims=_NN):
    a_parts = _split_bf16(a, pa)
    b_parts = _split_bf16(b, pb)
    order = max(len(a_parts), len(b_parts))
    out = None
    for i, ai in enumerate(a_parts):
        for j, bj in enumerate(b_parts):
            if i + j >= order:
                continue
            t = lax.dot_general(ai, bj, dims, preferred_element_type=F32)
            out = t if out is None else out + t
    return out


def _log_sigmoid(z):
    return jnp.minimum(z, 0.0) - jnp.log(1.0 + jnp.exp(-jnp.abs(z)))


def _sigmoid(z):
    return 1.0 / (1.0 + jnp.exp(-z))


def _ada_kernel(c_ref, w_ref, b_ref, o_ref):
    c = c_ref[...]
    cond = c * _sigmoid(c)
    o_ref[...] = _mm(cond, w_ref[0], 2, 2) + b_ref[0]


def _ada(c, ada_w, ada_b, layer):
    bsz, d = c.shape
    depth, _, n = ada_w.shape
    tn = 512
    return pl.pallas_call(
        _ada_kernel,
        grid=(n // tn,),
        in_specs=[
            pl.BlockSpec((bsz, d), lambda j: (0, 0)),
            pl.BlockSpec((1, d, tn), lambda j: (layer, 0, j)),
            pl.BlockSpec((1, 1, tn), lambda j: (layer, 0, j)),
        ],
        out_specs=pl.BlockSpec((bsz, tn), lambda j: (0, j)),
        out_shape=jax.ShapeDtypeStruct((bsz, n), F32),
        compiler_params=pltpu.CompilerParams(dimension_semantics=("arbitrary",)),
        name="ada",
    )(c, ada_w, ada_b.reshape(depth, 1, n))


def _in_proj_kernel(x_ref, mod_ref, g1_ref, wqkv_ref, wf_ref, wrw_ref,
                    fbrow_ref, qg_ref, kg_ref, e_ref, trilo_ref, mu_ref,
                    q_ref, k_ref, v_ref, ccol_ref, crow_ref, rw_ref,
                    carry_row, prev_ref, *, tm, tk):
    t = pl.program_id(1)

    @pl.when(t == 0)
    def _():
        carry_row[...] = jnp.zeros_like(carry_row)
        prev_ref[...] = jnp.zeros_like(prev_ref)

    x = x_ref[0]
    mod = mod_ref[0]
    sh1 = mod[0:1, :]
    sc1 = mod[1:2, :]
    ms = jnp.mean(x * x, axis=-1, keepdims=True)
    y = x * lax.rsqrt(ms + RMS_EPS) * g1_ref[...]
    hb = (y * (1.0 + sc1) + sh1).astype(BF16)

    qkv = _mm(hb, wqkv_ref[...])
    e = e_ref[...]
    q = qkv[:, 0:FOX_WIDTH]
    k = qkv[:, FOX_WIDTH:2 * FOX_WIDTH]
    qms = _head_sums(q * q, e, 1) * (1.0 / HEAD_DIM)
    kms = _head_sums(k * k, e, 1) * (1.0 / HEAD_DIM)
    q_ref[0] = (q * lax.rsqrt(qms + RMS_EPS) * qg_ref[...] * (HEAD_DIM ** -0.5)).astype(BF16)
    k_ref[0] = (k * lax.rsqrt(kms + RMS_EPS) * kg_ref[...]).astype(BF16)
    v_ref[0] = qkv[:, 2 * FOX_WIDTH:3 * FOX_WIDTH].astype(BF16)

    rw = _mm(hb, wrw_ref[...])
    rolled = pltpu.roll(rw, 1, 0)
    first_row = lax.broadcasted_iota(jnp.int32, (8, 1), 0) == 0
    top = jnp.where(first_row, prev_ref[0:1, :], rolled[0:8])
    prev_ref[0:1, :] = rw[tm - 1:tm, :]
    rw_ref[0] = rw + (jnp.concatenate([top, rolled[8:tm]], axis=0) - rw) * mu_ref[...]

    lf = _log_sigmoid(_mm(hb, wf_ref[...]) + fbrow_ref[...])
    blocks = []
    carry = carry_row[...]
    for j in range(tm // LANES):
        cj = _mm(trilo_ref[...], lf[j * LANES:(j + 1) * LANES], 1, 3) + carry
        blocks.append(cj)
        carry = cj[LANES - 1:LANES, :]
    carry_row[...] = carry
    ccol_ref[0] = jnp.concatenate(blocks, axis=0)
    for j, cj in enumerate(blocks):
        jj, off = divmod(j * LANES, tk)
        crow_ref[0, jj, :, off:off + LANES] = cj.T[0:16, :]


def _in_proj(x, mod, g1, wqkv, wf, wrw, fbrow, qg, kg, e, mu, tm, tk):
    bsz, seq, d = x.shape
    nt = seq // tm
    row = lax.broadcasted_iota(jnp.int32, (LANES, LANES), 0)
    col = lax.broadcasted_iota(jnp.int32, (LANES, LANES), 1)
    trilo = (col <= row).astype(BF16)
    const = lambda shape: pl.BlockSpec(shape, lambda b, t: (0,) * len(shape))
    kern = functools.partial(_in_proj_kernel, tm=tm, tk=tk)
    return pl.pallas_call(
        kern,
        grid=(bsz, nt),
        in_specs=[
            pl.BlockSpec((1, tm, d), lambda b, t: (b, t, 0)),
            pl.BlockSpec((1, 6, d), lambda b, t: (b, 0, 0)),
            const((1, d)),
            pl.BlockSpec((d, 3 * FOX_WIDTH), lambda b, t: (0, 0)),
            pl.BlockSpec((d, LANES), lambda b, t: (0, 3 * FOX_WIDTH // LANES)),
            const(wrw.shape),
            const(fbrow.shape), const(qg.shape), const(kg.shape),
            const(e.shape), const(trilo.shape), const(mu.shape),
        ],
        out_specs=[
            pl.BlockSpec((1, tm, FOX_WIDTH), lambda b, t: (b, t, 0)),
            pl.BlockSpec((1, tm, FOX_WIDTH), lambda b, t: (b, t, 0)),
            pl.BlockSpec((1, tm, FOX_WIDTH), lambda b, t: (b, t, 0)),
            pl.BlockSpec((1, tm, LANES), lambda b, t: (b, t, 0)),
            pl.BlockSpec((1, tm // tk, 16, tk), lambda b, t: (b, t, 0, 0)),
            pl.BlockSpec((1, tm, RW_COLS), lambda b, t: (b, t, 0)),
        ],
        out_shape=[
            jax.ShapeDtypeStruct((bsz, seq, FOX_WIDTH), BF16),
            jax.ShapeDtypeStruct((bsz, seq, FOX_WIDTH), BF16),
            jax.ShapeDtypeStruct((bsz, seq, FOX_WIDTH), BF16),
            jax.ShapeDtypeStruct((bsz, seq, LANES), F32),
            jax.ShapeDtypeStruct((bsz, seq // tk, 16, tk), F32),
            jax.ShapeDtypeStruct((bsz, seq, RW_COLS), F32),
        ],
        scratch_shapes=[pltpu.VMEM((1, LANES), F32), pltpu.VMEM((8, RW_COLS), F32)],
        compiler_params=pltpu.CompilerParams(
            dimension_semantics=("arbitrary", "arbitrary"), vmem_limit_bytes=VMEM_LIMIT),
        name="in_proj",
    )(x, mod, g1, wqkv, wf, wrw, fbrow, qg, kg, e, trilo, mu)


def _fox_bounded(bound, q_ref, k_ref, v_ref, ccol_ref, crow_ref, o_ref, acc_ref, qm_ref, cb_ref,
                 *, tq, tk):
    i = pl.program_id(1)
    nsub = tq // tk
    lane = lax.broadcasted_iota(jnp.int32, (tq, LANES), 1)
    low = lane < HEAD_DIM
    klow = lax.broadcasted_iota(jnp.int32, (tk, LANES), 1) < HEAD_DIM
    one_lo = jnp.where(klow, 1.0, 0.0).astype(BF16)
    one_hi = jnp.where(klow, 0.0, 1.0).astype(BF16)

    ccol = ccol_ref[0] - bound
    for hp in range(PAIRS):
        q2 = q_ref[0, :, LANES * hp:LANES * (hp + 1)]
        zero = jnp.zeros_like(q2)
        qm_ref[2 * hp] = jnp.where(low, q2, zero)
        qm_ref[2 * hp + 1] = jnp.where(low, zero, q2)
        for hh in range(2):
            h = 2 * hp + hh
            cb_ref[h] = jnp.broadcast_to(ccol[:, h:h + 1], (tq, LANES))
    acc_ref[...] = jnp.zeros_like(acc_ref)

    def step(j, sub):
        r0 = 0 if sub is None else sub * tk
        rows = tq - r0
        ks = pl.multiple_of(j * tk, tk)
        scores = []
        for hp in range(PAIRS):
            k2 = k_ref[0, pl.ds(ks, tk), LANES * hp:LANES * (hp + 1)]
            for hh in range(2):
                scores.append(_mm(qm_ref[2 * hp + hh, r0:tq, :], k2, dims=_NT))
        if sub is not None:
            causal = (lax.broadcasted_iota(jnp.int32, (rows, tk), 1)
                      <= lax.broadcasted_iota(jnp.int32, (rows, tk), 0))
        probs = []
        for h in range(FOX_HEADS):
            cb = cb_ref[h, r0:tq, :]
            bias = jnp.concatenate([cb] * (tk // LANES), axis=1) - crow_ref[0, j, h:h + 1, :]
            s = scores[h] + bias
            if sub is not None:
                s = jnp.where(causal, s, NEG_BIG)
            probs.append(jnp.exp(s).astype(BF16))
        for hp in range(PAIRS):
            v2 = v_ref[0, pl.ds(ks, tk), LANES * hp:LANES * (hp + 1)]
            zero = jnp.zeros_like(v2)
            vaug = jnp.concatenate([
                jnp.concatenate([jnp.where(klow, v2, zero), one_lo], axis=1),
                jnp.concatenate([jnp.where(klow, zero, v2), one_hi], axis=1)], axis=0)
            acc_ref[hp, r0:tq, :] += _mm(
                jnp.concatenate([probs[2 * hp], probs[2 * hp + 1]], axis=1), vaug)

    def body(jo, carry):
        for m in range(nsub):
            step(jo * nsub + m, None)
        return carry

    lax.fori_loop(0, i, body, 0)
    for m in range(nsub):
        step(i * nsub + m, m)
    for hp in range(PAIRS):
        a = acc_ref[hp]
        o = a[:, 0:LANES] / a[:, LANES:2 * LANES]
        o_ref[0, :, LANES * hp:LANES * (hp + 1)] = o.astype(BF16)


def _fox_kernel(flag_ref, bound_ref, q_ref, k_ref, v_ref, ccol_ref, crow_ref, o_ref, acc_ref,
                qm_ref, cb_ref, *, tq, tk):
    @pl.when(flag_ref[0] == 1)
    def _():
        _fox_bounded(bound_ref[0], q_ref, k_ref, v_ref, ccol_ref, crow_ref, o_ref, acc_ref,
                     qm_ref, cb_ref, tq=tq, tk=tk)

    @pl.when(flag_ref[0] == 0)
    def _():
        _fox_running_max(q_ref, k_ref, v_ref, ccol_ref, crow_ref, o_ref, tq=tq, tk=tk)


def _fox_running_max(q_ref, k_ref, v_ref, ccol_ref, crow_ref, o_ref, *, tq, tk):
    i = pl.program_id(1)
    nsub = tq // tk
    lane = lax.broadcasted_iota(jnp.int32, (tq, LANES), 1)
    low = lane < HEAD_DIM
    row = lax.broadcasted_iota(jnp.int32, (tq, tk), 0)
    col = lax.broadcasted_iota(jnp.int32, (tq, tk), 1)
    ccol = ccol_ref[0]

    for hp in range(PAIRS):
        ls = slice(LANES * hp, LANES * (hp + 1))
        q2 = q_ref[0, :, ls]
        zero = jnp.zeros_like(q2)
        qm = (jnp.where(low, q2, zero), jnp.where(low, zero, q2))
        cc = tuple(ccol[:, 2 * hp + hh:2 * hp + hh + 1] for hh in range(2))

        def step(j, carry, sub, ls=ls, qm=qm, cc=cc, hp=hp):
            ks = pl.multiple_of(j * tk, tk)
            k2 = k_ref[0, pl.ds(ks, tk), ls]
            v2 = v_ref[0, pl.ds(ks, tk), ls]
            new = []
            for hh in range(2):
                m, l, acc = carry[hh]
                s = _mm(qm[hh], k2, dims=_NT)
                cr = crow_ref[0, j, 2 * hp + hh:2 * hp + hh + 1, :]
                s = s + (cc[hh] - cr)
                if sub is not None:
                    s = jnp.where(col + sub * tk <= row, s, NEG_BIG)
                m_new = jnp.maximum(m, jnp.max(s, axis=1, keepdims=True))
                alpha = jnp.exp(m - m_new)
                p = jnp.exp(s - m_new)
                l_new = alpha * l + jnp.sum(p, axis=1, keepdims=True)
                acc_new = alpha * acc + _mm(p.astype(BF16), v2)
                new.append((m_new, l_new, acc_new))
            return tuple(new)

        init_one = (jnp.full((tq, 1), NEG_BIG, F32), jnp.zeros((tq, 1), F32),
                    jnp.zeros((tq, LANES), F32))
        carry = lax.fori_loop(0, i * nsub, lambda j, c: step(j, c, None), (init_one, init_one))
        for sub in range(nsub):
            carry = step(i * nsub + sub, carry, sub)
        o0 = carry[0][2] / carry[0][1]
        o1 = carry[1][2] / carry[1][1]
        o_ref[0, :, ls] = jnp.where(low, o0, o1).astype(BF16)


def _fox(flag, bound, q, k, v, ccol, crow, tq, tk):
    bsz, seq, _ = q.shape
    kern = functools.partial(_fox_kernel, tq=tq, tk=tk)
    return pl.pallas_call(
        kern,
        grid=(bsz, seq // tq),
        in_specs=[
            pl.BlockSpec(memory_space=pltpu.SMEM),
            pl.BlockSpec(memory_space=pltpu.SMEM),
            pl.BlockSpec((1, tq, FOX_WIDTH), lambda b, i: (b, i, 0)),
            pl.BlockSpec((1, seq, FOX_WIDTH), lambda b, i: (b, 0, 0)),
            pl.BlockSpec((1, seq, FOX_WIDTH), lambda b, i: (b, 0, 0)),
            pl.BlockSpec((1, tq, LANES), lambda b, i: (b, i, 0)),
            pl.BlockSpec((1, seq // tk, 16, tk), lambda b, i: (b, 0, 0, 0)),
        ],
        out_specs=pl.BlockSpec((1, tq, FOX_WIDTH), lambda b, i: (b, i, 0)),
        out_shape=jax.ShapeDtypeStruct((bsz, seq, FOX_WIDTH), BF16),
        scratch_shapes=[pltpu.VMEM((PAIRS, tq, 2 * LANES), F32),
                        pltpu.VMEM((FOX_HEADS, tq, LANES), BF16),
                        pltpu.VMEM((FOX_HEADS, tq, LANES), F32)],
        compiler_params=pltpu.CompilerParams(
            dimension_semantics=("arbitrary", "arbitrary"), vmem_limit_bytes=VMEM_LIMIT),
        name="fox",
    )(flag, bound, q, k, v, ccol, crow)


def _unit_tri_inverses(mats, bs, passes):
    c, w = mats[0].shape
    n = range(len(mats))
    lane_cache = {}

    def lane_ids(s):
        if s not in lane_cache:
            lane_cache[s] = lax.broadcasted_iota(jnp.int32, (s, w), 1)
        return lane_cache[s]

    def terms(x):
        return _split_bf16(x, passes)

    def dot_terms(a_t, b_t):
        out = None
        for i, ai in enumerate(a_t):
            for j, bj in enumerate(b_t):
                if i + j < max(len(a_t), len(b_t)):
                    t = lax.dot_general(ai, bj, _NN, preferred_element_type=F32)
                    out = t if out is None else out + t
        return out

    def block_rows(x, s, offset):
        lb = jnp.right_shift(lane_ids(s), s.bit_length() - 1)
        zero = jnp.zeros((s, w), BF16)
        keep = [lb == j for j in range(w // s)]
        out = []
        for t in terms(x):
            rows = []
            for j in range(w // s):
                if offset and j % 2 == 0:
                    rows.append(zero)
                else:
                    rows.append(jnp.where(keep[j - offset], t, zero))
            out.append(jnp.concatenate(rows, axis=0))
        return out

    def mm(a, b_terms):
        return dot_terms(terms(a), b_terms)

    s = bs
    sh = s.bit_length() - 1
    in_mat = jnp.bitwise_and(lane_ids(s), c - 1)
    q = []
    for a2 in mats:
        d = a2[0:s, :]
        for r in range(1, c // s):
            d = jnp.where(jnp.right_shift(in_mat, sh) == r, a2[r * s:(r + 1) * s, :], d)
        q.append(d)
    p = [mm(q[i], block_rows(q[i], s, 0)) for i in n]
    for _ in range(s.bit_length() - 3):
        both = [mm(jnp.concatenate([q[i], p[i]], axis=0), block_rows(p[i], s, 0)) for i in n]
        q = [q[i] + p[i] + both[i][0:s] for i in n]
        p = [both[i][s:2 * s] for i in n]
    q = [q[i] + p[i] + mm(q[i], block_rows(p[i], s, 0)) for i in n]

    while s < c:
        sh = s.bit_length() - 1
        lane = lane_ids(s)
        first = jnp.bitwise_and(lane, s) == 0
        pair_id = jnp.right_shift(jnp.bitwise_and(lane, c - 1), sh + 1)
        l21 = []
        for a2 in mats:
            z = jnp.zeros((s, w), F32)
            for m in range(c // (2 * s)):
                rows = a2[(2 * m + 1) * s:(2 * m + 2) * s, :]
                z = jnp.where(pair_id == m, jnp.where(first, rows, 0.0), z)
            l21.append(z)
        x = [l21[i] + mm(l21[i], block_rows(q[i], s, 0)) for i in n]
        t21 = [x[i] + mm(q[i], block_rows(x[i], s, 1)) for i in n]
        q = [jnp.concatenate([jnp.where(first, q[i], 0.0), jnp.where(first, t21[i], q[i])], axis=0)
             for i in n]
        s *= 2
    return q


def _head_sums(x, e, pa):
    wd = e.shape[0]
    return jnp.concatenate(
        [_mm(x[:, j:j + wd], e, pa, 1) for j in range(0, x.shape[1], wd)], axis=1)


def _rwkv_kernel(rw_ref, w0_ref, w2_ref, a0_ref, a2_ref, g2_ref, kk_ref, ka_ref,
                 rk_ref, lg_ref, lb_ref, e_ref, tri_ref, o_ref, s_ref, *, prec, nb):
    cch = CHUNK
    rows = nb * cch
    w = RWKV_WIDTH
    c = pl.program_id(1)
    pg, pd, ps = prec
    e = e_ref[...]

    @pl.when(c == 0)
    def _():
        s_ref[...] = jnp.zeros_like(s_ref)

    lane = lax.broadcasted_iota(jnp.int32, (cch, LANES), 1)
    low = lane < HEAD_DIM
    ri = lax.broadcasted_iota(jnp.int32, (cch, cch), 0)
    ci = lax.broadcasted_iota(jnp.int32, (cch, cch), 1)
    strict = ci < ri
    incl = ci <= ri
    blockdiag = (ri < HEAD_DIM) == (ci < HEAD_DIM)
    sl = [slice(LANES * hp, LANES * (hp + 1)) for hp in range(PAIRS)]
    rs = [slice(cch * i, cch * (i + 1)) for i in range(nb)]
    combos = [(i, hp) for i in range(nb) for hp in range(PAIRS)]
    n = range(len(combos))

    def halves(z):
        zero = jnp.zeros_like(z)
        return jnp.concatenate([jnp.where(low, z, zero), jnp.where(low, zero, z)], axis=0)

    def shifted(lo, hi):
        return rw_ref[:, :, lo:hi].reshape(rows, hi - lo)

    wa = shifted(OFF_LORA, OFF_GDN)
    gdn = shifted(OFF_GDN, RW_COLS)
    wlog = _log_sigmoid(w0_ref[...] + _mm(jnp.tanh(wa), w2_ref[...])) - 0.5
    logd = -jnp.exp(wlog)
    a = _sigmoid(a0_ref[...] + _mm(wa, a2_ref[...]))
    g = _mm(_sigmoid(gdn), g2_ref[...])
    tri = tri_ref[...]
    cum = jnp.concatenate([_mm(tri, logd[rs[i]], 1, 2) for i in range(nb)], axis=0)
    lasts = [cum[cch * (i + 1) - 1:cch * (i + 1), :] for i in range(nb)]
    clast = jnp.concatenate([jnp.broadcast_to(z, (cch, w)) for z in lasts], axis=0)
    pc = [jnp.exp(z) for z in lasts]
    k = shifted(w, 2 * w)
    kk = k * kk_ref[...]
    kk = kk * lax.rsqrt(jnp.maximum(_head_sums(kk * kk, e, 2), L2_EPS * L2_EPS))
    k = k * (1.0 + (a - 1.0) * ka_ref[...])
    avec = -kk
    bvec = kk * a
    einv = jnp.exp(-cum)
    at = (avec * jnp.exp(cum - logd)).astype(BF16)
    bt = (bvec * einv).astype(BF16)
    kt = (k * einv).astype(BF16)
    etail = jnp.exp(clast - cum)
    bh = (bvec * etail).astype(BF16)
    kh = (k * etail).astype(BF16)
    r = shifted(0, w)
    rt = (r * jnp.exp(cum)).astype(BF16)
    bonus = _head_sums(r * k * rk_ref[...], e, 1)
    v = shifted(2 * w, 3 * w)
    vb = v.astype(BF16)
    bv = bonus * v

    def blk(z, j):
        i, hp = combos[j]
        return z[rs[i], sl[hp]]

    aab, aak, arbk = [], [], []
    for j in n:
        rb = jnp.concatenate([blk(bt, j), blk(kt, j)], axis=0)
        blocks = []
        for hh in range(2):
            sel = low if hh == 0 else jnp.logical_not(low)
            zero = jnp.zeros((cch, LANES), BF16)
            la = jnp.concatenate([jnp.where(sel, blk(at, j), zero),
                                  jnp.where(sel, blk(rt, j), zero)], axis=0)
            blocks.append(_mm(la, rb, pg, pg, dims=_NT))
        aab.append(jnp.concatenate(
            [jnp.where(strict, gm[0:cch, 0:cch], 0.0) for gm in blocks], axis=1))
        aak.append(jnp.concatenate(
            [jnp.where(strict, gm[0:cch, cch:2 * cch], 0.0) for gm in blocks], axis=1))
        arbk.append(jnp.concatenate(
            [jnp.where(incl, gm[cch:2 * cch, 0:cch], 0.0) for gm in blocks]
            + [jnp.where(incl, gm[cch:2 * cch, cch:2 * cch], 0.0) for gm in blocks], axis=1))
    qm = _unit_tri_inverses(aab, INV_BLOCK, pd)

    sp = [s_ref[i, hp] for i, hp in combos]
    vst = [halves(blk(vb, j)) for j in n]
    rhs = [_mm(blk(at, j), sp[j], ps, ps, dims=_NT) + _mm(aak[j], vst[j], ps, ps) for j in n]
    u = [rhs[j] + _mm(qm[j], halves(rhs[j]), ps, ps) for j in n]
    ys = [_mm(blk(rt, j), sp[j], ps, ps, dims=_NT)
          + _mm(arbk[j], jnp.concatenate([halves(u[j]).astype(BF16), vst[j]], axis=0), ps, ps)
          for j in n]
    for j in n:
        i, hp = combos[j]
        uv = jnp.concatenate([u[j], blk(vb, j).astype(F32)], axis=0)
        bk = jnp.concatenate([blk(bh, j), blk(kh, j)], axis=0)
        upd = _mm(uv, bk, ps, ps, dims=_TN)
        s_ref[i, hp] = sp[j] * pc[i][:, sl[hp]] + jnp.where(blockdiag, upd, 0.0)

    y = jnp.concatenate(
        [jnp.concatenate(ys[PAIRS * i:PAIRS * (i + 1)], axis=1) for i in range(nb)], axis=0)
    inv_n = 1.0 / HEAD_DIM
    mean = _head_sums(y, e, 1) * inv_n
    d = y - mean
    var = _head_sums(d * d, e, 1) * inv_n
    yn = d * lax.rsqrt(var + GN_EPS) * lg_ref[...] + lb_ref[...]
    o_ref[...] = ((yn + bv) * g).astype(BF16).reshape(nb, cch, w)


def _rwkv(rw, w0, w2, a0, a2, g2, k_k, k_a, r_k, lnx_g, lnx_b, e, prec):
    bsz, seq, _ = rw.shape
    cch = CHUNK
    nb = RWKV_SEQS if bsz % RWKV_SEQS == 0 else 1
    row = lax.broadcasted_iota(jnp.int32, (cch, cch), 0)
    col = lax.broadcasted_iota(jnp.int32, (cch, cch), 1)
    tri = (col <= row).astype(BF16)
    const = lambda shape: pl.BlockSpec(shape, lambda b, t: (0,) * len(shape))
    args = (w0, w2, a0, a2, g2, k_k, k_a, r_k, lnx_g, lnx_b, e, tri)
    return pl.pallas_call(
        functools.partial(_rwkv_kernel, prec=prec, nb=nb),
        grid=(bsz // nb, seq // cch),
        in_specs=[pl.BlockSpec((nb, cch, RW_COLS), lambda b, t: (b, t, 0))]
        + [const(a.shape) for a in args],
        out_specs=pl.BlockSpec((nb, cch, RWKV_WIDTH), lambda b, t: (b, t, 0)),
        out_shape=jax.ShapeDtypeStruct((bsz, seq, RWKV_WIDTH), BF16),
        scratch_shapes=[pltpu.VMEM((nb, PAIRS, LANES, LANES), F32)],
        compiler_params=pltpu.CompilerParams(
            dimension_semantics=("arbitrary", "arbitrary"), vmem_limit_bytes=VMEM_LIMIT),
        name="rwkv",
    )(rw, *args)


def _out_ffn_kernel(of_ref, or_ref, x_ref, mod_ref, g2_ref, wt_ref, wb_ref, wg_ref, wu_ref, wd_ref,
                    o_ref, *, slabs):
    mod = mod_ref[0]
    gt1, sh2, sc2, gt2 = mod[2:3, :], mod[3:4, :], mod[4:5, :], mod[5:6, :]
    mix = _mm(of_ref[0], wt_ref[...]) + _mm(or_ref[0], wb_ref[...])
    x1 = x_ref[0] + gt1 * mix
    ms = jnp.mean(x1 * x1, axis=-1, keepdims=True)
    y = x1 * lax.rsqrt(ms + RMS_EPS) * g2_ref[...]
    h2 = (y * (1.0 + sc2) + sh2).astype(BF16)
    acc = None
    for lo, hi in slabs:
        gate = _mm(h2, wg_ref[:, lo:hi])
        up = _mm(h2, wu_ref[:, lo:hi])
        act = (gate * _sigmoid(gate) * up).astype(BF16)
        part = _mm(act, wd_ref[lo:hi, :])
        acc = part if acc is None else acc + part
    o_ref[0] = x1 + gt2 * acc


def _out_ffn(o_fox, o_rwkv, x, mod, g2, w_top, w_bot, wg, wu, wd, tm):
    bsz, seq, d = x.shape
    dff = wg.shape[1]
    tiles = dff // MXU_DIM if dff % MXU_DIM == 0 else 1
    cut = (tiles // 2) * (dff // tiles)
    slabs = ((0, cut), (cut, dff)) if cut else ((0, dff),)
    resident = lambda shape: pl.BlockSpec(shape, lambda b, t: (0,) * len(shape),
                                          pipeline_mode=pl.Buffered(1))
    return pl.pallas_call(
        functools.partial(_out_ffn_kernel, slabs=slabs),
        grid=(bsz, seq // tm),
        in_specs=[
            pl.BlockSpec((1, tm, FOX_WIDTH), lambda b, t: (b, t, 0)),
            pl.BlockSpec((1, tm, RWKV_WIDTH), lambda b, t: (b, t, 0)),
            pl.BlockSpec((1, tm, d), lambda b, t: (b, t, 0)),
            pl.BlockSpec((1, 6, d), lambda b, t: (b, 0, 0)),
            resident((1, d)), resident(w_top.shape), resident(w_bot.shape),
            resident(wg.shape), resident(wu.shape), resident(wd.shape),
        ],
        out_specs=pl.BlockSpec((1, tm, d), lambda b, t: (b, t, 0)),
        out_shape=jax.ShapeDtypeStruct((bsz, seq, d), F32),
        compiler_params=pltpu.CompilerParams(
            dimension_semantics=("arbitrary", "arbitrary"), vmem_limit_bytes=VMEM_LIMIT),
        name="out_ffn",
    )(o_fox, o_rwkv, x, mod, g2, w_top, w_bot, wg, wu, wd)


def _pad_cols(w, n):
    return jnp.pad(w, ((0, 0), (0, n - w.shape[1])))


def _pad_rows(w, n):
    return jnp.pad(w, ((0, n - w.shape[0]), (0, 0)))


def _layer(x, mod, norm1_g, norm2_g, w_in, fox_f_bias, fox_q_gain, fox_k_gain, rwkv_mu, rwkv_w0,
           rwkv_w2, rwkv_a0, rwkv_a2, rwkv_g2, rwkv_k_k, rwkv_k_a, rwkv_r_k, rwkv_lnx_g,
           rwkv_lnx_b, w_out, ffn_w_gate, ffn_w_up, ffn_w_down, *, tm, tq, tk, prec):
    bsz, seq, d = x.shape
    w = RWKV_WIDTH
    nfox = 3 * FOX_WIDTH + FOX_HEADS

    w_bf = w_in.astype(BF16)
    wqkv = wf = w_bf
    wr = w_in[:, nfox:]
    assert 3 * w == OFF_LORA and DECAY_LORA + A_LORA == OFF_GDN - OFF_LORA
    wrw = _pad_cols(wr, RW_COLS).astype(BF16)
    mu_p = _pad_cols(rwkv_mu.reshape(1, -1), RW_COLS)
    fbrow = _pad_cols(fox_f_bias.reshape(1, -1), LANES)
    qg = jnp.tile(fox_q_gain, (1, 1)).reshape(1, FOX_WIDTH)
    kg = fox_k_gain.reshape(1, FOX_WIDTH)
    hi = lax.broadcasted_iota(jnp.int32, (MXU_DIM, MXU_DIM), 0) // HEAD_DIM
    hj = lax.broadcasted_iota(jnp.int32, (MXU_DIM, MXU_DIM), 1) // HEAD_DIM
    e = (hi == hj).astype(BF16)

    q, k, v, ccol, crow, rw = _in_proj(
        x, mod, norm1_g.reshape(1, d), wqkv, wf, wrw, fbrow, qg, kg, e, mu_p, tm, tk)
    bound = (1.05 * HEAD_DIM ** 0.5) * jnp.max(jnp.abs(fox_q_gain)) * jnp.max(jnp.abs(fox_k_gain))
    flag = (bound <= FOX_BOUND_MAX).astype(jnp.int32)
    o_fox = _fox(flag.reshape(1), bound.astype(F32).reshape(1), q, k, v, ccol, crow, tq, tk)
    o_rwkv = _rwkv(
        rw, rwkv_w0.reshape(1, w), _pad_rows(rwkv_w2, LANES).astype(BF16),
        rwkv_a0.reshape(1, w), jnp.pad(rwkv_a2, ((DECAY_LORA, 0), (0, 0))).astype(BF16),
        _pad_rows(rwkv_g2, RW_COLS - OFF_GDN).astype(BF16), rwkv_k_k.reshape(1, w), rwkv_k_a.reshape(1, w),
        rwkv_r_k.reshape(1, w), rwkv_lnx_g.reshape(1, w), rwkv_lnx_b.reshape(1, w), e, prec)
    wo = w_out.astype(BF16)
    return _out_ffn(o_fox, o_rwkv, x, mod, norm2_g.reshape(1, d), wo[0:FOX_WIDTH], wo[FOX_WIDTH:],
                    ffn_w_gate.astype(BF16), ffn_w_up.astype(BF16), ffn_w_down.astype(BF16),
                    tm)


def kernel(x, c, ada_w, ada_b, norm1_g, norm2_g, w_in, fox_f_bias, fox_q_gain, fox_k_gain, rwkv_mu,
           rwkv_w0, rwkv_w2, rwkv_a0, rwkv_a2, rwkv_g2, rwkv_k_k, rwkv_k_a, rwkv_r_k, rwkv_lnx_g,
           rwkv_lnx_b, w_out, ffn_w_gate, ffn_w_up, ffn_w_down):
    bsz, seq, d = x.shape
    depth = ada_w.shape[0]
    tm = min(ROW_TILE, seq)
    tk = min(KEY_TILE, seq)
    tq = min(QUERY_TILE, seq)
    assert seq % tm == 0 and seq % tq == 0 and tq % tk == 0 and seq % CHUNK == 0
    for l in range(depth):
        mod = _ada(c, ada_w, ada_b, l).reshape(bsz, 6, d)
        x = _layer(x, mod, norm1_g[l], norm2_g[l], w_in[l], fox_f_bias[l], fox_q_gain[l],
                   fox_k_gain[l], rwkv_mu[l], rwkv_w0[l], rwkv_w2[l], rwkv_a0[l], rwkv_a2[l],
                   rwkv_g2[l], rwkv_k_k[l], rwkv_k_a[l], rwkv_r_k[l], rwkv_lnx_g[l],
                   rwkv_lnx_b[l], w_out[l], ffn_w_gate[l], ffn_w_up[l], ffn_w_down[l],
                   tm=tm, tq=tq, tk=tk, prec=RWKV_PASSES)
    return x
```

```python
import functools

import jax
import jax.numpy as jnp
from jax import lax
from jax.experimental import pallas as pl
from jax.experimental.pallas import tpu as pltpu

F32 = jnp.float32
BF16 = jnp.bfloat16

HEAD_DIM = 64
FOX_HEADS = 8
RWKV_HEADS = 8
FOX_WIDTH = FOX_HEADS * HEAD_DIM
RWKV_WIDTH = RWKV_HEADS * HEAD_DIM
DECAY_LORA = 64
A_LORA = 64
GATE_LORA = 160
RMS_EPS = 1e-6
GN_EPS = 64e-5
L2_EPS = 1e-12

LANES = 128
MXU_DIM = 256
PAIRS = FOX_HEADS // 2
RWKV_SEQS = 8
INV_BLOCK = 64
RW_COLS = 1920
OFF_LORA, OFF_GDN = 1536, 1664
CHUNK = 128
NEG_BIG = -1e30
RWKV_PASSES = (1, 1, 1)
FOX_BOUND_MAX = 30.0
ROW_TILE = 512
QUERY_TILE = 1024
KEY_TILE = 256
VMEM_LIMIT = 56 * 1024 * 1024


def _split_bf16(x, n):
    if x.dtype == BF16:
        return [x]
    parts = []
    r = x
    for i in range(n):
        p = r.astype(BF16)
        parts.append(p)
        if i < n - 1:
            r = r - p.astype(F32)
    return parts


_NN = (((1,), (0,)), ((), ()))
_NT = (((1,), (1,)), ((), ()))
_TN = (((0,), (0,)), ((), ()))


def _mm(a, b, pa=1, pb=1, dims=_NN):
    a_parts = _split_bf16(a, pa)
    b_parts = _split_bf16(b, pb)
    order = max(len(a_parts), len(b_parts))
    out = None
    for i, ai in enumerate(a_parts):
        for j, bj in enumerate(b_parts):
            if i + j >= order:
                continue
            t = lax.dot_general(ai, bj, dims, preferred_element_type=F32)
            out = t if out is None else out + t
    return out


def _log_sigmoid(z):
    return jnp.minimum(z, 0.0) - jnp.log(1.0 + jnp.exp(-jnp.abs(z)))


def _sigmoid(z):
    return 1.0 / (1.0 + jnp.exp(-z))


def _ada_kernel(c_ref, w_ref, b_ref, o_ref):
    c = c_ref[...]
    cond = c * _sigmoid(c)
    o_ref[...] = _mm(cond, w_ref[0], 2, 2) + b_ref[0]


def _ada(c, ada_w, ada_b, layer):
    bsz, d = c.shape
    depth, _, n = ada_w.shape
    tn = 512
    return pl.pallas_call(
        _ada_kernel,
        grid=(n // tn,),
        in_specs=[
            pl.BlockSpec((bsz, d), lambda j: (0, 0)),
            pl.BlockSpec((1, d, tn), lambda j: (layer, 0, j)),
            pl.BlockSpec((1, 1, tn), lambda j: (layer, 0, j)),
        ],
        out_specs=pl.BlockSpec((bsz, tn), lambda j: (0, j)),
        out_shape=jax.ShapeDtypeStruct((bsz, n), F32),
        compiler_params=pltpu.CompilerParams(dimension_semantics=("arbitrary",)),
        name="ada",
    )(c, ada_w, ada_b.reshape(depth, 1, n))


def _in_proj_kernel(x_ref, mod_ref, g1_ref, wqkv_ref, wf_ref, wrw_ref,
                    fbrow_ref, qg_ref, kg_ref, e_ref, trilo_ref, mu_ref,
                    q_ref, k_ref, v_ref, ccol_ref, crow_ref, rw_ref,
                    carry_row, prev_ref, *, tm, tk):
    t = pl.program_id(1)

    @pl.when(t == 0)
    def _():
        carry_row[...] = jnp.zeros_like(carry_row)
        prev_ref[...] = jnp.zeros_like(prev_ref)

    x = x_ref[0]
    mod = mod_ref[0]
    sh1 = mod[0:1, :]
    sc1 = mod[1:2, :]
    ms = jnp.mean(x * x, axis=-1, keepdims=True)
    y = x * lax.rsqrt(ms + RMS_EPS) * g1_ref[...]
    hb = (y * (1.0 + sc1) + sh1).astype(BF16)

    qkv = _mm(hb, wqkv_ref[...])
    e = e_ref[...]
    q = qkv[:, 0:FOX_WIDTH]
    k = qkv[:, FOX_WIDTH:2 * FOX_WIDTH]
    qms = _head_sums(q * q, e, 1) * (1.0 / HEAD_DIM)
    kms = _head_sums(k * k, e, 1) * (1.0 / HEAD_DIM)
    q_ref[0] = (q * lax.rsqrt(qms + RMS_EPS) * qg_ref[...] * (HEAD_DIM ** -0.5)).astype(BF16)
    k_ref[0] = (k * lax.rsqrt(kms + RMS_EPS) * kg_ref[...]).astype(BF16)
    v_ref[0] = qkv[:, 2 * FOX_WIDTH:3 * FOX_WIDTH].astype(BF16)

    rw = _mm(hb, wrw_ref[...])
    rolled = pltpu.roll(rw, 1, 0)
    first_row = lax.broadcasted_iota(jnp.int32, (8, 1), 0) == 0
    top = jnp.where(first_row, prev_ref[0:1, :], rolled[0:8])
    prev_ref[0:1, :] = rw[tm - 1:tm, :]
    rw_ref[0] = rw + (jnp.concatenate([top, rolled[8:tm]], axis=0) - rw) * mu_ref[...]

    lf = _log_sigmoid(_mm(hb, wf_ref[...]) + fbrow_ref[...])
    blocks = []
    carry = carry_row[...]
    for j in range(tm // LANES):
        cj = _mm(trilo_ref[...], lf[j * LANES:(j + 1) * LANES], 1, 3) + carry
        blocks.append(cj)
        carry = cj[LANES - 1:LANES, :]
    carry_row[...] = carry
    ccol_ref[0] = jnp.concatenate(blocks, axis=0)
    for j, cj in enumerate(blocks):
        jj, off = divmod(j * LANES, tk)
        crow_ref[0, jj, :, off:off + LANES] = cj.T[0:16, :]


def _in_proj(x, mod, g1, wqkv, wf, wrw, fbrow, qg, kg, e, mu, tm, tk):
    bsz, seq, d = x.shape
    nt = seq // tm
    row = lax.broadcasted_iota(jnp.int32, (LANES, LANES), 0)
    col = lax.broadcasted_iota(jnp.int32, (LANES, LANES), 1)
    trilo = (col <= row).astype(BF16)
    const = lambda shape: pl.BlockSpec(shape, lambda b, t: (0,) * len(shape))
    kern = functools.partial(_in_proj_kernel, tm=tm, tk=tk)
    return pl.pallas_call(
        kern,
        grid=(bsz, nt),
        in_specs=[
            pl.BlockSpec((1, tm, d), lambda b, t: (b, t, 0)),
            pl.BlockSpec((1, 6, d), lambda b, t: (b, 0, 0)),
            const((1, d)),
            pl.BlockSpec((d, 3 * FOX_WIDTH), lambda b, t: (0, 0)),
            pl.BlockSpec((d, LANES), lambda b, t: (0, 3 * FOX_WIDTH // LANES)),
            const(wrw.shape),
            const(fbrow.shape), const(qg.shape), const(kg.shape),
            const(e.shape), const(trilo.shape), const(mu.shape),
        ],
        out_specs=[
            pl.BlockSpec((1, tm, FOX_WIDTH), lambda b, t: (b, t, 0)),
            pl.BlockSpec((1, tm, FOX_WIDTH), lambda b, t: (b, t, 0)),
            pl.BlockSpec((1, tm, FOX_WIDTH), lambda b, t: (b, t, 0)),
            pl.BlockSpec((1, tm, LANES), lambda b, t: (b, t, 0)),
            pl.BlockSpec((1, tm // tk, 16, tk), lambda b, t: (b, t, 0, 0)),
            pl.BlockSpec((1, tm, RW_COLS), lambda b, t: (b, t, 0)),
        ],
        out_shape=[
            jax.ShapeDtypeStruct((bsz, seq, FOX_WIDTH), BF16),
            jax.ShapeDtypeStruct((bsz, seq, FOX_WIDTH), BF16),
            jax.ShapeDtypeStruct((bsz, seq, FOX_WIDTH), BF16),
            jax.ShapeDtypeStruct((bsz, seq, LANES), F32),
            jax.ShapeDtypeStruct((bsz, seq // tk, 16, tk), F32),
            jax.ShapeDtypeStruct((bsz, seq, RW_COLS), F32),
        ],
        scratch_shapes=[pltpu.VMEM((1, LANES), F32), pltpu.VMEM((8, RW_COLS), F32)],
        compiler_params=pltpu.CompilerParams(
            dimension_semantics=("arbitrary", "arbitrary"), vmem_limit_bytes=VMEM_LIMIT),
        name="in_proj",
    )(x, mod, g1, wqkv, wf, wrw, fbrow, qg, kg, e, trilo, mu)


def _fox_bounded(bound, q_ref, k_ref, v_ref, ccol_ref, crow_ref, o_ref, acc_ref, qm_ref, cb_ref,
                 *, tq, tk):
    i = pl.program_id(1)
    nsub = tq // tk
    lane = lax.broadcasted_iota(jnp.int32, (tq, LANES), 1)
    low = lane < HEAD_DIM
    klow = lax.broadcasted_iota(jnp.int32, (tk, LANES), 1) < HEAD_DIM
    one_lo = jnp.where(klow, 1.0, 0.0).astype(BF16)
    one_hi = jnp.where(klow, 0.0, 1.0).astype(BF16)

    ccol = ccol_ref[0] - bound
    for hp in range(PAIRS):
        q2 = q_ref[0, :, LANES * hp:LANES * (hp + 1)]
        zero = jnp.zeros_like(q2)
        qm_ref[2 * hp] = jnp.where(low, q2, zero)
        qm_ref[2 * hp + 1] = jnp.where(low, zero, q2)
        for hh in range(2):
            h = 2 * hp + hh
            cb_ref[h] = jnp.broadcast_to(ccol[:, h:h + 1], (tq, LANES))
    acc_ref[...] = jnp.zeros_like(acc_ref)

    def step(j, sub):
        r0 = 0 if sub is None else sub * tk
        rows = tq - r0
        ks = pl.multiple_of(j * tk, tk)
        scores = []
        for hp in range(PAIRS):
            k2 = k_ref[0, pl.ds(ks, tk), LANES * hp:LANES * (hp + 1)]
            for hh in range(2):
                scores.append(_mm(qm_ref[2 * hp + hh, r0:tq, :], k2, dims=_NT))
        if sub is not None:
            causal = (lax.broadcasted_iota(jnp.int32, (rows, tk), 1)
                      <= lax.broadcasted_iota(jnp.int32, (rows, tk), 0))
        probs = []
        for h in range(FOX_HEADS):
            cb = cb_ref[h, r0:tq, :]
            bias = jnp.concatenate([cb] * (tk // LANES), axis=1) - crow_ref[0, j, h:h + 1, :]
            s = scores[h] + bias
            if sub is not None:
                s = jnp.where(causal, s, NEG_BIG)
            probs.append(jnp.exp(s).astype(BF16))
        for hp in range(PAIRS):
            v2 = v_ref[0, pl.ds(ks, tk), LANES * hp:LANES * (hp + 1)]
            zero = jnp.zeros_like(v2)
            vaug = jnp.concatenate([
                jnp.concatenate([jnp.where(klow, v2, zero), one_lo], axis=1),
                jnp.concatenate([jnp.where(klow, zero, v2), one_hi], axis=1)], axis=0)
            acc_ref[hp, r0:tq, :] += _mm(
                jnp.concatenate([probs[2 * hp], probs[2 * hp + 1]], axis=1), vaug)

    def body(jo, carry):
        for m in range(nsub):
            step(jo * nsub + m, None)
        return carry

    lax.fori_loop(0, i, body, 0)
    for m in range(nsub):
        step(i * nsub + m, m)
    for hp in range(PAIRS):
        a = acc_ref[hp]
        o = a[:, 0:LANES] / a[:, LANES:2 * LANES]
        o_ref[0, :, LANES * hp:LANES * (hp + 1)] = o.astype(BF16)


def _fox_kernel(flag_ref, bound_ref, q_ref, k_ref, v_ref, ccol_ref, crow_ref, o_ref, acc_ref,
                qm_ref, cb_ref, *, tq, tk):
    @pl.when(flag_ref[0] == 1)
    def _():
        _fox_bounded(bound_ref[0], q_ref, k_ref, v_ref, ccol_ref, crow_ref, o_ref, acc_ref,
                     qm_ref, cb_ref, tq=tq, tk=tk)

    @pl.when(flag_ref[0] == 0)
    def _():
        _fox_running_max(q_ref, k_ref, v_ref, ccol_ref, crow_ref, o_ref, tq=tq, tk=tk)


def _fox_running_max(q_ref, k_ref, v_ref, ccol_ref, crow_ref, o_ref, *, tq, tk):
    i = pl.program_id(1)
    nsub = tq // tk
    lane = lax.broadcasted_iota(jnp.int32, (tq, LANES), 1)
    low = lane < HEAD_DIM
    row = lax.broadcasted_iota(jnp.int32, (tq, tk), 0)
    col = lax.broadcasted_iota(jnp.int32, (tq, tk), 1)
    ccol = ccol_ref[0]

    for hp in range(PAIRS):
        ls = slice(LANES * hp, LANES * (hp + 1))
        q2 = q_ref[0, :, ls]
        zero = jnp.zeros_like(q2)
        qm = (jnp.where(low, q2, zero), jnp.where(low, zero, q2))
        cc = tuple(ccol[:, 2 * hp + hh:2 * hp + hh + 1] for hh in range(2))

        def step(j, carry, sub, ls=ls, qm=qm, cc=cc, hp=hp):
            ks = pl.multiple_of(j * tk, tk)
            k2 = k_ref[0, pl.ds(ks, tk), ls]
            v2 = v_ref[0, pl.ds(ks, tk), ls]
            new = []
            for hh in range(2):
                m, l, acc = carry[hh]
                s = _mm(qm[hh], k2, dims=_NT)
                cr = crow_ref[0, j, 2 * hp + hh:2 * hp + hh + 1, :]
                s = s + (cc[hh] - cr)
                if sub is not None:
                    s = jnp.where(col + sub * tk <= row, s, NEG_BIG)
                m_new = jnp.maximum(m, jnp.max(s, axis=1, keepdims=True))
                alpha = jnp.exp(m - m_new)
                p = jnp.exp(s - m_new)
                l_new = alpha * l + jnp.sum(p, axis=1, keepdims=True)
                acc_new = alpha * acc + _mm(p.astype(BF16), v2)
                new.append((m_new, l_new, acc_new))
            return tuple(new)

        init_one = (jnp.full((tq, 1), NEG_BIG, F32), jnp.zeros((tq, 1), F32),
                    jnp.zeros((tq, LANES), F32))
        carry = lax.fori_loop(0, i * nsub, lambda j, c: step(j, c, None), (init_one, init_one))
        for sub in range(nsub):
            carry = step(i * nsub + sub, carry, sub)
        o0 = carry[0][2] / carry[0][1]
        o1 = carry[1][2] / carry[1][1]
        o_ref[0, :, ls] = jnp.where(low, o0, o1).astype(BF16)


def _fox(flag, bound, q, k, v, ccol, crow, tq, tk):
    bsz, seq, _ = q.shape
    kern = functools.partial(_fox_kernel, tq=tq, tk=tk)
    return pl.pallas_call(
        kern,
        grid=(bsz, seq // tq),
        in_specs=[
            pl.BlockSpec(memory_space=pltpu.SMEM),
            pl.BlockSpec(memory_space=pltpu.SMEM),
            pl.BlockSpec((1, tq, FOX_WIDTH), lambda b, i: (b, i, 0)),
            pl.BlockSpec((1, seq, FOX_WIDTH), lambda b, i: (b, 0, 0)),
            pl.BlockSpec((1, seq, FOX_WIDTH), lambda b, i: (b, 0, 0)),
            pl.BlockSpec((1, tq, LANES), lambda b, i: (b, i, 0)),
            pl.BlockSpec((1, seq // tk, 16, tk), lambda b, i: (b, 0, 0, 0)),
        ],
        out_specs=pl.BlockSpec((1, tq, FOX_WIDTH), lambda b, i: (b, i, 0)),
        out_shape=jax.ShapeDtypeStruct((bsz, seq, FOX_WIDTH), BF16),
        scratch_shapes=[pltpu.VMEM((PAIRS, tq, 2 * LANES), F32),
                        pltpu.VMEM((FOX_HEADS, tq, LANES), BF16),
                        pltpu.VMEM((FOX_HEADS, tq, LANES), F32)],
        compiler_params=pltpu.CompilerParams(
            dimension_semantics=("arbitrary", "arbitrary"), vmem_limit_bytes=VMEM_LIMIT),
        name="fox",
    )(flag, bound, q, k, v, ccol, crow)


def _unit_tri_inverses(mats, bs, passes):
    c, w = mats[0].shape
    n = range(len(mats))
    lane_cache = {}

    def lane_ids(s):
        if s not in lane_cache:
            lane_cache[s] = lax.broadcasted_iota(jnp.int32, (s, w), 1)
        return lane_cache[s]

    def terms(x):
        return _split_bf16(x, passes)

    def dot_terms(a_t, b_t):
        out = None
        for i, ai in enumerate(a_t):
            for j, bj in enumerate(b_t):
                if i + j < max(len(a_t), len(b_t)):
                    t = lax.dot_general(ai, bj, _NN, preferred_element_type=F32)
                    out = t if out is None else out + t
        return out

    def block_rows(x, s, offset):
        lb = jnp.right_shift(lane_ids(s), s.bit_length() - 1)
        zero = jnp.zeros((s, w), BF16)
        keep = [lb == j for j in range(w // s)]
        out = []
        for t in terms(x):
            rows = []
            for j in range(w // s):
                if offset and j % 2 == 0:
                    rows.append(zero)
                else:
                    rows.append(jnp.where(keep[j - offset], t, zero))
            out.append(jnp.concatenate(rows, axis=0))
        return out

    def mm(a, b_terms):
        return dot_terms(terms(a), b_terms)

    s = bs
    sh = s.bit_length() - 1
    in_mat = jnp.bitwise_and(lane_ids(s), c - 1)
    q = []
    for a2 in mats:
        d = a2[0:s, :]
        for r in range(1, c // s):
            d = jnp.where(jnp.right_shift(in_mat, sh) == r, a2[r * s:(r + 1) * s, :], d)
        q.append(d)
    p = [mm(q[i], block_rows(q[i], s, 0)) for i in n]
    for _ in range(s.bit_length() - 3):
        both = [mm(jnp.concatenate([q[i], p[i]], axis=0), block_rows(p[i], s, 0)) for i in n]
        q = [q[i] + p[i] + both[i][0:s] for i in n]
        p = [both[i][s:2 * s] for i in n]
    q = [q[i] + p[i] + mm(q[i], block_rows(p[i], s, 0)) for i in n]

    while s < c:
        sh = s.bit_length() - 1
        lane = lane_ids(s)
        first = jnp.bitwise_and(lane, s) == 0
        pair_id = jnp.right_shift(jnp.bitwise_and(lane, c - 1), sh + 1)
        l21 = []
        for a2 in mats:
            z = jnp.zeros((s, w), F32)
            for m in range(c // (2 * s)):
                rows = a2[(2 * m + 1) * s:(2 * m + 2) * s, :]
                z = jnp.where(pair_id == m, jnp.where(first, rows, 0.0), z)
            l21.append(z)
        x = [l21[i] + mm(l21[i], block_rows(q[i], s, 0)) for i in n]
        t21 = [x[i] + mm(q[i], block_rows(x[i], s, 1)) for i in n]
        q = [jnp.concatenate([jnp.where(first, q[i], 0.0), jnp.where(first, t21[i], q[i])], axis=0)
             for i in n]
        s *= 2
    return q


def _head_sums(x, e, pa):
    wd = e.shape[0]
    return jnp.concatenate(
        [_mm(x[:, j:j + wd], e, pa, 1) for j in range(0, x.shape[1], wd)], axis=1)


def _rwkv_kernel(rw_ref, w0_ref, w2_ref, a0_ref, a2_ref, g2_ref, kk_ref, ka_ref,
                 rk_ref, lg_ref, lb_ref, e_ref, tri_ref, o_ref, s_ref, *, prec, nb):
    cch = CHUNK
    rows = nb * cch
    w = RWKV_WIDTH
    c = pl.program_id(1)
    pg, pd, ps = prec
    e = e_ref[...]

    @pl.when(c == 0)
    def _():
        s_ref[...] = jnp.zeros_like(s_ref)

    lane = lax.broadcasted_iota(jnp.int32, (cch, LANES), 1)
    low = lane < HEAD_DIM
    ri = lax.broadcasted_iota(jnp.int32, (cch, cch), 0)
    ci = lax.broadcasted_iota(jnp.int32, (cch, cch), 1)
    strict = ci < ri
    incl = ci <= ri
    blockdiag = (ri < HEAD_DIM) == (ci < HEAD_DIM)
    sl = [slice(LANES * hp, LANES * (hp + 1)) for hp in range(PAIRS)]
    rs = [slice(cch * i, cch * (i + 1)) for i in range(nb)]
    combos = [(i, hp) for i in range(nb) for hp in range(PAIRS)]
    n = range(len(combos))

    def halves(z):
        zero = jnp.zeros_like(z)
        return jnp.concatenate([jnp.where(low, z, zero), jnp.where(low, zero, z)], axis=0)

    def shifted(lo, hi):
        return rw_ref[:, :, lo:hi].reshape(rows, hi - lo)

    wa = shifted(OFF_LORA, OFF_GDN)
    gdn = shifted(OFF_GDN, RW_COLS)
    wlog = _log_sigmoid(w0_ref[...] + _mm(jnp.tanh(wa), w2_ref[...])) - 0.5
    logd = -jnp.exp(wlog)
    a = _sigmoid(a0_ref[...] + _mm(wa, a2_ref[...]))
    g = _mm(_sigmoid(gdn), g2_ref[...])
    tri = tri_ref[...]
    cum = jnp.concatenate([_mm(tri, logd[rs[i]], 1, 2) for i in range(nb)], axis=0)
    lasts = [cum[cch * (i + 1) - 1:cch * (i + 1), :] for i in range(nb)]
    clast = jnp.concatenate([jnp.broadcast_to(z, (cch, w)) for z in lasts], axis=0)
    pc = [jnp.exp(z) for z in lasts]
    k = shifted(w, 2 * w)
    kk = k * kk_ref[...]
    kk = kk * lax.rsqrt(jnp.maximum(_head_sums(kk * kk, e, 2), L2_EPS * L2_EPS))
    k = k * (1.0 + (a - 1.0) * ka_ref[...])
    avec = -kk
    bvec = kk * a
    einv = jnp.exp(-cum)
    at = (avec * jnp.exp(cum - logd)).astype(BF16)
    bt = (bvec * einv).astype(BF16)
    kt = (k * einv).astype(BF16)
    etail = jnp.exp(clast - cum)
    bh = (bvec * etail).astype(BF16)
    kh = (k * etail).astype(BF16)
    r = shifted(0, w)
    rt = (r * jnp.exp(cum)).astype(BF16)
    bonus = _head_sums(r * k * rk_ref[...], e, 1)
    v = shifted(2 * w, 3 * w)
    vb = v.astype(BF16)
    bv = bonus * v

    def blk(z, j):
        i, hp = combos[j]
        return z[rs[i], sl[hp]]

    aab, aak, arbk = [], [], []
    for j in n:
        rb = jnp.concatenate([blk(bt, j), blk(kt, j)], axis=0)
        blocks = []
        for hh in range(2):
            sel = low if hh == 0 else jnp.logical_not(low)
            zero = jnp.zeros((cch, LANES), BF16)
            la = jnp.concatenate([jnp.where(sel, blk(at, j), zero),
                                  jnp.where(sel, blk(rt, j), zero)], axis=0)
            blocks.append(_mm(la, rb, pg, pg, dims=_NT))
        aab.append(jnp.concatenate(
            [jnp.where(strict, gm[0:cch, 0:cch], 0.0) for gm in blocks], axis=1))
        aak.append(jnp.concatenate(
            [jnp.where(strict, gm[0:cch, cch:2 * cch], 0.0) for gm in blocks], axis=1))
        arbk.append(jnp.concatenate(
            [jnp.where(incl, gm[cch:2 * cch, 0:cch], 0.0) for gm in blocks]
            + [jnp.where(incl, gm[cch:2 * cch, cch:2 * cch], 0.0) for gm in blocks], axis=1))
    qm = _unit_tri_inverses(aab, INV_BLOCK, pd)

    sp = [s_ref[i, hp] for i, hp in combos]
    vst = [halves(blk(vb, j)) for j in n]
    rhs = [_mm(blk(at, j), sp[j], ps, ps, dims=_NT) + _mm(aak[j], vst[j], ps, ps) for j in n]
    u = [rhs[j] + _mm(qm[j], halves(rhs[j]), ps, ps) for j in n]
    ys = [_mm(blk(rt, j), sp[j], ps, ps, dims=_NT)
          + _mm(arbk[j], jnp.concatenate([halves(u[j]).astype(BF16), vst[j]], axis=0), ps, ps)
          for j in n]
    for j in n:
        i, hp = combos[j]
        uv = jnp.concatenate([u[j], blk(vb, j).astype(F32)], axis=0)
        bk = jnp.concatenate([blk(bh, j), blk(kh, j)], axis=0)
        upd = _mm(uv, bk, ps, ps, dims=_TN)
        s_ref[i, hp] = sp[j] * pc[i][:, sl[hp]] + jnp.where(blockdiag, upd, 0.0)

    y = jnp.concatenate(
        [jnp.concatenate(ys[PAIRS * i:PAIRS * (i + 1)], axis=1) for i in range(nb)], axis=0)
    inv_n = 1.0 / HEAD_DIM
    mean = _head_sums(y, e, 1) * inv_n
    d = y - mean
    var = _head_sums(d * d, e, 1) * inv_n
    yn = d * lax.rsqrt(var + GN_EPS) * lg_ref[...] + lb_ref[...]
    o_ref[...] = ((yn + bv) * g).astype(BF16).reshape(nb, cch, w)


def _rwkv(rw, w0, w2, a0, a2, g2, k_k, k_a, r_k, lnx_g, lnx_b, e, prec):
    bsz, seq, _ = rw.shape
    cch = CHUNK
    nb = RWKV_SEQS if bsz % RWKV_SEQS == 0 else 1
    row = lax.broadcasted_iota(jnp.int32, (cch, cch), 0)
    col = lax.broadcasted_iota(jnp.int32, (cch, cch), 1)
    tri = (col <= row).astype(BF16)
    const = lambda shape: pl.BlockSpec(shape, lambda b, t: (0,) * len(shape))
    args = (w0, w2, a0, a2, g2, k_k, k_a, r_k, lnx_g, lnx_b, e, tri)
    return pl.pallas_call(
        functools.partial(_rwkv_kernel, prec=prec, nb=nb),
        grid=(bsz // nb, seq // cch),
        in_specs=[pl.BlockSpec((nb, cch, RW_COLS), lambda b, t: (b, t, 0))]
        + [const(a.shape) for a in args],
        out_specs=pl.BlockSpec((nb, cch, RWKV_WIDTH), lambda b, t: (b, t, 0)),
        out_shape=jax.ShapeDtypeStruct((bsz, seq, RWKV_WIDTH), BF16),
        scratch_shapes=[pltpu.VMEM((nb, PAIRS, LANES, LANES), F32)],
        compiler_params=pltpu.CompilerParams(
            dimension_semantics=("arbitrary", "arbitrary"), vmem_limit_bytes=VMEM_LIMIT),
        name="rwkv",
    )(rw, *args)


def _out_ffn_kernel(of_ref, or_ref, x_ref, mod_ref, g2_ref, wt_ref, wb_ref, wg_ref, wu_ref, wd_ref,
                    o_ref, *, slabs):
    mod = mod_ref[0]
    gt1, sh2, sc2, gt2 = mod[2:3, :], mod[3:4, :], mod[4:5, :], mod[5:6, :]
    mix = _mm(of_ref[0], wt_ref[...]) + _mm(or_ref[0], wb_ref[...])
    x1 = x_ref[0] + gt1 * mix
    ms = jnp.mean(x1 * x1, axis=-1, keepdims=True)
    y = x1 * lax.rsqrt(ms + RMS_EPS) * g2_ref[...]
    h2 = (y * (1.0 + sc2) + sh2).astype(BF16)
    acc = None
    for lo, hi in slabs:
        gate = _mm(h2, wg_ref[:, lo:hi])
        up = _mm(h2, wu_ref[:, lo:hi])
        act = (gate * _sigmoid(gate) * up).astype(BF16)
        part = _mm(act, wd_ref[lo:hi, :])
        acc = part if acc is None else acc + part
    o_ref[0] = x1 + gt2 * acc


def _out_ffn(o_fox, o_rwkv, x, mod, g2, w_top, w_bot, wg, wu, wd, tm):
    bsz, seq, d = x.shape
    dff = wg.shape[1]
    tiles = dff // MXU_DIM if dff % MXU_DIM == 0 else 1
    cut = (tiles // 2) * (dff // tiles)
    slabs = ((0, cut), (cut, dff)) if cut else ((0, dff),)
    resident = lambda shape: pl.BlockSpec(shape, lambda b, t: (0,) * len(shape),
                                          pipeline_mode=pl.Buffered(1))
    return pl.pallas_call(
        functools.partial(_out_ffn_kernel, slabs=slabs),
        grid=(bsz, seq // tm),
        in_specs=[
            pl.BlockSpec((1, tm, FOX_WIDTH), lambda b, t: (b, t, 0)),
            pl.BlockSpec((1, tm, RWKV_WIDTH), lambda b, t: (b, t, 0)),
            pl.BlockSpec((1, tm, d), lambda b, t: (b, t, 0)),
            pl.BlockSpec((1, 6, d), lambda b, t: (b, 0, 0)),
            resident((1, d)), resident(w_top.shape), resident(w_bot.shape),
            resident(wg.shape), resident(wu.shape), resident(wd.shape),
        ],
        out_specs=pl.BlockSpec((1, tm, d), lambda b, t: (b, t, 0)),
        out_shape=jax.ShapeDtypeStruct((bsz, seq, d), F32),
        compiler_params=pltpu.CompilerParams(
            dimension_semantics=("arbitrary", "arbitrary"), vmem_limit_bytes=VMEM_LIMIT),
        name="out_ffn",
    )(o_fox, o_rwkv, x, mod, g2, w_top, w_bot, wg, wu, wd)


def _pad_cols(w, n):
    return jnp.pad(w, ((0, 0), (0, n - w.shape[1])))


def _pad_rows(w, n):
    return jnp.pad(w, ((0, n - w.shape[0]), (0, 0)))


def _layer(x, mod, norm1_g, norm2_g, w_in, fox_f_bias, fox_q_gain, fox_k_gain, rwkv_mu, rwkv_w0,
           rwkv_w2, rwkv_a0, rwkv_a2, rwkv_g2, rwkv_k_k, rwkv_k_a, rwkv_r_k, rwkv_lnx_g,
           rwkv_lnx_b, w_out, ffn_w_gate, ffn_w_up, ffn_w_down, *, tm, tq, tk, prec):
    bsz, seq, d = x.shape
    w = RWKV_WIDTH
    nfox = 3 * FOX_WIDTH + FOX_HEADS

    w_bf = w_in.astype(BF16)
    wqkv = wf = w_bf
    wr = w_in[:, nfox:]
    assert 3 * w == OFF_LORA and DECAY_LORA + A_LORA == OFF_GDN - OFF_LORA
    wrw = _pad_cols(wr, RW_COLS).astype(BF16)
    mu_p = _pad_cols(rwkv_mu.reshape(1, -1), RW_COLS)
    fbrow = _pad_cols(fox_f_bias.reshape(1, -1), LANES)
    qg = jnp.tile(fox_q_gain, (1, 1)).reshape(1, FOX_WIDTH)
    kg = fox_k_gain.reshape(1, FOX_WIDTH)
    hi = lax.broadcasted_iota(jnp.int32, (MXU_DIM, MXU_DIM), 0) // HEAD_DIM
    hj = lax.broadcasted_iota(jnp.int32, (MXU_DIM, MXU_DIM), 1) // HEAD_DIM
    e = (hi == hj).astype(BF16)

    q, k, v, ccol, crow, rw = _in_proj(
        x, mod, norm1_g.reshape(1, d), wqkv, wf, wrw, fbrow, qg, kg, e, mu_p, tm, tk)
    bound = (1.05 * HEAD_DIM ** 0.5) * jnp.max(jnp.abs(fox_q_gain)) * jnp.max(jnp.abs(fox_k_gain))
    flag = (bound <= FOX_BOUND_MAX).astype(jnp.int32)
    o_fox = _fox(flag.reshape(1), bound.astype(F32).reshape(1), q, k, v, ccol, crow, tq, tk)
    o_rwkv = _rwkv(
        rw, rwkv_w0.reshape(1, w), _pad_rows(rwkv_w2, LANES).astype(BF16),
        rwkv_a0.reshape(1, w), jnp.pad(rwkv_a2, ((DECAY_LORA, 0), (0, 0))).astype(BF16),
        _pad_rows(rwkv_g2, RW_COLS - OFF_GDN).astype(BF16), rwkv_k_k.reshape(1, w), rwkv_k_a.reshape(1, w),
        rwkv_r_k.reshape(1, w), rwkv_lnx_g.reshape(1, w), rwkv_lnx_b.reshape(1, w), e, prec)
    wo = w_out.astype(BF16)
    return _out_ffn(o_fox, o_rwkv, x, mod, norm2_g.reshape(1, d), wo[0:FOX_WIDTH], wo[FOX_WIDTH:],
                    ffn_w_gate.astype(BF16), ffn_w_up.astype(BF16), ffn_w_down.astype(BF16),
                    tm)


def kernel(x, c, ada_w, ada_b, norm1_g, norm2_g, w_in, fox_f_bias, fox_q_gain, fox_k_gain, rwkv_mu,
           rwkv_w0, rwkv_w2, rwkv_a0, rwkv_a2, rwkv_g2, rwkv_k_k, rwkv_k_a, rwkv_r_k, rwkv_lnx_g,
           rwkv_lnx_b, w_out, ffn_w_gate, ffn_w_up, ffn_w_down):
    bsz, seq, d = x.shape
    depth = ada_w.shape[0]
    tm = min(ROW_TILE, seq)
    tk = min(KEY_TILE, seq)
    tq = min(QUERY_TILE, seq)
    assert seq % tm == 0 and seq % tq == 0 and tq % tk == 0 and seq % CHUNK == 0
    for l in range(depth):
        mod = _ada(c, ada_w, ada_b, l).reshape(bsz, 6, d)
        x = _layer(x, mod, norm1_g[l], norm2_g[l], w_in[l], fox_f_bias[l], fox_q_gain[l],
                   fox_k_gain[l], rwkv_mu[l], rwkv_w0[l], rwkv_w2[l], rwkv_a0[l], rwkv_a2[l],
                   rwkv_g2[l], rwkv_k_k[l], rwkv_k_a[l], rwkv_r_k[l], rwkv_lnx_g[l],
                   rwkv_lnx_b[l], w_out[l], ffn_w_gate[l], ffn_w_up[l], ffn_w_down[l],
                   tm=tm, tq=tq, tk=tk, prec=RWKV_PASSES)
    return x
```

```python
import functools

import jax
import jax.numpy as jnp
from jax import lax
from jax.experimental import pallas as pl
from jax.experimental.pallas import tpu as pltpu

F32 = jnp.float32
BF16 = jnp.bfloat16

HEAD_DIM = 64
FOX_HEADS = 8
RWKV_HEADS = 8
FOX_WIDTH = FOX_HEADS * HEAD_DIM
RWKV_WIDTH = RWKV_HEADS * HEAD_DIM
DECAY_LORA = 64
A_LORA = 64
GATE_LORA = 160
RMS_EPS = 1e-6
GN_EPS = 64e-5
L2_EPS = 1e-12

LANES = 128
MXU_DIM = 256
PAIRS = FOX_HEADS // 2
RWKV_SEQS = 8
INV_BLOCK = 64
RW_COLS = 1920
OFF_LORA, OFF_GDN = 1536, 1664
CHUNK = 128
NEG_BIG = -1e30
RWKV_PASSES = (1, 1, 1)
FOX_BOUND_MAX = 30.0
ROW_TILE = 512
QUERY_TILE = 512
KEY_TILE = 256
VMEM_LIMIT = 56 * 1024 * 1024


def _split_bf16(x, n):
    if x.dtype == BF16:
        return [x]
    parts = []
    r = x
    for i in range(n):
        p = r.astype(BF16)
        parts.append(p)
        if i < n - 1:
            r = r - p.astype(F32)
    return parts


_NN = (((1,), (0,)), ((), ()))
_NT = (((1,), (1,)), ((), ()))
_TN = (((0,), (0,)), ((), ()))


def _mm(a, b, pa=1, pb=1, dims=_NN):
    a_parts = _split_bf16(a, pa)
    b_parts = _split_bf16(b, pb)
    order = max(len(a_parts), len(b_parts))
    out = None
    for i, ai in enumerate(a_parts):
        for j, bj in enumerate(b_parts):
            if i + j >= order:
                continue
            t = lax.dot_general(ai, bj, dims, preferred_element_type=F32)
            out = t if out is None else out + t
    return out


def _log_sigmoid(z):
    return jnp.minimum(z, 0.0) - jnp.log(1.0 + jnp.exp(-jnp.abs(z)))


def _sigmoid(z):
    return 1.0 / (1.0 + jnp.exp(-z))


def _ada_kernel(c_ref, w_ref, b_ref, o_ref):
    c = c_ref[...]
    cond = c * _sigmoid(c)
    o_ref[...] = _mm(cond, w_ref[0], 2, 2) + b_ref[0]


def _ada(c, ada_w, ada_b, layer):
    bsz, d = c.shape
    depth, _, n = ada_w.shape
    tn = 512
    return pl.pallas_call(
        _ada_kernel,
        grid=(n // tn,),
        in_specs=[
            pl.BlockSpec((bsz, d), lambda j: (0, 0)),
            pl.BlockSpec((1, d, tn), lambda j: (layer, 0, j)),
            pl.BlockSpec((1, 1, tn), lambda j: (layer, 0, j)),
        ],
        out_specs=pl.BlockSpec((bsz, tn), lambda j: (0, j)),
        out_shape=jax.ShapeDtypeStruct((bsz, n), F32),
        compiler_params=pltpu.CompilerParams(dimension_semantics=("arbitrary",)),
        name="ada",
    )(c, ada_w, ada_b.reshape(depth, 1, n))


def _in_proj_kernel(x_ref, mod_ref, g1_ref, wqkv_ref, wf_ref, wrw_ref,
                    fbrow_ref, qg_ref, kg_ref, e_ref, trilo_ref, mu_ref,
                    q_ref, k_ref, v_ref, ccol_ref, crow_ref, rw_ref,
                    carry_row, prev_ref, *, tm, tk):
    t = pl.program_id(1)

    @pl.when(t == 0)
    def _():
        carry_row[...] = jnp.zeros_like(carry_row)
        prev_ref[...] = jnp.zeros_like(prev_ref)

    x = x_ref[0]
    mod = mod_ref[0]
    sh1 = mod[0:1, :]
    sc1 = mod[1:2, :]
    ms = jnp.mean(x * x, axis=-1, keepdims=True)
    y = x * lax.rsqrt(ms + RMS_EPS) * g1_ref[...]
    hb = (y * (1.0 + sc1) + sh1).astype(BF16)

    qkv = _mm(hb, wqkv_ref[...])
    e = e_ref[...]
    q = qkv[:, 0:FOX_WIDTH]
    k = qkv[:, FOX_WIDTH:2 * FOX_WIDTH]
    qms = _head_sums(q * q, e, 1) * (1.0 / HEAD_DIM)
    kms = _head_sums(k * k, e, 1) * (1.0 / HEAD_DIM)
    q_ref[0] = (q * lax.rsqrt(qms + RMS_EPS) * qg_ref[...] * (HEAD_DIM ** -0.5)).astype(BF16)
    k_ref[0] = (k * lax.rsqrt(kms + RMS_EPS) * kg_ref[...]).astype(BF16)
    v_ref[0] = qkv[:, 2 * FOX_WIDTH:3 * FOX_WIDTH].astype(BF16)

    rw = _mm(hb, wrw_ref[...])
    rolled = pltpu.roll(rw, 1, 0)
    first_row = lax.broadcasted_iota(jnp.int32, (8, 1), 0) == 0
    top = jnp.where(first_row, prev_ref[0:1, :], rolled[0:8])
    prev_ref[0:1, :] = rw[tm - 1:tm, :]
    rw_ref[0] = rw + (jnp.concatenate([top, rolled[8:tm]], axis=0) - rw) * mu_ref[...]

    lf = _log_sigmoid(_mm(hb, wf_ref[...]) + fbrow_ref[...])
    blocks = []
    carry = carry_row[...]
    for j in range(tm // LANES):
        cj = _mm(trilo_ref[...], lf[j * LANES:(j + 1) * LANES], 1, 3) + carry
        blocks.append(cj)
        carry = cj[LANES - 1:LANES, :]
    carry_row[...] = carry
    ccol_ref[0] = jnp.concatenate(blocks, axis=0)
    for j, cj in enumerate(blocks):
        jj, off = divmod(j * LANES, tk)
        crow_ref[0, jj, :, off:off + LANES] = cj.T[0:16, :]


def _in_proj(x, mod, g1, wqkv, wf, wrw, fbrow, qg, kg, e, mu, tm, tk):
    bsz, seq, d = x.shape
    nt = seq // tm
    row = lax.broadcasted_iota(jnp.int32, (LANES, LANES), 0)
    col = lax.broadcasted_iota(jnp.int32, (LANES, LANES), 1)
    trilo = (col <= row).astype(BF16)
    const = lambda shape: pl.BlockSpec(shape, lambda b, t: (0,) * len(shape))
    kern = functools.partial(_in_proj_kernel, tm=tm, tk=tk)
    return pl.pallas_call(
        kern,
        grid=(bsz, nt),
        in_specs=[
            pl.BlockSpec((1, tm, d), lambda b, t: (b, t, 0)),
            pl.BlockSpec((1, 6, d), lambda b, t: (b, 0, 0)),
            const((1, d)),
            pl.BlockSpec((d, 3 * FOX_WIDTH), lambda b, t: (0, 0)),
            pl.BlockSpec((d, LANES), lambda b, t: (0, 3 * FOX_WIDTH // LANES)),
            const(wrw.shape),
            const(fbrow.shape), const(qg.shape), const(kg.shape),
            const(e.shape), const(trilo.shape), const(mu.shape),
        ],
        out_specs=[
            pl.BlockSpec((1, tm, FOX_WIDTH), lambda b, t: (b, t, 0)),
            pl.BlockSpec((1, tm, FOX_WIDTH), lambda b, t: (b, t, 0)),
            pl.BlockSpec((1, tm, FOX_WIDTH), lambda b, t: (b, t, 0)),
            pl.BlockSpec((1, tm, LANES), lambda b, t: (b, t, 0)),
            pl.BlockSpec((1, tm // tk, 16, tk), lambda b, t: (b, t, 0, 0)),
            pl.BlockSpec((1, tm, RW_COLS), lambda b, t: (b, t, 0)),
        ],
        out_shape=[
            jax.ShapeDtypeStruct((bsz, seq, FOX_WIDTH), BF16),
            jax.ShapeDtypeStruct((bsz, seq, FOX_WIDTH), BF16),
            jax.ShapeDtypeStruct((bsz, seq, FOX_WIDTH), BF16),
            jax.ShapeDtypeStruct((bsz, seq, LANES), F32),
            jax.ShapeDtypeStruct((bsz, seq // tk, 16, tk), F32),
            jax.ShapeDtypeStruct((bsz, seq, RW_COLS), F32),
        ],
        scratch_shapes=[pltpu.VMEM((1, LANES), F32), pltpu.VMEM((8, RW_COLS), F32)],
        compiler_params=pltpu.CompilerParams(
            dimension_semantics=("arbitrary", "arbitrary"), vmem_limit_bytes=VMEM_LIMIT),
        name="in_proj",
    )(x, mod, g1, wqkv, wf, wrw, fbrow, qg, kg, e, trilo, mu)


def _fox_bounded(bound, q_ref, k_ref, v_ref, ccol_ref, crow_ref, o_ref, acc_ref, qm_ref, cb_ref,
                 *, tq, tk):
    i = pl.program_id(1)
    nsub = tq // tk
    lane = lax.broadcasted_iota(jnp.int32, (tq, LANES), 1)
    low = lane < HEAD_DIM
    klow = lax.broadcasted_iota(jnp.int32, (tk, LANES), 1) < HEAD_DIM
    one_lo = jnp.where(klow, 1.0, 0.0).astype(BF16)
    one_hi = jnp.where(klow, 0.0, 1.0).astype(BF16)

    ccol = ccol_ref[0] - bound
    for hp in range(PAIRS):
        q2 = q_ref[0, :, LANES * hp:LANES * (hp + 1)]
        zero = jnp.zeros_like(q2)
        qm_ref[2 * hp] = jnp.where(low, q2, zero)
        qm_ref[2 * hp + 1] = jnp.where(low, zero, q2)
        for hh in range(2):
            h = 2 * hp + hh
            cb_ref[h] = jnp.broadcast_to(ccol[:, h:h + 1], (tq, LANES))
    acc_ref[...] = jnp.zeros_like(acc_ref)

    def step(j, sub):
        r0 = 0 if sub is None else sub * tk
        rows = tq - r0
        ks = pl.multiple_of(j * tk, tk)
        if sub is not None:
            causal = (lax.broadcasted_iota(jnp.int32, (rows, tk), 1)
                      <= lax.broadcasted_iota(jnp.int32, (rows, tk), 0))

        def qk(hp):
            k2 = k_ref[0, pl.ds(ks, tk), LANES * hp:LANES * (hp + 1)]
            return [_mm(qm_ref[2 * hp + hh, r0:tq, :], k2, dims=_NT) for hh in range(2)]

        def softmax_numerators(hp, scores):
            probs = []
            for hh in range(2):
                h = 2 * hp + hh
                cb = cb_ref[h, r0:tq, :]
                bias = jnp.concatenate([cb] * (tk // LANES), axis=1) - crow_ref[0, j, h:h + 1, :]
                s = scores[hh] + bias
                if sub is not None:
                    s = jnp.where(causal, s, NEG_BIG)
                probs.append(jnp.exp(s).astype(BF16))
            return probs

        def pv(hp, probs):
            v2 = v_ref[0, pl.ds(ks, tk), LANES * hp:LANES * (hp + 1)]
            zero = jnp.zeros_like(v2)
            vaug = jnp.concatenate([
                jnp.concatenate([jnp.where(klow, v2, zero), one_lo], axis=1),
                jnp.concatenate([jnp.where(klow, zero, v2), one_hi], axis=1)], axis=0)
            acc_ref[hp, r0:tq, :] += _mm(jnp.concatenate(probs, axis=1), vaug)

        scores, probs = {}, {}
        for t in range(PAIRS + 2):
            if t < PAIRS:
                scores[t] = qk(t)
            if 0 <= t - 1 < PAIRS:
                probs[t - 1] = softmax_numerators(t - 1, scores.pop(t - 1))
            if 0 <= t - 2 < PAIRS:
                pv(t - 2, probs.pop(t - 2))

    def body(jo, carry):
        for m in range(nsub):
            step(jo * nsub + m, None)
        return carry

    lax.fori_loop(0, i, body, 0)
    for m in range(nsub):
        step(i * nsub + m, m)
    for hp in range(PAIRS):
        a = acc_ref[hp]
        o = a[:, 0:LANES] / a[:, LANES:2 * LANES]
        o_ref[0, :, LANES * hp:LANES * (hp + 1)] = o.astype(BF16)


def _fox_kernel(flag_ref, bound_ref, q_ref, k_ref, v_ref, ccol_ref, crow_ref, o_ref, acc_ref,
                qm_ref, cb_ref, *, tq, tk):
    @pl.when(flag_ref[0] == 1)
    def _():
        _fox_bounded(bound_ref[0], q_ref, k_ref, v_ref, ccol_ref, crow_ref, o_ref, acc_ref,
                     qm_ref, cb_ref, tq=tq, tk=tk)

    @pl.when(flag_ref[0] == 0)
    def _():
        _fox_running_max(q_ref, k_ref, v_ref, ccol_ref, crow_ref, o_ref, tq=tq, tk=tk)


def _fox_running_max(q_ref, k_ref, v_ref, ccol_ref, crow_ref, o_ref, *, tq, tk):
    i = pl.program_id(1)
    nsub = tq // tk
    lane = lax.broadcasted_iota(jnp.int32, (tq, LANES), 1)
    low = lane < HEAD_DIM
    row = lax.broadcasted_iota(jnp.int32, (tq, tk), 0)
    col = lax.broadcasted_iota(jnp.int32, (tq, tk), 1)
    ccol = ccol_ref[0]

    for hp in range(PAIRS):
        ls = slice(LANES * hp, LANES * (hp + 1))
        q2 = q_ref[0, :, ls]
        zero = jnp.zeros_like(q2)
        qm = (jnp.where(low, q2, zero), jnp.where(low, zero, q2))
        cc = tuple(ccol[:, 2 * hp + hh:2 * hp + hh + 1] for hh in range(2))

        def step(j, carry, sub, ls=ls, qm=qm, cc=cc, hp=hp):
            ks = pl.multiple_of(j * tk, tk)
            k2 = k_ref[0, pl.ds(ks, tk), ls]
            v2 = v_ref[0, pl.ds(ks, tk), ls]
            new = []
            for hh in range(2):
                m, l, acc = carry[hh]
                s = _mm(qm[hh], k2, dims=_NT)
                cr = crow_ref[0, j, 2 * hp + hh:2 * hp + hh + 1, :]
                s = s + (cc[hh] - cr)
                if sub is not None:
                    s = jnp.where(col + sub * tk <= row, s, NEG_BIG)
                m_new = jnp.maximum(m, jnp.max(s, axis=1, keepdims=True))
                alpha = jnp.exp(m - m_new)
                p = jnp.exp(s - m_new)
                l_new = alpha * l + jnp.sum(p, axis=1, keepdims=True)
                acc_new = alpha * acc + _mm(p.astype(BF16), v2)
                new.append((m_new, l_new, acc_new))
            return tuple(new)

        init_one = (jnp.full((tq, 1), NEG_BIG, F32), jnp.zeros((tq, 1), F32),
                    jnp.zeros((tq, LANES), F32))
        carry = lax.fori_loop(0, i * nsub, lambda j, c: step(j, c, None), (init_one, init_one))
        for sub in range(nsub):
            carry = step(i * nsub + sub, carry, sub)
        o0 = carry[0][2] / carry[0][1]
        o1 = carry[1][2] / carry[1][1]
        o_ref[0, :, ls] = jnp.where(low, o0, o1).astype(BF16)


def _fox(flag, bound, q, k, v, ccol, crow, tq, tk):
    bsz, seq, _ = q.shape
    kern = functools.partial(_fox_kernel, tq=tq, tk=tk)
    return pl.pallas_call(
        kern,
        grid=(bsz, seq // tq),
        in_specs=[
            pl.BlockSpec(memory_space=pltpu.SMEM),
            pl.BlockSpec(memory_space=pltpu.SMEM),
            pl.BlockSpec((1, tq, FOX_WIDTH), lambda b, i: (b, i, 0)),
            pl.BlockSpec((1, seq, FOX_WIDTH), lambda b, i: (b, 0, 0)),
            pl.BlockSpec((1, seq, FOX_WIDTH), lambda b, i: (b, 0, 0)),
            pl.BlockSpec((1, tq, LANES), lambda b, i: (b, i, 0)),
            pl.BlockSpec((1, seq // tk, 16, tk), lambda b, i: (b, 0, 0, 0)),
        ],
        out_specs=pl.BlockSpec((1, tq, FOX_WIDTH), lambda b, i: (b, i, 0)),
        out_shape=jax.ShapeDtypeStruct((bsz, seq, FOX_WIDTH), BF16),
        scratch_shapes=[pltpu.VMEM((PAIRS, tq, 2 * LANES), F32),
                        pltpu.VMEM((FOX_HEADS, tq, LANES), BF16),
                        pltpu.VMEM((FOX_HEADS, tq, LANES), F32)],
        compiler_params=pltpu.CompilerParams(
            dimension_semantics=("arbitrary", "arbitrary"), vmem_limit_bytes=VMEM_LIMIT),
        name="fox",
    )(flag, bound, q, k, v, ccol, crow)


def _unit_tri_inverses(mats, bs, passes):
    c, w = mats[0].shape
    n = range(len(mats))
    lane_cache = {}

    def lane_ids(s):
        if s not in lane_cache:
            lane_cache[s] = lax.broadcasted_iota(jnp.int32, (s, w), 1)
        return lane_cache[s]

    def terms(x):
        return _split_bf16(x, passes)

    def dot_terms(a_t, b_t):
        out = None
        for i, ai in enumerate(a_t):
            for j, bj in enumerate(b_t):
                if i + j < max(len(a_t), len(b_t)):
                    t = lax.dot_general(ai, bj, _NN, preferred_element_type=F32)
                    out = t if out is None else out + t
        return out

    def block_rows(x, s, offset):
        lb = jnp.right_shift(lane_ids(s), s.bit_length() - 1)
        zero = jnp.zeros((s, w), BF16)
        keep = [lb == j for j in range(w // s)]
        out = []
        for t in terms(x):
            rows = []
            for j in range(w // s):
                if offset and j % 2 == 0:
                    rows.append(zero)
                else:
                    rows.append(jnp.where(keep[j - offset], t, zero))
            out.append(jnp.concatenate(rows, axis=0))
        return out

    def mm(a, b_terms):
        return dot_terms(terms(a), b_terms)

    s = bs
    sh = s.bit_length() - 1
    in_mat = jnp.bitwise_and(lane_ids(s), c - 1)
    q = []
    for a2 in mats:
        d = a2[0:s, :]
        for r in range(1, c // s):
            d = jnp.where(jnp.right_shift(in_mat, sh) == r, a2[r * s:(r + 1) * s, :], d)
        q.append(d)
    p = [mm(q[i], block_rows(q[i], s, 0)) for i in n]
    for _ in range(s.bit_length() - 3):
        both = [mm(jnp.concatenate([q[i], p[i]], axis=0), block_rows(p[i], s, 0)) for i in n]
        q = [q[i] + p[i] + both[i][0:s] for i in n]
        p = [both[i][s:2 * s] for i in n]
    q = [q[i] + p[i] + mm(q[i], block_rows(p[i], s, 0)) for i in n]

    while s < c:
        sh = s.bit_length() - 1
        lane = lane_ids(s)
        first = jnp.bitwise_and(lane, s) == 0
        pair_id = jnp.right_shift(jnp.bitwise_and(lane, c - 1), sh + 1)
        l21 = []
        for a2 in mats:
            z = jnp.zeros((s, w), F32)
            for m in range(c // (2 * s)):
                rows = a2[(2 * m + 1) * s:(2 * m + 2) * s, :]
                z = jnp.where(pair_id == m, jnp.where(first, rows, 0.0), z)
            l21.append(z)
        x = [l21[i] + mm(l21[i], block_rows(q[i], s, 0)) for i in n]
        t21 = [x[i] + mm(q[i], block_rows(x[i], s, 1)) for i in n]
        q = [jnp.concatenate([jnp.where(first, q[i], 0.0), jnp.where(first, t21[i], q[i])], axis=0)
             for i in n]
        s *= 2
    return q


def _head_sums(x, e, pa):
    wd = e.shape[0]
    return jnp.concatenate(
        [_mm(x[:, j:j + wd], e, pa, 1) for j in range(0, x.shape[1], wd)], axis=1)


def _rwkv_kernel(rw_ref, w0_ref, w2_ref, a0_ref, a2_ref, g2_ref, kk_ref, ka_ref,
                 rk_ref, lg_ref, lb_ref, e_ref, tri_ref, o_ref, s_ref, *, prec, nb):
    cch = CHUNK
    rows = nb * cch
    w = RWKV_WIDTH
    c = pl.program_id(1)
    pg, pd, ps = prec
    e = e_ref[...]

    @pl.when(c == 0)
    def _():
        s_ref[...] = jnp.zeros_like(s_ref)

    lane = lax.broadcasted_iota(jnp.int32, (cch, LANES), 1)
    low = lane < HEAD_DIM
    ri = lax.broadcasted_iota(jnp.int32, (cch, cch), 0)
    ci = lax.broadcasted_iota(jnp.int32, (cch, cch), 1)
    strict = ci < ri
    incl = ci <= ri
    blockdiag = (ri < HEAD_DIM) == (ci < HEAD_DIM)
    sl = [slice(LANES * hp, LANES * (hp + 1)) for hp in range(PAIRS)]
    rs = [slice(cch * i, cch * (i + 1)) for i in range(nb)]
    combos = [(i, hp) for i in range(nb) for hp in range(PAIRS)]
    n = range(len(combos))

    def halves(z):
        zero = jnp.zeros_like(z)
        return jnp.concatenate([jnp.where(low, z, zero), jnp.where(low, zero, z)], axis=0)

    def shifted(lo, hi):
        return rw_ref[:, :, lo:hi].reshape(rows, hi - lo)

    wa = shifted(OFF_LORA, OFF_GDN)
    gdn = shifted(OFF_GDN, RW_COLS)
    wlog = _log_sigmoid(w0_ref[...] + _mm(jnp.tanh(wa), w2_ref[...])) - 0.5
    logd = -jnp.exp(wlog)
    a = _sigmoid(a0_ref[...] + _mm(wa, a2_ref[...]))
    g = _mm(_sigmoid(gdn), g2_ref[...])
    tri = tri_ref[...]
    cum = jnp.concatenate([_mm(tri, logd[rs[i]], 1, 2) for i in range(nb)], axis=0)
    lasts = [cum[cch * (i + 1) - 1:cch * (i + 1), :] for i in range(nb)]
    clast = jnp.concatenate([jnp.broadcast_to(z, (cch, w)) for z in lasts], axis=0)
    pc = [jnp.exp(z) for z in lasts]
    k = shifted(w, 2 * w)
    kk = k * kk_ref[...]
    kk = kk * lax.rsqrt(jnp.maximum(_head_sums(kk * kk, e, 2), L2_EPS * L2_EPS))
    k = k * (1.0 + (a - 1.0) * ka_ref[...])
    avec = -kk
    bvec = kk * a
    einv = jnp.exp(-cum)
    at = (avec * jnp.exp(cum - logd)).astype(BF16)
    bt = (bvec * einv).astype(BF16)
    kt = (k * einv).astype(BF16)
    etail = jnp.exp(clast - cum)
    bh = (bvec * etail).astype(BF16)
    kh = (k * etail).astype(BF16)
    r = shifted(0, w)
    rt = (r * jnp.exp(cum)).astype(BF16)
    bonus = _head_sums(r * k * rk_ref[...], e, 1)
    v = shifted(2 * w, 3 * w)
    vb = v.astype(BF16)
    bv = bonus * v

    def blk(z, j):
        i, hp = combos[j]
        return z[rs[i], sl[hp]]

    aab, aak, arbk = [], [], []
    for j in n:
        rb = jnp.concatenate([blk(bt, j), blk(kt, j)], axis=0)
        blocks = []
        for hh in range(2):
            sel = low if hh == 0 else jnp.logical_not(low)
            zero = jnp.zeros((cch, LANES), BF16)
            la = jnp.concatenate([jnp.where(sel, blk(at, j), zero),
                                  jnp.where(sel, blk(rt, j), zero)], axis=0)
            blocks.append(_mm(la, rb, pg, pg, dims=_NT))
        aab.append(jnp.concatenate(
            [jnp.where(strict, gm[0:cch, 0:cch], 0.0) for gm in blocks], axis=1))
        aak.append(jnp.concatenate(
            [jnp.where(strict, gm[0:cch, cch:2 * cch], 0.0) for gm in blocks], axis=1))
        arbk.append(jnp.concatenate(
            [jnp.where(incl, gm[cch:2 * cch, 0:cch], 0.0) for gm in blocks]
            + [jnp.where(incl, gm[cch:2 * cch, cch:2 * cch], 0.0) for gm in blocks], axis=1))
    qm = _unit_tri_inverses(aab, INV_BLOCK, pd)

    sp = [s_ref[i, hp] for i, hp in combos]
    vst = [halves(blk(vb, j)) for j in n]
    rhs = [_mm(blk(at, j), sp[j], ps, ps, dims=_NT) + _mm(aak[j], vst[j], ps, ps) for j in n]
    u = [rhs[j] + _mm(qm[j], halves(rhs[j]), ps, ps) for j in n]
    ys = [_mm(blk(rt, j), sp[j], ps, ps, dims=_NT)
          + _mm(arbk[j], jnp.concatenate([halves(u[j]).astype(BF16), vst[j]], axis=0), ps, ps)
          for j in n]
    for j in n:
        i, hp = combos[j]
        uv = jnp.concatenate([u[j], blk(vb, j).astype(F32)], axis=0)
        bk = jnp.concatenate([blk(bh, j), blk(kh, j)], axis=0)
        upd = _mm(uv, bk, ps, ps, dims=_TN)
        s_ref[i, hp] = sp[j] * pc[i][:, sl[hp]] + jnp.where(blockdiag, upd, 0.0)

    y = jnp.concatenate(
        [jnp.concatenate(ys[PAIRS * i:PAIRS * (i + 1)], axis=1) for i in range(nb)], axis=0)
    inv_n = 1.0 / HEAD_DIM
    mean = _head_sums(y, e, 1) * inv_n
    d = y - mean
    var = _head_sums(d * d, e, 1) * inv_n
    yn = d * lax.rsqrt(var + GN_EPS) * lg_ref[...] + lb_ref[...]
    o_ref[...] = ((yn + bv) * g).astype(BF16).reshape(nb, cch, w)


def _rwkv(rw, w0, w2, a0, a2, g2, k_k, k_a, r_k, lnx_g, lnx_b, e, prec):
    bsz, seq, _ = rw.shape
    cch = CHUNK
    nb = RWKV_SEQS if bsz % RWKV_SEQS == 0 else 1
    row = lax.broadcasted_iota(jnp.int32, (cch, cch), 0)
    col = lax.broadcasted_iota(jnp.int32, (cch, cch), 1)
    tri = (col <= row).astype(BF16)
    const = lambda shape: pl.BlockSpec(shape, lambda b, t: (0,) * len(shape))
    args = (w0, w2, a0, a2, g2, k_k, k_a, r_k, lnx_g, lnx_b, e, tri)
    return pl.pallas_call(
        functools.partial(_rwkv_kernel, prec=prec, nb=nb),
        grid=(bsz // nb, seq // cch),
        in_specs=[pl.BlockSpec((nb, cch, RW_COLS), lambda b, t: (b, t, 0))]
        + [const(a.shape) for a in args],
        out_specs=pl.BlockSpec((nb, cch, RWKV_WIDTH), lambda b, t: (b, t, 0)),
        out_shape=jax.ShapeDtypeStruct((bsz, seq, RWKV_WIDTH), BF16),
        scratch_shapes=[pltpu.VMEM((nb, PAIRS, LANES, LANES), F32)],
        compiler_params=pltpu.CompilerParams(
            dimension_semantics=("arbitrary", "arbitrary"), vmem_limit_bytes=VMEM_LIMIT),
        name="rwkv",
    )(rw, *args)


def _out_ffn_kernel(of_ref, or_ref, x_ref, mod_ref, g2_ref, wt_ref, wb_ref, wg_ref, wu_ref, wd_ref,
                    o_ref, *, slabs):
    mod = mod_ref[0]
    gt1, sh2, sc2, gt2 = mod[2:3, :], mod[3:4, :], mod[4:5, :], mod[5:6, :]
    mix = _mm(of_ref[0], wt_ref[...]) + _mm(or_ref[0], wb_ref[...])
    x1 = x_ref[0] + gt1 * mix
    ms = jnp.mean(x1 * x1, axis=-1, keepdims=True)
    y = x1 * lax.rsqrt(ms + RMS_EPS) * g2_ref[...]
    h2 = (y * (1.0 + sc2) + sh2).astype(BF16)
    acc = None
    for lo, hi in slabs:
        gate = _mm(h2, wg_ref[:, lo:hi])
        up = _mm(h2, wu_ref[:, lo:hi])
        act = (gate * _sigmoid(gate) * up).astype(BF16)
        part = _mm(act, wd_ref[lo:hi, :])
        acc = part if acc is None else acc + part
    o_ref[0] = x1 + gt2 * acc


def _out_ffn(o_fox, o_rwkv, x, mod, g2, w_top, w_bot, wg, wu, wd, tm):
    bsz, seq, d = x.shape
    dff = wg.shape[1]
    tiles = dff // MXU_DIM if dff % MXU_DIM == 0 else 1
    cut = (tiles // 2) * (dff // tiles)
    slabs = ((0, cut), (cut, dff)) if cut else ((0, dff),)
    resident = lambda shape: pl.BlockSpec(shape, lambda b, t: (0,) * len(shape),
                                          pipeline_mode=pl.Buffered(1))
    return pl.pallas_call(
        functools.partial(_out_ffn_kernel, slabs=slabs),
        grid=(bsz, seq // tm),
        in_specs=[
            pl.BlockSpec((1, tm, FOX_WIDTH), lambda b, t: (b, t, 0)),
            pl.BlockSpec((1, tm, RWKV_WIDTH), lambda b, t: (b, t, 0)),
            pl.BlockSpec((1, tm, d), lambda b, t: (b, t, 0)),
            pl.BlockSpec((1, 6, d), lambda b, t: (b, 0, 0)),
            resident((1, d)), resident(w_top.shape), resident(w_bot.shape),
            resident(wg.shape), resident(wu.shape), resident(wd.shape),
        ],
        out_specs=pl.BlockSpec((1, tm, d), lambda b, t: (b, t, 0)),
        out_shape=jax.ShapeDtypeStruct((bsz, seq, d), F32),
        compiler_params=pltpu.CompilerParams(
            dimension_semantics=("arbitrary", "arbitrary"), vmem_limit_bytes=VMEM_LIMIT),
        name="out_ffn",
    )(o_fox, o_rwkv, x, mod, g2, w_top, w_bot, wg, wu, wd)


def _pad_cols(w, n):
    return jnp.pad(w, ((0, 0), (0, n - w.shape[1])))


def _pad_rows(w, n):
    return jnp.pad(w, ((0, n - w.shape[0]), (0, 0)))


def _layer(x, mod, norm1_g, norm2_g, w_in, fox_f_bias, fox_q_gain, fox_k_gain, rwkv_mu, rwkv_w0,
           rwkv_w2, rwkv_a0, rwkv_a2, rwkv_g2, rwkv_k_k, rwkv_k_a, rwkv_r_k, rwkv_lnx_g,
           rwkv_lnx_b, w_out, ffn_w_gate, ffn_w_up, ffn_w_down, *, tm, tq, tk, prec):
    bsz, seq, d = x.shape
    w = RWKV_WIDTH
    nfox = 3 * FOX_WIDTH + FOX_HEADS

    w_bf = w_in.astype(BF16)
    wqkv = wf = w_bf
    wr = w_in[:, nfox:]
    assert 3 * w == OFF_LORA and DECAY_LORA + A_LORA == OFF_GDN - OFF_LORA
    wrw = _pad_cols(wr, RW_COLS).astype(BF16)
    mu_p = _pad_cols(rwkv_mu.reshape(1, -1), RW_COLS)
    fbrow = _pad_cols(fox_f_bias.reshape(1, -1), LANES)
    qg = jnp.tile(fox_q_gain, (1, 1)).reshape(1, FOX_WIDTH)
    kg = fox_k_gain.reshape(1, FOX_WIDTH)
    hi = lax.broadcasted_iota(jnp.int32, (MXU_DIM, MXU_DIM), 0) // HEAD_DIM
    hj = lax.broadcasted_iota(jnp.int32, (MXU_DIM, MXU_DIM), 1) // HEAD_DIM
    e = (hi == hj).astype(BF16)

    q, k, v, ccol, crow, rw = _in_proj(
        x, mod, norm1_g.reshape(1, d), wqkv, wf, wrw, fbrow, qg, kg, e, mu_p, tm, tk)
    bound = (1.05 * HEAD_DIM ** 0.5) * jnp.max(jnp.abs(fox_q_gain)) * jnp.max(jnp.abs(fox_k_gain))
    flag = (bound <= FOX_BOUND_MAX).astype(jnp.int32)
    o_fox = _fox(flag.reshape(1), bound.astype(F32).reshape(1), q, k, v, ccol, crow, tq, tk)
    o_rwkv = _rwkv(
        rw, rwkv_w0.reshape(1, w), _pad_rows(rwkv_w2, LANES).astype(BF16),
        rwkv_a0.reshape(1, w), jnp.pad(rwkv_a2, ((DECAY_LORA, 0), (0, 0))).astype(BF16),
        _pad_rows(rwkv_g2, RW_COLS - OFF_GDN).astype(BF16), rwkv_k_k.reshape(1, w), rwkv_k_a.reshape(1, w),
        rwkv_r_k.reshape(1, w), rwkv_lnx_g.reshape(1, w), rwkv_lnx_b.reshape(1, w), e, prec)
    wo = w_out.astype(BF16)
    return _out_ffn(o_fox, o_rwkv, x, mod, norm2_g.reshape(1, d), wo[0:FOX_WIDTH], wo[FOX_WIDTH:],
                    ffn_w_gate.astype(BF16), ffn_w_up.astype(BF16), ffn_w_down.astype(BF16),
                    tm)


def kernel(x, c, ada_w, ada_b, norm1_g, norm2_g, w_in, fox_f_bias, fox_q_gain, fox_k_gain, rwkv_mu,
           rwkv_w0, rwkv_w2, rwkv_a0, rwkv_a2, rwkv_g2, rwkv_k_k, rwkv_k_a, rwkv_r_k, rwkv_lnx_g,
           rwkv_lnx_b, w_out, ffn_w_gate, ffn_w_up, ffn_w_down):
    bsz, seq, d = x.shape
    depth = ada_w.shape[0]
    tm = min(ROW_TILE, seq)
    tk = min(KEY_TILE, seq)
    tq = min(QUERY_TILE, seq)
    assert seq % tm == 0 and seq % tq == 0 and tq % tk == 0 and seq % CHUNK == 0
    for l in range(depth):
        mod = _ada(c, ada_w, ada_b, l).reshape(bsz, 6, d)
        x = _layer(x, mod, norm1_g[l], norm2_g[l], w_in[l], fox_f_bias[l], fox_q_gain[l],
                   fox_k_gain[l], rwkv_mu[l], rwkv_w0[l], rwkv_w2[l], rwkv_a0[l], rwkv_a2[l],
                   rwkv_g2[l], rwkv_k_k[l], rwkv_k_a[l], rwkv_r_k[l], rwkv_lnx_g[l],
                   rwkv_lnx_b[l], w_out[l], ffn_w_gate[l], ffn_w_up[l], ffn_w_down[l],
                   tm=tm, tq=tq, tk=tk, prec=RWKV_PASSES)
    return x
```

```python
import functools

import jax
import jax.numpy as jnp
from jax import lax
from jax.experimental import pallas as pl
from jax.experimental.pallas import tpu as pltpu

F32 = jnp.float32
BF16 = jnp.bfloat16

HEAD_DIM = 64
FOX_HEADS = 8
RWKV_HEADS = 8
FOX_WIDTH = FOX_HEADS * HEAD_DIM
RWKV_WIDTH = RWKV_HEADS * HEAD_DIM
DECAY_LORA = 64
A_LORA = 64
GATE_LORA = 160
RMS_EPS = 1e-6
GN_EPS = 64e-5
L2_EPS = 1e-12

LANES = 128
MXU_DIM = 256
PAIRS = FOX_HEADS // 2
RWKV_SEQS = 8
INV_BLOCK = 64
RW_COLS = 1920
OFF_LORA, OFF_GDN = 1536, 1664
CHUNK = 128
NEG_BIG = -1e30
RWKV_PASSES = (1, 1, 1)
FOX_BOUND_MAX = 30.0
ROW_TILE = 512
QUERY_TILE = 512
KEY_TILE = 256
VMEM_LIMIT = 56 * 1024 * 1024


def _split_bf16(x, n):
    if x.dtype == BF16:
        return [x]
    parts = []
    r = x
    for i in range(n):
        p = r.astype(BF16)
        parts.append(p)
        if i < n - 1:
            r = r - p.astype(F32)
    return parts


_NN = (((1,), (0,)), ((), ()))
_NT = (((1,), (1,)), ((), ()))
_TN = (((0,), (0,)), ((), ()))


def _mm(a, b, pa=1, pb=1, dims=_NN):
    a_parts = _split_bf16(a, pa)
    b_parts = _split_bf16(b, pb)
    order = max(len(a_parts), len(b_parts))
    out = None
    for i, ai in enumerate(a_parts):
        for j, bj in enumerate(b_parts):
            if i + j >= order:
                continue
            t = lax.dot_general(ai, bj, dims, preferred_element_type=F32)
            out = t if out is None else out + t
    return out


def _log_sigmoid(z):
    return jnp.minimum(z, 0.0) - jnp.log(1.0 + jnp.exp(-jnp.abs(z)))


def _sigmoid(z):
    return 1.0 / (1.0 + jnp.exp(-z))


def _ada_kernel(c_ref, w_ref, b_ref, o_ref):
    c = c_ref[...]
    cond = c * _sigmoid(c)
    o_ref[...] = _mm(cond, w_ref[0], 2, 2) + b_ref[0]


def _ada(c, ada_w, ada_b, layer):
    bsz, d = c.shape
    depth, _, n = ada_w.shape
    tn = 512
    return pl.pallas_call(
        _ada_kernel,
        grid=(n // tn,),
        in_specs=[
            pl.BlockSpec((bsz, d), lambda j: (0, 0)),
            pl.BlockSpec((1, d, tn), lambda j: (layer, 0, j)),
            pl.BlockSpec((1, 1, tn), lambda j: (layer, 0, j)),
        ],
        out_specs=pl.BlockSpec((bsz, tn), lambda j: (0, j)),
        out_shape=jax.ShapeDtypeStruct((bsz, n), F32),
        compiler_params=pltpu.CompilerParams(dimension_semantics=("arbitrary",)),
        name="ada",
    )(c, ada_w, ada_b.reshape(depth, 1, n))


def _in_proj_kernel(x_ref, mod_ref, g1_ref, wqkv_ref, wf_ref, wrw_ref,
                    fbrow_ref, qg_ref, kg_ref, e_ref, trilo_ref, mu_ref,
                    q_ref, k_ref, v_ref, ccol_ref, crow_ref, rw_ref,
                    carry_row, prev_ref, *, tm, tk):
    t = pl.program_id(1)

    @pl.when(t == 0)
    def _():
        carry_row[...] = jnp.zeros_like(carry_row)
        prev_ref[...] = jnp.zeros_like(prev_ref)

    x = x_ref[0]
    mod = mod_ref[0]
    sh1 = mod[0:1, :]
    sc1 = mod[1:2, :]
    ms = jnp.mean(x * x, axis=-1, keepdims=True)
    y = x * lax.rsqrt(ms + RMS_EPS) * g1_ref[...]
    hb = (y * (1.0 + sc1) + sh1).astype(BF16)

    qkv = _mm(hb, wqkv_ref[...])
    e = e_ref[...]
    q = qkv[:, 0:FOX_WIDTH]
    k = qkv[:, FOX_WIDTH:2 * FOX_WIDTH]
    qms = _head_sums(q * q, e, 1) * (1.0 / HEAD_DIM)
    kms = _head_sums(k * k, e, 1) * (1.0 / HEAD_DIM)
    q_ref[0] = (q * lax.rsqrt(qms + RMS_EPS) * qg_ref[...] * (HEAD_DIM ** -0.5)).astype(BF16)
    k_ref[0] = (k * lax.rsqrt(kms + RMS_EPS) * kg_ref[...]).astype(BF16)
    v_ref[0] = qkv[:, 2 * FOX_WIDTH:3 * FOX_WIDTH].astype(BF16)

    rw = _mm(hb, wrw_ref[...])
    rolled = pltpu.roll(rw, 1, 0)
    first_row = lax.broadcasted_iota(jnp.int32, (8, 1), 0) == 0
    top = jnp.where(first_row, prev_ref[0:1, :], rolled[0:8])
    prev_ref[0:1, :] = rw[tm - 1:tm, :]
    rw_ref[0] = rw + (jnp.concatenate([top, rolled[8:tm]], axis=0) - rw) * mu_ref[...]

    lf = _log_sigmoid(_mm(hb, wf_ref[...]) + fbrow_ref[...])
    blocks = []
    carry = carry_row[...]
    for j in range(tm // LANES):
        cj = _mm(trilo_ref[...], lf[j * LANES:(j + 1) * LANES], 1, 3) + carry
        blocks.append(cj)
        carry = cj[LANES - 1:LANES, :]
    carry_row[...] = carry
    ccol_ref[0] = jnp.concatenate(blocks, axis=0)
    for j, cj in enumerate(blocks):
        jj, off = divmod(j * LANES, tk)
        crow_ref[0, jj, :, off:off + LANES] = cj.T[0:16, :]


def _in_proj(x, mod, g1, wqkv, wf, wrw, fbrow, qg, kg, e, mu, tm, tk):
    bsz, seq, d = x.shape
    nt = seq // tm
    row = lax.broadcasted_iota(jnp.int32, (LANES, LANES), 0)
    col = lax.broadcasted_iota(jnp.int32, (LANES, LANES), 1)
    trilo = (col <= row).astype(BF16)
    const = lambda shape: pl.BlockSpec(shape, lambda b, t: (0,) * len(shape))
    kern = functools.partial(_in_proj_kernel, tm=tm, tk=tk)
    return pl.pallas_call(
        kern,
        grid=(bsz, nt),
        in_specs=[
            pl.BlockSpec((1, tm, d), lambda b, t: (b, t, 0)),
            pl.BlockSpec((1, 6, d), lambda b, t: (b, 0, 0)),
            const((1, d)),
            pl.BlockSpec((d, 3 * FOX_WIDTH), lambda b, t: (0, 0)),
            pl.BlockSpec((d, LANES), lambda b, t: (0, 3 * FOX_WIDTH // LANES)),
            const(wrw.shape),
            const(fbrow.shape), const(qg.shape), const(kg.shape),
            const(e.shape), const(trilo.shape), const(mu.shape),
        ],
        out_specs=[
            pl.BlockSpec((1, tm, FOX_WIDTH), lambda b, t: (b, t, 0)),
            pl.BlockSpec((1, tm, FOX_WIDTH), lambda b, t: (b, t, 0)),
            pl.BlockSpec((1, tm, FOX_WIDTH), lambda b, t: (b, t, 0)),
            pl.BlockSpec((1, tm, LANES), lambda b, t: (b, t, 0)),
            pl.BlockSpec((1, tm // tk, 16, tk), lambda b, t: (b, t, 0, 0)),
            pl.BlockSpec((1, tm, RW_COLS), lambda b, t: (b, t, 0)),
        ],
        out_shape=[
            jax.ShapeDtypeStruct((bsz, seq, FOX_WIDTH), BF16),
            jax.ShapeDtypeStruct((bsz, seq, FOX_WIDTH), BF16),
            jax.ShapeDtypeStruct((bsz, seq, FOX_WIDTH), BF16),
            jax.ShapeDtypeStruct((bsz, seq, LANES), F32),
            jax.ShapeDtypeStruct((bsz, seq // tk, 16, tk), F32),
            jax.ShapeDtypeStruct((bsz, seq, RW_COLS), F32),
        ],
        scratch_shapes=[pltpu.VMEM((1, LANES), F32), pltpu.VMEM((8, RW_COLS), F32)],
        compiler_params=pltpu.CompilerParams(
            dimension_semantics=("arbitrary", "arbitrary"), vmem_limit_bytes=VMEM_LIMIT),
        name="in_proj",
    )(x, mod, g1, wqkv, wf, wrw, fbrow, qg, kg, e, trilo, mu)


def _fox_bounded(bound, q_ref, k_ref, v_ref, ccol_ref, crow_ref, o_ref, acc_ref, qm_ref, cb_ref,
                 *, tq, tk):
    i = pl.program_id(1)
    nsub = tq // tk
    lane = lax.broadcasted_iota(jnp.int32, (tq, LANES), 1)
    low = lane < HEAD_DIM
    klow = lax.broadcasted_iota(jnp.int32, (tk, LANES), 1) < HEAD_DIM
    one_lo = jnp.where(klow, 1.0, 0.0).astype(BF16)
    one_hi = jnp.where(klow, 0.0, 1.0).astype(BF16)

    ccol = ccol_ref[0] - bound
    for hp in range(PAIRS):
        q2 = q_ref[0, :, LANES * hp:LANES * (hp + 1)]
        zero = jnp.zeros_like(q2)
        qm_ref[2 * hp] = jnp.where(low, q2, zero)
        qm_ref[2 * hp + 1] = jnp.where(low, zero, q2)
        for hh in range(2):
            h = 2 * hp + hh
            cb_ref[h] = jnp.broadcast_to(ccol[:, h:h + 1], (tq, LANES))
    acc_ref[...] = jnp.zeros_like(acc_ref)

    def step(j, sub):
        r0 = 0 if sub is None else sub * tk
        rows = tq - r0
        ks = pl.multiple_of(j * tk, tk)
        if sub is not None:
            causal = (lax.broadcasted_iota(jnp.int32, (rows, tk), 1)
                      <= lax.broadcasted_iota(jnp.int32, (rows, tk), 0))

        def qk(hp):
            k2 = k_ref[0, pl.ds(ks, tk), LANES * hp:LANES * (hp + 1)]
            return [_mm(qm_ref[2 * hp + hh, r0:tq, :], k2, dims=_NT) for hh in range(2)]

        def softmax_numerators(hp, scores):
            probs = []
            for hh in range(2):
                h = 2 * hp + hh
                cb = cb_ref[h, r0:tq, :]
                bias = jnp.concatenate([cb] * (tk // LANES), axis=1) - crow_ref[0, j, h:h + 1, :]
                s = scores[hh] + bias
                if sub is not None:
                    s = jnp.where(causal, s, NEG_BIG)
                probs.append(jnp.exp(s).astype(BF16))
            return probs

        def pv(hp, probs):
            v2 = v_ref[0, pl.ds(ks, tk), LANES * hp:LANES * (hp + 1)]
            zero = jnp.zeros_like(v2)
            vaug = jnp.concatenate([
                jnp.concatenate([jnp.where(klow, v2, zero), one_lo], axis=1),
                jnp.concatenate([jnp.where(klow, zero, v2), one_hi], axis=1)], axis=0)
            acc_ref[hp, r0:tq, :] += _mm(jnp.concatenate(probs, axis=1), vaug)

        scores, probs = {}, {}
        for t in range(PAIRS + 2):
            if t < PAIRS:
                scores[t] = qk(t)
            if 0 <= t - 1 < PAIRS:
                probs[t - 1] = softmax_numerators(t - 1, scores.pop(t - 1))
            if 0 <= t - 2 < PAIRS:
                pv(t - 2, probs.pop(t - 2))

    def body(jo, carry):
        for m in range(nsub):
            step(jo * nsub + m, None)
        return carry

    lax.fori_loop(0, i, body, 0)
    for m in range(nsub):
        step(i * nsub + m, m)
    for hp in range(PAIRS):
        a = acc_ref[hp]
        o = a[:, 0:LANES] / a[:, LANES:2 * LANES]
        o_ref[0, :, LANES * hp:LANES * (hp + 1)] = o.astype(BF16)


def _fox_kernel(flag_ref, bound_ref, q_ref, k_ref, v_ref, ccol_ref, crow_ref, o_ref, acc_ref,
                qm_ref, cb_ref, *, tq, tk):
    @pl.when(flag_ref[0] == 1)
    def _():
        _fox_bounded(bound_ref[0], q_ref, k_ref, v_ref, ccol_ref, crow_ref, o_ref, acc_ref,
                     qm_ref, cb_ref, tq=tq, tk=tk)

    @pl.when(flag_ref[0] == 0)
    def _():
        _fox_running_max(q_ref, k_ref, v_ref, ccol_ref, crow_ref, o_ref, tq=tq, tk=tk)


def _fox_running_max(q_ref, k_ref, v_ref, ccol_ref, crow_ref, o_ref, *, tq, tk):
    i = pl.program_id(1)
    nsub = tq // tk
    lane = lax.broadcasted_iota(jnp.int32, (tq, LANES), 1)
    low = lane < HEAD_DIM
    row = lax.broadcasted_iota(jnp.int32, (tq, tk), 0)
    col = lax.broadcasted_iota(jnp.int32, (tq, tk), 1)
    ccol = ccol_ref[0]

    for hp in range(PAIRS):
        ls = slice(LANES * hp, LANES * (hp + 1))
        q2 = q_ref[0, :, ls]
        zero = jnp.zeros_like(q2)
        qm = (jnp.where(low, q2, zero), jnp.where(low, zero, q2))
        cc = tuple(ccol[:, 2 * hp + hh:2 * hp + hh + 1] for hh in range(2))

        def step(j, carry, sub, ls=ls, qm=qm, cc=cc, hp=hp):
            ks = pl.multiple_of(j * tk, tk)
            k2 = k_ref[0, pl.ds(ks, tk), ls]
            v2 = v_ref[0, pl.ds(ks, tk), ls]
            new = []
            for hh in range(2):
                m, l, acc = carry[hh]
                s = _mm(qm[hh], k2, dims=_NT)
                cr = crow_ref[0, j, 2 * hp + hh:2 * hp + hh + 1, :]
                s = s + (cc[hh] - cr)
                if sub is not None:
                    s = jnp.where(col + sub * tk <= row, s, NEG_BIG)
                m_new = jnp.maximum(m, jnp.max(s, axis=1, keepdims=True))
                alpha = jnp.exp(m - m_new)
                p = jnp.exp(s - m_new)
                l_new = alpha * l + jnp.sum(p, axis=1, keepdims=True)
                acc_new = alpha * acc + _mm(p.astype(BF16), v2)
                new.append((m_new, l_new, acc_new))
            return tuple(new)

        init_one = (jnp.full((tq, 1), NEG_BIG, F32), jnp.zeros((tq, 1), F32),
                    jnp.zeros((tq, LANES), F32))
        carry = lax.fori_loop(0, i * nsub, lambda j, c: step(j, c, None), (init_one, init_one))
        for sub in range(nsub):
            carry = step(i * nsub + sub, carry, sub)
        o0 = carry[0][2] / carry[0][1]
        o1 = carry[1][2] / carry[1][1]
        o_ref[0, :, ls] = jnp.where(low, o0, o1).astype(BF16)


def _fox(flag, bound, q, k, v, ccol, crow, tq, tk):
    bsz, seq, _ = q.shape
    kern = functools.partial(_fox_kernel, tq=tq, tk=tk)
    return pl.pallas_call(
        kern,
        grid=(bsz, seq // tq),
        in_specs=[
            pl.BlockSpec(memory_space=pltpu.SMEM),
            pl.BlockSpec(memory_space=pltpu.SMEM),
            pl.BlockSpec((1, tq, FOX_WIDTH), lambda b, i: (b, i, 0)),
            pl.BlockSpec((1, seq, FOX_WIDTH), lambda b, i: (b, 0, 0)),
            pl.BlockSpec((1, seq, FOX_WIDTH), lambda b, i: (b, 0, 0)),
            pl.BlockSpec((1, tq, LANES), lambda b, i: (b, i, 0)),
            pl.BlockSpec((1, seq // tk, 16, tk), lambda b, i: (b, 0, 0, 0)),
        ],
        out_specs=pl.BlockSpec((1, tq, FOX_WIDTH), lambda b, i: (b, i, 0)),
        out_shape=jax.ShapeDtypeStruct((bsz, seq, FOX_WIDTH), BF16),
        scratch_shapes=[pltpu.VMEM((PAIRS, tq, 2 * LANES), F32),
                        pltpu.VMEM((FOX_HEADS, tq, LANES), BF16),
                        pltpu.VMEM((FOX_HEADS, tq, LANES), F32)],
        compiler_params=pltpu.CompilerParams(
            dimension_semantics=("arbitrary", "arbitrary"), vmem_limit_bytes=VMEM_LIMIT),
        name="fox",
    )(flag, bound, q, k, v, ccol, crow)


def _unit_tri_inverses(mats, bs, passes):
    c, w = mats[0].shape
    n = range(len(mats))
    lane_cache = {}

    def lane_ids(s):
        if s not in lane_cache:
            lane_cache[s] = lax.broadcasted_iota(jnp.int32, (s, w), 1)
        return lane_cache[s]

    def terms(x):
        return _split_bf16(x, passes)

    def dot_terms(a_t, b_t):
        out = None
        for i, ai in enumerate(a_t):
            for j, bj in enumerate(b_t):
                if i + j < max(len(a_t), len(b_t)):
                    t = lax.dot_general(ai, bj, _NN, preferred_element_type=F32)
                    out = t if out is None else out + t
        return out

    def block_rows(x, s, offset):
        lb = jnp.right_shift(lane_ids(s), s.bit_length() - 1)
        zero = jnp.zeros((s, w), BF16)
        keep = [lb == j for j in range(w // s)]
        out = []
        for t in terms(x):
            rows = []
            for j in range(w // s):
                if offset and j % 2 == 0:
                    rows.append(zero)
                else:
                    rows.append(jnp.where(keep[j - offset], t, zero))
            out.append(jnp.concatenate(rows, axis=0))
        return out

    def mm(a, b_terms):
        return dot_terms(terms(a), b_terms)

    s = bs
    sh = s.bit_length() - 1
    in_mat = jnp.bitwise_and(lane_ids(s), c - 1)
    q = []
    for a2 in mats:
        d = a2[0:s, :]
        for r in range(1, c // s):
            d = jnp.where(jnp.right_shift(in_mat, sh) == r, a2[r * s:(r + 1) * s, :], d)
        q.append(d)
    p = [mm(q[i], block_rows(q[i], s, 0)) for i in n]
    for _ in range(s.bit_length() - 3):
        both = [mm(jnp.concatenate([q[i], p[i]], axis=0), block_rows(p[i], s, 0)) for i in n]
        q = [q[i] + p[i] + both[i][0:s] for i in n]
        p = [both[i][s:2 * s] for i in n]
    q = [q[i] + p[i] + mm(q[i], block_rows(p[i], s, 0)) for i in n]

    while s < c:
        sh = s.bit_length() - 1
        lane = lane_ids(s)
        first = jnp.bitwise_and(lane, s) == 0
        pair_id = jnp.right_shift(jnp.bitwise_and(lane, c - 1), sh + 1)
        l21 = []
        for a2 in mats:
            z = jnp.zeros((s, w), F32)
            for m in range(c // (2 * s)):
                rows = a2[(2 * m + 1) * s:(2 * m + 2) * s, :]
                z = jnp.where(pair_id == m, jnp.where(first, rows, 0.0), z)
            l21.append(z)
        x = [l21[i] + mm(l21[i], block_rows(q[i], s, 0)) for i in n]
        t21 = [x[i] + mm(q[i], block_rows(x[i], s, 1)) for i in n]
        q = [jnp.concatenate([jnp.where(first, q[i], 0.0), jnp.where(first, t21[i], q[i])], axis=0)
             for i in n]
        s *= 2
    return q


def _head_sums(x, e, pa):
    wd = e.shape[0]
    return jnp.concatenate(
        [_mm(x[:, j:j + wd], e, pa, 1) for j in range(0, x.shape[1], wd)], axis=1)


def _rwkv_kernel(rw_ref, w0_ref, w2_ref, a0_ref, a2_ref, g2_ref, kk_ref, ka_ref,
                 rk_ref, lg_ref, lb_ref, e_ref, tri_ref, o_ref, s_ref, *, prec, nb):
    cch = CHUNK
    rows = nb * cch
    w = RWKV_WIDTH
    c = pl.program_id(1)
    pg, pd, ps = prec
    e = e_ref[...]

    @pl.when(c == 0)
    def _():
        s_ref[...] = jnp.zeros_like(s_ref)

    lane = lax.broadcasted_iota(jnp.int32, (cch, LANES), 1)
    low = lane < HEAD_DIM
    ri = lax.broadcasted_iota(jnp.int32, (cch, cch), 0)
    ci = lax.broadcasted_iota(jnp.int32, (cch, cch), 1)
    strict = ci < ri
    incl = ci <= ri
    blockdiag = (ri < HEAD_DIM) == (ci < HEAD_DIM)
    sl = [slice(LANES * hp, LANES * (hp + 1)) for hp in range(PAIRS)]
    rs = [slice(cch * i, cch * (i + 1)) for i in range(nb)]
    combos = [(i, hp) for i in range(nb) for hp in range(PAIRS)]
    n = range(len(combos))

    def halves(z):
        zero = jnp.zeros_like(z)
        return jnp.concatenate([jnp.where(low, z, zero), jnp.where(low, zero, z)], axis=0)

    def shifted(lo, hi):
        return rw_ref[:, :, lo:hi].reshape(rows, hi - lo)

    wa = shifted(OFF_LORA, OFF_GDN)
    gdn = shifted(OFF_GDN, RW_COLS)
    wlog = _log_sigmoid(w0_ref[...] + _mm(jnp.tanh(wa), w2_ref[...])) - 0.5
    logd = -jnp.exp(wlog)
    a = _sigmoid(a0_ref[...] + _mm(wa, a2_ref[...]))
    g = _mm(_sigmoid(gdn), g2_ref[...])
    tri = tri_ref[...]
    cum = jnp.concatenate([_mm(tri, logd[rs[i]], 1, 2) for i in range(nb)], axis=0)
    lasts = [cum[cch * (i + 1) - 1:cch * (i + 1), :] for i in range(nb)]
    clast = jnp.concatenate([jnp.broadcast_to(z, (cch, w)) for z in lasts], axis=0)
    pc = [jnp.exp(z) for z in lasts]
    k = shifted(w, 2 * w)
    kk = k * kk_ref[...]
    kk = kk * lax.rsqrt(jnp.maximum(_head_sums(kk * kk, e, 2), L2_EPS * L2_EPS))
    k = k * (1.0 + (a - 1.0) * ka_ref[...])
    avec = -kk
    bvec = kk * a
    einv = jnp.exp(-cum)
    at = (avec * jnp.exp(cum - logd)).astype(BF16)
    bt = (bvec * einv).astype(BF16)
    kt = (k * einv).astype(BF16)
    etail = jnp.exp(clast - cum)
    bh = (bvec * etail).astype(BF16)
    kh = (k * etail).astype(BF16)
    r = shifted(0, w)
    rt = (r * jnp.exp(cum)).astype(BF16)
    bonus = _head_sums(r * k * rk_ref[...], e, 1)
    v = shifted(2 * w, 3 * w)
    vb = v.astype(BF16)
    bv = bonus * v

    def blk(z, j):
        i, hp = combos[j]
        return z[rs[i], sl[hp]]

    aab, aak, arbk = [], [], []
    for j in n:
        rb = jnp.concatenate([blk(bt, j), blk(kt, j)], axis=0)
        blocks = []
        for hh in range(2):
            sel = low if hh == 0 else jnp.logical_not(low)
            zero = jnp.zeros((cch, LANES), BF16)
            la = jnp.concatenate([jnp.where(sel, blk(at, j), zero),
                                  jnp.where(sel, blk(rt, j), zero)], axis=0)
            blocks.append(_mm(la, rb, pg, pg, dims=_NT))
        aab.append(jnp.concatenate(
            [jnp.where(strict, gm[0:cch, 0:cch], 0.0) for gm in blocks], axis=1))
        aak.append(jnp.concatenate(
            [jnp.where(strict, gm[0:cch, cch:2 * cch], 0.0) for gm in blocks], axis=1))
        arbk.append(jnp.concatenate(
            [jnp.where(incl, gm[cch:2 * cch, 0:cch], 0.0) for gm in blocks]
            + [jnp.where(incl, gm[cch:2 * cch, cch:2 * cch], 0.0) for gm in blocks], axis=1))
    qm = _unit_tri_inverses(aab, INV_BLOCK, pd)

    sp = [s_ref[i, hp] for i, hp in combos]
    vst = [halves(blk(vb, j)) for j in n]
    rhs = [_mm(blk(at, j), sp[j], ps, ps, dims=_NT) + _mm(aak[j], vst[j], ps, ps) for j in n]
    u = [rhs[j] + _mm(qm[j], halves(rhs[j]), ps, ps) for j in n]
    ys = [_mm(blk(rt, j), sp[j], ps, ps, dims=_NT)
          + _mm(arbk[j], jnp.concatenate([halves(u[j]).astype(BF16), vst[j]], axis=0), ps, ps)
          for j in n]
    for j in n:
        i, hp = combos[j]
        uv = jnp.concatenate([u[j], blk(vb, j).astype(F32)], axis=0)
        bk = jnp.concatenate([blk(bh, j), blk(kh, j)], axis=0)
        upd = _mm(uv, bk, ps, ps, dims=_TN)
        s_ref[i, hp] = sp[j] * pc[i][:, sl[hp]] + jnp.where(blockdiag, upd, 0.0)

    y = jnp.concatenate(
        [jnp.concatenate(ys[PAIRS * i:PAIRS * (i + 1)], axis=1) for i in range(nb)], axis=0)
    inv_n = 1.0 / HEAD_DIM
    mean = _head_sums(y, e, 1) * inv_n
    d = y - mean
    var = _head_sums(d * d, e, 1) * inv_n
    yn = d * lax.rsqrt(var + GN_EPS) * lg_ref[...] + lb_ref[...]
    o_ref[...] = ((yn + bv) * g).astype(BF16).reshape(nb, cch, w)


def _rwkv(rw, w0, w2, a0, a2, g2, k_k, k_a, r_k, lnx_g, lnx_b, e, prec):
    bsz, seq, _ = rw.shape
    cch = CHUNK
    nb = RWKV_SEQS if bsz % RWKV_SEQS == 0 else 1
    row = lax.broadcasted_iota(jnp.int32, (cch, cch), 0)
    col = lax.broadcasted_iota(jnp.int32, (cch, cch), 1)
    tri = (col <= row).astype(BF16)
    const = lambda shape: pl.BlockSpec(shape, lambda b, t: (0,) * len(shape))
    args = (w0, w2, a0, a2, g2, k_k, k_a, r_k, lnx_g, lnx_b, e, tri)
    return pl.pallas_call(
        functools.partial(_rwkv_kernel, prec=prec, nb=nb),
        grid=(bsz // nb, seq // cch),
        in_specs=[pl.BlockSpec((nb, cch, RW_COLS), lambda b, t: (b, t, 0))]
        + [const(a.shape) for a in args],
        out_specs=pl.BlockSpec((nb, cch, RWKV_WIDTH), lambda b, t: (b, t, 0)),
        out_shape=jax.ShapeDtypeStruct((bsz, seq, RWKV_WIDTH), BF16),
        scratch_shapes=[pltpu.VMEM((nb, PAIRS, LANES, LANES), F32)],
        compiler_params=pltpu.CompilerParams(
            dimension_semantics=("arbitrary", "arbitrary"), vmem_limit_bytes=VMEM_LIMIT),
        name="rwkv",
    )(rw, *args)


def _out_ffn_kernel(of_ref, or_ref, x_ref, mod_ref, g2_ref, wt_ref, wb_ref, wg_ref, wu_ref, wd_ref,
                    o_ref, *, slabs):
    mod = mod_ref[0]
    gt1, sh2, sc2, gt2 = mod[2:3, :], mod[3:4, :], mod[4:5, :], mod[5:6, :]
    mix = _mm(of_ref[0], wt_ref[...]) + _mm(or_ref[0], wb_ref[...])
    x1 = x_ref[0] + gt1 * mix
    ms = jnp.mean(x1 * x1, axis=-1, keepdims=True)
    y = x1 * lax.rsqrt(ms + RMS_EPS) * g2_ref[...]
    h2 = (y * (1.0 + sc2) + sh2).astype(BF16)
    acc = None
    pending = None
    for slab in tuple(slabs) + (None,):
        nxt = None
        if slab is not None:
            lo, hi = slab
            nxt = (lo, hi, _mm(h2, wg_ref[:, lo:hi]), _mm(h2, wu_ref[:, lo:hi]))
        if pending is not None:
            plo, phi, gate, up = pending
            act = (gate * _sigmoid(gate) * up).astype(BF16)
            part = _mm(act, wd_ref[plo:phi, :])
            acc = part if acc is None else acc + part
        pending = nxt
    o_ref[0] = x1 + gt2 * acc


def _out_ffn(o_fox, o_rwkv, x, mod, g2, w_top, w_bot, wg, wu, wd, tm):
    bsz, seq, d = x.shape
    dff = wg.shape[1]
    tiles = dff // MXU_DIM if dff % MXU_DIM == 0 else 1
    per = dff // tiles
    cuts = list(range(0, dff, 3 * per)) + [dff]
    slabs = tuple(zip(cuts[:-1], cuts[1:]))
    resident = lambda shape: pl.BlockSpec(shape, lambda b, t: (0,) * len(shape),
                                          pipeline_mode=pl.Buffered(1))
    return pl.pallas_call(
        functools.partial(_out_ffn_kernel, slabs=slabs),
        grid=(bsz, seq // tm),
        in_specs=[
            pl.BlockSpec((1, tm, FOX_WIDTH), lambda b, t: (b, t, 0)),
            pl.BlockSpec((1, tm, RWKV_WIDTH), lambda b, t: (b, t, 0)),
            pl.BlockSpec((1, tm, d), lambda b, t: (b, t, 0)),
            pl.BlockSpec((1, 6, d), lambda b, t: (b, 0, 0)),
            resident((1, d)), resident(w_top.shape), resident(w_bot.shape),
            resident(wg.shape), resident(wu.shape), resident(wd.shape),
        ],
        out_specs=pl.BlockSpec((1, tm, d), lambda b, t: (b, t, 0)),
        out_shape=jax.ShapeDtypeStruct((bsz, seq, d), F32),
        compiler_params=pltpu.CompilerParams(
            dimension_semantics=("arbitrary", "arbitrary"), vmem_limit_bytes=VMEM_LIMIT),
        name="out_ffn",
    )(o_fox, o_rwkv, x, mod, g2, w_top, w_bot, wg, wu, wd)


def _pad_cols(w, n):
    return jnp.pad(w, ((0, 0), (0, n - w.shape[1])))


def _pad_rows(w, n):
    return jnp.pad(w, ((0, n - w.shape[0]), (0, 0)))


def _layer(x, mod, norm1_g, norm2_g, w_in, fox_f_bias, fox_q_gain, fox_k_gain, rwkv_mu, rwkv_w0,
           rwkv_w2, rwkv_a0, rwkv_a2, rwkv_g2, rwkv_k_k, rwkv_k_a, rwkv_r_k, rwkv_lnx_g,
           rwkv_lnx_b, w_out, ffn_w_gate, ffn_w_up, ffn_w_down, *, tm, tq, tk, prec):
    bsz, seq, d = x.shape
    w = RWKV_WIDTH
    nfox = 3 * FOX_WIDTH + FOX_HEADS

    w_bf = w_in.astype(BF16)
    wqkv = wf = w_bf
    wr = w_in[:, nfox:]
    assert 3 * w == OFF_LORA and DECAY_LORA + A_LORA == OFF_GDN - OFF_LORA
    wrw = _pad_cols(wr, RW_COLS).astype(BF16)
    mu_p = _pad_cols(rwkv_mu.reshape(1, -1), RW_COLS)
    fbrow = _pad_cols(fox_f_bias.reshape(1, -1), LANES)
    qg = jnp.tile(fox_q_gain, (1, 1)).reshape(1, FOX_WIDTH)
    kg = fox_k_gain.reshape(1, FOX_WIDTH)
    hi = lax.broadcasted_iota(jnp.int32, (MXU_DIM, MXU_DIM), 0) // HEAD_DIM
    hj = lax.broadcasted_iota(jnp.int32, (MXU_DIM, MXU_DIM), 1) // HEAD_DIM
    e = (hi == hj).astype(BF16)

    q, k, v, ccol, crow, rw = _in_proj(
        x, mod, norm1_g.reshape(1, d), wqkv, wf, wrw, fbrow, qg, kg, e, mu_p, tm, tk)
    bound = (1.05 * HEAD_DIM ** 0.5) * jnp.max(jnp.abs(fox_q_gain)) * jnp.max(jnp.abs(fox_k_gain))
    flag = (bound <= FOX_BOUND_MAX).astype(jnp.int32)
    o_fox = _fox(flag.reshape(1), bound.astype(F32).reshape(1), q, k, v, ccol, crow, tq, tk)
    o_rwkv = _rwkv(
        rw, rwkv_w0.reshape(1, w), _pad_rows(rwkv_w2, LANES).astype(BF16),
        rwkv_a0.reshape(1, w), jnp.pad(rwkv_a2, ((DECAY_LORA, 0), (0, 0))).astype(BF16),
        _pad_rows(rwkv_g2, RW_COLS - OFF_GDN).astype(BF16), rwkv_k_k.reshape(1, w), rwkv_k_a.reshape(1, w),
        rwkv_r_k.reshape(1, w), rwkv_lnx_g.reshape(1, w), rwkv_lnx_b.reshape(1, w), e, prec)
    wo = w_out.astype(BF16)
    return _out_ffn(o_fox, o_rwkv, x, mod, norm2_g.reshape(1, d), wo[0:FOX_WIDTH], wo[FOX_WIDTH:],
                    ffn_w_gate.astype(BF16), ffn_w_up.astype(BF16), ffn_w_down.astype(BF16),
                    tm)


def kernel(x, c, ada_w, ada_b, norm1_g, norm2_g, w_in, fox_f_bias, fox_q_gain, fox_k_gain, rwkv_mu,
           rwkv_w0, rwkv_w2, rwkv_a0, rwkv_a2, rwkv_g2, rwkv_k_k, rwkv_k_a, rwkv_r_k, rwkv_lnx_g,
           rwkv_lnx_b, w_out, ffn_w_gate, ffn_w_up, ffn_w_down):
    bsz, seq, d = x.shape
    depth = ada_w.shape[0]
    tm = min(ROW_TILE, seq)
    tk = min(KEY_TILE, seq)
    tq = min(QUERY_TILE, seq)
    assert seq % tm == 0 and seq % tq == 0 and tq % tk == 0 and seq % CHUNK == 0
    for l in range(depth):
        mod = _ada(c, ada_w, ada_b, l).reshape(bsz, 6, d)
        x = _layer(x, mod, norm1_g[l], norm2_g[l], w_in[l], fox_f_bias[l], fox_q_gain[l],
                   fox_k_gain[l], rwkv_mu[l], rwkv_w0[l], rwkv_w2[l], rwkv_a0[l], rwkv_a2[l],
                   rwkv_g2[l], rwkv_k_k[l], rwkv_k_a[l], rwkv_r_k[l], rwkv_lnx_g[l],
                   rwkv_lnx_b[l], w_out[l], ffn_w_gate[l], ffn_w_up[l], ffn_w_down[l],
                   tm=tm, tq=tq, tk=tk, prec=RWKV_PASSES)
    return x
```
